```python
import math
import jax
import jax.numpy as jnp
from jax import lax
import numpy as np

D_MODEL = 1024
BATCH = 8
SEQ = 8192
DEPTH = 2

N_MIXERS = 4
N_HEADS = 4
HEAD_DIM = 64
MIX_WIDTH = N_HEADS * HEAD_DIM
D_FF = 2816
PLE_DIM = 256
NORM_EPS = 1e-6
NEG_BIG = -1e30
POS_BIG = 1e30
GATE_FLOOR = 1e-20

NSA_CMP_BLOCK = 32
NSA_CMP_STRIDE = 16
NSA_SEL_BLOCK = 64
NSA_TOP_N = 16
NSA_WINDOW = 512
NSA_Q_CHUNK = 64
NSA_CMP_HIDDEN = 128
NSA_WIDTHS = (MIX_WIDTH, HEAD_DIM, HEAD_DIM, HEAD_DIM, HEAD_DIM, HEAD_DIM, HEAD_DIM, 3 * N_HEADS)

HGRN_CHUNK = 32
HGRN_WIDTHS = (MIX_WIDTH, MIX_WIDTH, MIX_WIDTH, MIX_WIDTH)

RET_CHUNK = 64
RET_ROPE_BASE = 10000.0
RET_GN_EPS = 1e-5
RET_WIDTHS = (MIX_WIDTH, MIX_WIDTH, MIX_WIDTH, MIX_WIDTH)

RWKV_W_RANK = 64
RWKV_A_RANK = 64
RWKV_G_RANK = 128
RWKV_GN_EPS = 64e-5
RWKV_WIDTHS = (MIX_WIDTH, MIX_WIDTH, MIX_WIDTH, RWKV_W_RANK, RWKV_A_RANK, RWKV_G_RANK)
RWKV_IN = sum(RWKV_WIDTHS)

GROUP_WIDTHS = (sum(NSA_WIDTHS), sum(HGRN_WIDTHS), sum(RET_WIDTHS), RWKV_IN)
D_IN = sum(GROUP_WIDTHS)

kernel_name = 'hybrid_nsa_hgrn2_retnet_rwkv7_block'


def _split(a, widths):
    out = []
    off = 0
    for w in widths:
        out.append(a[..., off:off + w])
        off += w
    return out


def _rmsnorm(x, g):
    xf = x.astype(jnp.float32)
    y = xf * lax.rsqrt(jnp.mean(xf * xf, axis=-1, keepdims=True) + NORM_EPS)
    return (y * g.astype(jnp.float32)).astype(x.dtype)


def _head_rmsnorm(o, g):
    b, s, h, d = o.shape
    y = o * lax.rsqrt(jnp.mean(o * o, axis=-1, keepdims=True) + NORM_EPS)
    return y.reshape(b, s, h * d) * g.astype(jnp.float32)


def _head_layernorm(o, g, beta, eps):
    b, s, h, d = o.shape
    mu = jnp.mean(o, axis=-1, keepdims=True)
    var = jnp.mean(jnp.square(o - mu), axis=-1, keepdims=True)
    y = ((o - mu) * lax.rsqrt(var + eps)).reshape(b, s, h * d)
    return y * g.astype(jnp.float32) + beta.astype(jnp.float32)


def _nsa_mixer(u, pos_k, pos_v, cmp_k1, cmp_k2, cmp_v1, cmp_v2):
    u = u.astype(jnp.float32)
    bsz, seq, _ = u.shape
    q, k_c, v_c, k_s, v_s, k_w, v_w, g_logit = _split(u, NSA_WIDTHS)
    q = q.reshape(bsz, seq, N_HEADS, HEAD_DIM) * (HEAD_DIM ** -0.5)
    gates = jax.nn.sigmoid(g_logit).reshape(bsz, seq, 3, N_HEADS)

    n_cmp = (seq - NSA_CMP_BLOCK) // NSA_CMP_STRIDE + 1
    cmp_start = jnp.arange(n_cmp) * NSA_CMP_STRIDE
    blk_idx = cmp_start[:, None] + jnp.arange(NSA_CMP_BLOCK)[None, :]

    def compress(kv, pos, w1, w2):
        blk = kv[:, blk_idx] + pos.astype(jnp.float32)
        hid = jax.nn.silu(blk.reshape(bsz, n_cmp, NSA_CMP_BLOCK * HEAD_DIM) @ w1)
        return hid @ w2

    k_cmp = compress(k_c, pos_k, cmp_k1, cmp_k2)
    v_cmp = compress(v_c, pos_v, cmp_v1, cmp_v2)
    cmp_end = cmp_start + NSA_CMP_BLOCK - 1

    n_sel = seq // NSA_SEL_BLOCK
    n_top = min(NSA_TOP_N, n_sel)
    sel_start = jnp.arange(n_sel) * NSA_SEL_BLOCK
    overlap = ((cmp_start[:, None] < sel_start[None, :] + NSA_SEL_BLOCK)
               & (cmp_start[:, None] + NSA_CMP_BLOCK > sel_start[None, :])).astype(jnp.float32)
    k_blocks = k_s.reshape(bsz, n_sel, NSA_SEL_BLOCK, HEAD_DIM)
    v_blocks = v_s.reshape(bsz, n_sel, NSA_SEL_BLOCK, HEAD_DIM)
    blk_ids = jnp.arange(n_sel)
    gather = jax.vmap(lambda blocks, ix: blocks[ix])

    k_win = jnp.pad(k_w, ((0, 0), (NSA_WINDOW, 0), (0, 0)))
    v_win = jnp.pad(v_w, ((0, 0), (NSA_WINDOW, 0), (0, 0)))

    def query_chunk(c):
        t0 = c * NSA_Q_CHUNK
        qc = lax.dynamic_slice_in_dim(q, t0, NSA_Q_CHUNK, axis=1)
        gc = lax.dynamic_slice_in_dim(gates, t0, NSA_Q_CHUNK, axis=1)
        t = t0 + jnp.arange(NSA_Q_CHUNK)
        m_cmp = cmp_end[None, :] <= t[:, None]
        s_cmp = jnp.einsum('bqhd,bnd->bhqn', qc, k_cmp)
        p_cmp = jax.nn.softmax(jnp.where(m_cmp, s_cmp, NEG_BIG), axis=-1) * m_cmp
        o_cmp = jnp.einsum('bhqn,bnd->bqhd', p_cmp, v_cmp)
        importance = jnp.einsum('bhqn,ns->bqs', p_cmp, overlap)
        cur = t // NSA_SEL_BLOCK
        allowed = blk_ids[None, :] <= cur[:, None]
        forced = ((blk_ids[None, :] == 0) | (blk_ids[None, :] == cur[:, None])
                  | (blk_ids[None, :] == cur[:, None] - 1))
        score = jnp.where(forced, POS_BIG, jnp.where(allowed, importance, NEG_BIG))
        _, sel = lax.top_k(score, n_top)
        k_sel = gather(k_blocks, sel)
        v_sel = gather(v_blocks, sel)
        key_pos = sel[..., None] * NSA_SEL_BLOCK + jnp.arange(NSA_SEL_BLOCK)
        m_sel = (key_pos <= t[None, :, None, None])[:, :, None]
        s_sel = jnp.where(m_sel, jnp.einsum('bqhd,bqnkd->bqhnk', qc, k_sel), NEG_BIG)
        p_sel = jax.nn.softmax(s_sel.reshape(bsz, NSA_Q_CHUNK, N_HEADS, n_top * NSA_SEL_BLOCK), axis=-1)
        p_sel = p_sel.reshape(bsz, NSA_Q_CHUNK, N_HEADS, n_top, NSA_SEL_BLOCK)
        o_sel = jnp.einsum('bqhnk,bqnkd->bqhd', p_sel, v_sel)
        kw = lax.dynamic_slice_in_dim(k_win, t0, NSA_Q_CHUNK + NSA_WINDOW, axis=1)
        vw = lax.dynamic_slice_in_dim(v_win, t0, NSA_Q_CHUNK + NSA_WINDOW, axis=1)
        s_pos = t0 - NSA_WINDOW + jnp.arange(NSA_Q_CHUNK + NSA_WINDOW)
        dist = t[:, None] - s_pos[None, :]
        m_win = (dist >= 0) & (dist < NSA_WINDOW) & (s_pos[None, :] >= 0)
        s_win = jnp.einsum('bqhd,bkd->bhqk', qc, kw)
        p_win = jax.nn.softmax(jnp.where(m_win, s_win, NEG_BIG), axis=-1)
        o_win = jnp.einsum('bhqk,bkd->bqhd', p_win, vw)
        return (gc[:, :, 0, :, None] * o_cmp + gc[:, :, 1, :, None] * o_sel
                + gc[:, :, 2, :, None] * o_win)

    out = lax.map(query_chunk, jnp.arange(seq // NSA_Q_CHUNK))
    return out.transpose(1, 0, 2, 3, 4).reshape(bsz, seq, MIX_WIDTH)


def _chunk_gated_linear(q, k, v, log_f, chunk):
    bsz, seq, nh, dk = q.shape
    dv = v.shape[-1]
    n = seq // chunk

    def to_chunks(a):
        return a.reshape(bsz, n, chunk, nh, a.shape[-1]).transpose(1, 0, 3, 2, 4)

    causal = jnp.tril(jnp.ones((chunk, chunk), bool))[:, :, None]

    def step(state, inp):
        qc, kc, vc, gc = inp
        b = jnp.cumsum(gc, axis=2)
        o_inter = jnp.einsum('bhtd,bhde->bhte', qc * jnp.exp(b), state)
        diff = b[:, :, :, None, :] - b[:, :, None, :, :]
        decay = jnp.where(causal, jnp.exp(jnp.where(causal, diff, 0.0)), 0.0)
        attn = jnp.einsum('bhtd,bhsd,bhtsd->bhts', qc, kc, decay)
        o = o_inter + jnp.einsum('bhts,bhse->bhte', attn, vc)
        b_last = b[:, :, -1:, :]
        state = (jnp.exp(b_last[:, :, 0, :])[..., None] * state
                 + jnp.einsum('bhsd,bhse->bhde', kc * jnp.exp(b_last - b), vc))
        return state, o

    init = jnp.zeros((bsz, nh, dk, dv), jnp.float32)
    _, o = lax.scan(step, init, (to_chunks(q), to_chunks(k), to_chunks(v), to_chunks(log_f)))
    return o.transpose(1, 0, 3, 2, 4).reshape(bsz, seq, nh, dv)


def _hgrn2_mixer(u, lower_bound, norm_g):
    u = u.astype(jnp.float32)
    bsz, seq, _ = u.shape
    q, f_logit, i_in, o_gate = _split(u, HGRN_WIDTHS)
    lb = lower_bound.astype(jnp.float32)
    f = lb + (1.0 - lb) * jax.nn.sigmoid(f_logit)
    log_f = jnp.log(jnp.maximum(f, GATE_FLOOR))
    k = 1.0 - f
    q = jax.nn.silu(q)

    def heads(a):
        return a.reshape(bsz, seq, N_HEADS, HEAD_DIM)

    o = _chunk_gated_linear(heads(q), heads(k), heads(i_in), heads(log_f), HGRN_CHUNK)
    return _head_rmsnorm(o, norm_g) * jax.nn.silu(o_gate)


def _rope(a, cos, sin):
    half = a.shape[-1] // 2
    a1, a2 = a[..., :half], a[..., half:]
    return jnp.concatenate([a1 * cos - a2 * sin, a1 * sin + a2 * cos], axis=-1)


def _retention_mixer(u, norm_g, norm_b):
    u = u.astype(jnp.float32)
    bsz, seq, _ = u.shape
    q, k, v, g = _split(u, RET_WIDTHS)
    q = q.reshape(bsz, seq, N_HEADS, HEAD_DIM)
    k = k.reshape(bsz, seq, N_HEADS, HEAD_DIM)
    v = v.reshape(bsz, seq, N_HEADS, HEAD_DIM)
    pos = jnp.arange(seq, dtype=jnp.float32)
    inv_freq = RET_ROPE_BASE ** (-jnp.arange(0, HEAD_DIM, 2, dtype=jnp.float32) / HEAD_DIM)
    ang = pos[:, None] * inv_freq[None, :]
    cos, sin = jnp.cos(ang)[:, None, :], jnp.sin(ang)[:, None, :]
    q = _rope(q, cos, sin)
    k = _rope(k, cos, sin) * (HEAD_DIM ** -0.5)
    log_gamma = jnp.log(1.0 - jnp.exp2(-5.0 - jnp.arange(N_HEADS, dtype=jnp.float32)))

    n = seq // RET_CHUNK
    qc = q.reshape(bsz, n, RET_CHUNK, N_HEADS, HEAD_DIM)
    kc = k.reshape(bsz, n, RET_CHUNK, N_HEADS, HEAD_DIM)
    vc = v.reshape(bsz, n, RET_CHUNK, N_HEADS, HEAD_DIM)
    i = jnp.arange(RET_CHUNK, dtype=jnp.float32)
    dpos = i[:, None] - i[None, :]
    decay_mask = jnp.where(dpos >= 0, jnp.exp(jnp.maximum(dpos, 0.0)[None] * log_gamma[:, None, None]), 0.0)
    s = jnp.einsum('bnthd,bnshd->bnhts', qc, kc) * decay_mask
    o_intra = jnp.einsum('bnhts,bnshe->bnthe', s, vc)

    k_decay = jnp.exp((RET_CHUNK - 1.0 - i)[:, None] * log_gamma[None, :])
    kv = jnp.einsum('bnshd,sh,bnshe->nbhde', kc, k_decay, vc)
    chunk_decay = jnp.exp(RET_CHUNK * log_gamma)

    def carry_state(state, kv_c):
        return chunk_decay[None, :, None, None] * state + kv_c, state

    _, states = lax.scan(carry_state, jnp.zeros((bsz, N_HEADS, HEAD_DIM, HEAD_DIM), jnp.float32), kv)
    q_decay = jnp.exp((i + 1.0)[:, None] * log_gamma[None, :])
    o_inter = jnp.einsum('bnthd,th,nbhde->bnthe', qc, q_decay, states)
    o = (o_intra + o_inter).reshape(bsz, seq, N_HEADS, HEAD_DIM)
    return _head_layernorm(o, norm_g, norm_b, RET_GN_EPS) * jax.nn.silu(g)


def _rwkv7_mixer(u, mu, w0, w_up, a0, a_up, g_up, k_k, k_a, r_k, norm_g, norm_b):
    u = u.astype(jnp.float32)
    bsz, seq, _ = u.shape
    u_prev = jnp.pad(u[:, :-1], ((0, 0), (1, 0), (0, 0)))
    u = u + mu * (u_prev - u)
    r, k, v, w_lo, a_lo, g_lo = _split(u, RWKV_WIDTHS)
    decay = jnp.exp(-math.exp(-0.5) * jax.nn.sigmoid(w0 + jnp.tanh(w_lo) @ w_up))
    a = jax.nn.sigmoid(a0 + a_lo @ a_up)
    g = jax.nn.sigmoid(g_lo) @ g_up

    def heads(t):
        return t.reshape(bsz, seq, N_HEADS, HEAD_DIM)

    kk = heads(k * k_k)
    kk = kk * lax.rsqrt(jnp.maximum(jnp.sum(kk * kk, axis=-1, keepdims=True), 1e-24))
    k = k * (1.0 + (a - 1.0) * k_a)
    r_h, k_h, v_h, w_h, a_h = heads(r), heads(k), heads(v), heads(decay), heads(a)

    def step(state, inp):
        r_t, w_t, k_t, v_t, kk_t, a_t = inp
        sa = jnp.einsum('bhvk,bhk->bhv', state, -kk_t)
        state = (state * w_t[:, :, None, :] + sa[..., None] * (kk_t * a_t)[:, :, None, :]
                 + v_t[..., None] * k_t[:, :, None, :])
        return state, jnp.einsum('bhvk,bhk->bhv', state, r_t)

    def time_major(t):
        return t.transpose(1, 0, 2, 3)

    init = jnp.zeros((bsz, N_HEADS, HEAD_DIM, HEAD_DIM), jnp.float32)
    _, y = lax.scan(step, init, (time_major(r_h), time_major(w_h), time_major(k_h),
                                 time_major(v_h), time_major(kk), time_major(a_h)))
    y = _head_layernorm(time_major(y), norm_g, norm_b, RWKV_GN_EPS)
    bonus = jnp.sum(r_h * k_h * r_k.reshape(N_HEADS, HEAD_DIM), axis=-1, keepdims=True) * v_h
    return (y + bonus.reshape(bsz, seq, MIX_WIDTH)) * g


def setup_inputs(seed: int = 0) -> dict:
    key = jax.random.key(seed)
    keys = iter(jax.random.split(key, 48))

    def normal(shape, scale):
        return scale * jax.random.normal(next(keys), shape, jnp.float32)

    def gain(shape):
        return 1.0 + normal(shape, 0.02)

    L = DEPTH
    cmp_in = NSA_CMP_BLOCK * HEAD_DIM
    return {
        'x': normal((BATCH, SEQ, D_MODEL), 1.0),
        'p': normal((DEPTH, BATCH, SEQ, PLE_DIM), 1.0),
        'norm_mix': gain((L, D_MODEL)),
        'w_in': normal((L, D_MODEL, D_IN), D_MODEL ** -0.5),
        'nsa_pos_k': normal((L, NSA_CMP_BLOCK, HEAD_DIM), 0.02),
        'nsa_pos_v': normal((L, NSA_CMP_BLOCK, HEAD_DIM), 0.02),
        'nsa_cmp_k1': normal((L, cmp_in, NSA_CMP_HIDDEN), cmp_in ** -0.5),
        'nsa_cmp_k2': normal((L, NSA_CMP_HIDDEN, HEAD_DIM), NSA_CMP_HIDDEN ** -0.5),
        'nsa_cmp_v1': normal((L, cmp_in, NSA_CMP_HIDDEN), cmp_in ** -0.5),
        'nsa_cmp_v2': normal((L, NSA_CMP_HIDDEN, HEAD_DIM), NSA_CMP_HIDDEN ** -0.5),
        'hgrn_lb_logits': normal((L, MIX_WIDTH), 0.5),
        'hgrn_norm': gain((L, MIX_WIDTH)),
        'ret_norm_g': gain((L, MIX_WIDTH)),
        'ret_norm_b': normal((L, MIX_WIDTH), 0.02),
        'rwkv_mu': jax.random.uniform(next(keys), (L, RWKV_IN), jnp.float32, minval=0.2, maxval=0.8),
        'rwkv_w0': normal((L, MIX_WIDTH), 0.5),
        'rwkv_w_up': normal((L, RWKV_W_RANK, MIX_WIDTH), 0.5 * RWKV_W_RANK ** -0.5),
        'rwkv_a0': normal((L, MIX_WIDTH), 0.1),
        'rwkv_a_up': normal((L, RWKV_A_RANK, MIX_WIDTH), 0.5 * RWKV_A_RANK ** -0.5),
        'rwkv_g_up': normal((L, RWKV_G_RANK, MIX_WIDTH), RWKV_G_RANK ** -0.5),
        'rwkv_k_k': 0.85 + normal((L, MIX_WIDTH), 0.02),
        'rwkv_k_a': 1.0 + normal((L, MIX_WIDTH), 0.02),
        'rwkv_r_k': normal((L, MIX_WIDTH), 0.1),
        'rwkv_norm_g': gain((L, MIX_WIDTH)),
        'rwkv_norm_b': normal((L, MIX_WIDTH), 0.02),
        'w_branch': normal((L, N_MIXERS, MIX_WIDTH, D_MODEL), MIX_WIDTH ** -0.5),
        'w_gate': normal((L, N_MIXERS, D_MODEL, D_MODEL), D_MODEL ** -0.5),
        'b_gate': normal((L, N_MIXERS, D_MODEL), 0.01),
        'w_out': normal((L, D_MODEL, D_MODEL), D_MODEL ** -0.5),
        'norm_ffn': gain((L, D_MODEL)),
        'w_ffn_gate': normal((L, D_MODEL, D_FF), D_MODEL ** -0.5),
        'w_ffn_up': normal((L, D_MODEL, D_FF), D_MODEL ** -0.5),
        'w_ffn_down': normal((L, D_FF, D_MODEL), D_FF ** -0.5),
        'norm_ple': gain((L, D_MODEL)),
        'w_ple_gate': normal((L, D_MODEL, D_MODEL), D_MODEL ** -0.5),
        'w_ple_proj': normal((L, PLE_DIM, D_MODEL), PLE_DIM ** -0.5),
        'norm_final': gain((D_MODEL,)),
    }


def reference(x, p, norm_mix, w_in, nsa_pos_k, nsa_pos_v, nsa_cmp_k1, nsa_cmp_k2, nsa_cmp_v1,
              nsa_cmp_v2, hgrn_lb_logits, hgrn_norm, ret_norm_g, ret_norm_b, rwkv_mu, rwkv_w0,
              rwkv_w_up, rwkv_a0, rwkv_a_up, rwkv_g_up, rwkv_k_k, rwkv_k_a, rwkv_r_k, rwkv_norm_g,
              rwkv_norm_b, w_branch, w_gate, b_gate, w_out, norm_ffn, w_ffn_gate, w_ffn_up,
              w_ffn_down, norm_ple, w_ple_gate, w_ple_proj, norm_final):
    lb_soft = jax.nn.softmax(hgrn_lb_logits.astype(jnp.float32), axis=0)
    lower_bounds = jnp.cumsum(lb_soft, axis=0) - lb_soft[0]

    h = x
    for i in range(DEPTH):
        xn = _rmsnorm(h, norm_mix[i])
        u = xn @ w_in[i]
        u_nsa, u_hgrn, u_ret, u_rwkv = _split(u, GROUP_WIDTHS)
        branch_outs = (
            _nsa_mixer(u_nsa, nsa_pos_k[i], nsa_pos_v[i], nsa_cmp_k1[i], nsa_cmp_k2[i],
                       nsa_cmp_v1[i], nsa_cmp_v2[i]),
            _hgrn2_mixer(u_hgrn, lower_bounds[i], hgrn_norm[i]),
            _retention_mixer(u_ret, ret_norm_g[i], ret_norm_b[i]),
            _rwkv7_mixer(u_rwkv, rwkv_mu[i], rwkv_w0[i], rwkv_w_up[i], rwkv_a0[i], rwkv_a_up[i],
                         rwkv_g_up[i], rwkv_k_k[i], rwkv_k_a[i], rwkv_r_k[i], rwkv_norm_g[i],
                         rwkv_norm_b[i]),
        )
        merged = None
        for m in range(N_MIXERS):
            gate = jax.nn.sigmoid(xn @ w_gate[i, m] + b_gate[i, m])
            term = gate * (branch_outs[m] @ w_branch[i, m])
            merged = term if merged is None else merged + term
        h = h + (merged @ w_out[i]).astype(h.dtype)

        hn = _rmsnorm(h, norm_ffn[i])
        ff = (jax.nn.silu(hn @ w_ffn_gate[i]) * (hn @ w_ffn_up[i])) @ w_ffn_down[i]
        h = h + ff.astype(h.dtype)

        hp = _rmsnorm(h, norm_ple[i])
        ple = jax.nn.sigmoid(hp @ w_ple_gate[i]) * (p[i] @ w_ple_proj[i])
        h = h + ple.astype(h.dtype)
    return _rmsnorm(h, norm_final)
```

```python
import functools
import math

import jax
import jax.numpy as jnp
from jax import lax
from jax.experimental import pallas as pl
from jax.experimental.pallas import tpu as pltpu

F32 = jnp.float32
BF16 = jnp.bfloat16

D_MODEL = 1024
N_HEADS = 4
HEAD_DIM = 64
MIX = N_HEADS * HEAD_DIM
D_FF = 2816
PLE_DIM = 256
NORM_EPS = 1e-6
NEG_BIG = -1e30
POS_BIG = 1e30
GATE_FLOOR = 1e-20

NSA_CMP_BLOCK = 32
NSA_CMP_STRIDE = 16
NSA_SEL_BLOCK = 64
NSA_TOP_N = 16
NSA_WINDOW = 512
NSA_CMP_HIDDEN = 128
NSA_WIDTH = 652
NSA_PAD = 768

RET_ROPE_BASE = 10000.0
RET_GN_EPS = 1e-5
RWKV_GN_EPS = 64e-5

VMEM_LIMIT = 56 * 1024 * 1024


def _bdot(a, b):
    return jnp.dot(a.astype(BF16), b.astype(BF16), preferred_element_type=F32)


def _bdot_nt(a, b):
    return lax.dot_general(a.astype(BF16), b.astype(BF16), (((1,), (1,)), ((), ())),
                           preferred_element_type=F32)


def _fdot(a, b):
    return jnp.dot(a, b, precision=lax.Precision.HIGHEST, preferred_element_type=F32)


def _fdot_nt(a, b):
    return lax.dot_general(a, b, (((1,), (1,)), ((), ())), precision=lax.Precision.HIGHEST,
                           preferred_element_type=F32)


def _sigmoid(x):
    return 1.0 / (1.0 + jnp.exp(-x))


def _silu(x):
    return x * _sigmoid(x)


def _head_sum(x):
    parts = []
    for h in range(N_HEADS):
        s = jnp.sum(x[:, h * HEAD_DIM:(h + 1) * HEAD_DIM], axis=-1, keepdims=True)
        parts.append(jnp.broadcast_to(s, (x.shape[0], HEAD_DIM)))
    return jnp.concatenate(parts, axis=-1)


def _params(sem):
    return pltpu.CompilerParams(dimension_semantics=sem, vmem_limit_bytes=VMEM_LIMIT)


def _row_spec(tile, width):
    return pl.BlockSpec((None, tile, width), lambda b, c: (b, c, 0))


def _const_spec(shape):
    nd = len(shape)
    return pl.BlockSpec(shape, lambda b, c: (0,) * nd)


RWKV_TB = 256
RWKV_C = 32


def _rwkv_kernel(u_ref, mu_ref, w0_ref, wup_ref, a0_ref, aup_ref, gup_ref, kk_ref, ka_ref,
                 rk_ref, ng_ref, nb_ref, o_ref, state_ref, prev_ref):
    c = pl.program_id(1)

    @pl.when(c == 0)
    def _():
        state_ref[...] = jnp.zeros_like(state_ref)
        prev_ref[...] = jnp.zeros_like(prev_ref)

    u = u_ref[...]
    tb = u.shape[0]
    row = lax.broadcasted_iota(jnp.int32, u.shape, 0)
    u_prev = jnp.where(row == 0, prev_ref[...], pltpu.roll(u, 1, axis=0))
    prev_ref[...] = u[tb - 1:tb, :]
    xs = u + mu_ref[...] * (u_prev - u)
    r = xs[:, 0:MIX]
    k = xs[:, MIX:2 * MIX]
    v = xs[:, 2 * MIX:3 * MIX]
    w_lo = xs[:, 3 * MIX:3 * MIX + 64]
    a_lo = xs[:, 3 * MIX + 64:3 * MIX + 128]
    g_lo = xs[:, 3 * MIX + 128:3 * MIX + 256]

    logw = -math.exp(-0.5) * _sigmoid(w0_ref[...] + _bdot(jnp.tanh(w_lo), wup_ref[...]))
    a = _sigmoid(a0_ref[...] + _bdot(a_lo, aup_ref[...]))
    g = _bdot(_sigmoid(g_lo), gup_ref[...])
    kk = k * kk_ref[...]
    kk = kk * lax.rsqrt(jnp.maximum(_head_sum(kk * kk), 1e-24))
    k2 = k * (1.0 + (a - 1.0) * ka_ref[...])
    alpha = -kk
    beta = kk * a
    bonus = _head_sum(r * k2 * rk_ref[...]) * v

    C = RWKV_C
    ri = lax.broadcasted_iota(jnp.int32, (C, C), 0)
    ci = lax.broadcasted_iota(jnp.int32, (C, C), 1)
    tril_incl = (ri >= ci)
    tril_strict = (ri > ci)
    tril_f = tril_incl.astype(F32)
    eye_c = (ri == ci).astype(F32)
    rk_i = lax.broadcasted_iota(jnp.int32, (HEAD_DIM, HEAD_DIM), 0)
    ck_i = lax.broadcasted_iota(jnp.int32, (HEAD_DIM, HEAD_DIM), 1)
    eye_k = (rk_i == ck_i)

    states = [state_ref[h] for h in range(N_HEADS)]
    y_chunks = []
    for ch in range(tb // C):
        sl = slice(ch * C, (ch + 1) * C)
        lw = logw[sl]
        cum = _fdot(tril_f, lw)
        cum_last = cum[C - 1:C, :]
        e_cum = jnp.exp(cum)
        e_prev = jnp.exp(cum - lw)
        e_inv = jnp.exp(-cum)
        e_last = jnp.exp(cum_last - cum)
        gam = jnp.exp(cum_last)
        ag = alpha[sl] * e_prev
        rg = r[sl] * e_cum
        bi = beta[sl] * e_inv
        ki = k2[sl] * e_inv
        bl = beta[sl] * e_last
        kl = k2[sl] * e_last
        vc = v[sl]
        y_heads = []
        for h in range(N_HEADS):
            hs = slice(h * HEAD_DIM, (h + 1) * HEAD_DIM)
            ag_h, rg_h, bi_h, ki_h, bl_h, kl_h, v_h = (t[:, hs] for t in (ag, rg, bi, ki, bl, kl, vc))
            a_ab = jnp.where(tril_strict, _fdot_nt(ag_h, bi_h), 0.0)
            a_ak = jnp.where(tril_strict, _fdot_nt(ag_h, ki_h), 0.0)
            a_rb = jnp.where(tril_incl, _fdot_nt(rg_h, bi_h), 0.0)
            a_rk = jnp.where(tril_incl, _fdot_nt(rg_h, ki_h), 0.0)
            pw = a_ab
            t_inv = eye_c + pw
            for _ in range(int(math.log2(C)) - 1):
                pw = _fdot(pw, pw)
                t_inv = t_inv + _fdot(t_inv, pw)
            w1 = _fdot(t_inv, ag_h)
            w2 = _fdot(t_inv, _fdot(a_ak, v_h))
            rq = rg_h + _fdot(a_rb, w1)
            y0 = _fdot(a_rb, w2) + _fdot(a_rk, v_h)
            bl_t = bl_h.T
            m_mat = jnp.where(eye_k, gam[:, hs], 0.0) + _fdot(bl_t, w1)
            n_mat = _fdot(bl_t, w2) + _fdot(kl_h.T, v_h)
            h0 = states[h]
            y_heads.append(_fdot(rq, h0) + y0)
            states[h] = _fdot(m_mat, h0) + n_mat
        y_chunks.append(jnp.concatenate(y_heads, axis=-1))
    for h in range(N_HEADS):
        state_ref[h] = states[h]
    y = jnp.concatenate(y_chunks, axis=0)

    mean = _head_sum(y) * (1.0 / HEAD_DIM)
    yc = y - mean
    var = _head_sum(yc * yc) * (1.0 / HEAD_DIM)
    yn = yc * lax.rsqrt(var + RWKV_GN_EPS) * ng_ref[...] + nb_ref[...]
    o_ref[...] = (yn + bonus) * g


def _rwkv_call(u, mu, w0, w_up, a0, a_up, g_up, k_k, k_a, r_k, norm_g, norm_b):
    bsz, seq, _ = u.shape
    tb = min(RWKV_TB, seq)
    vec = lambda t: t.reshape(1, -1).astype(F32)
    args = (u, vec(mu), vec(w0), w_up, vec(a0), a_up, g_up, vec(k_k), vec(k_a), vec(r_k),
            vec(norm_g), vec(norm_b))
    in_specs = [_row_spec(tb, 4 * MIX)] + [_const_spec(t.shape) for t in args[1:]]
    return pl.pallas_call(
        _rwkv_kernel,
        grid=(bsz, seq // tb),
        in_specs=in_specs,
        out_specs=_row_spec(tb, MIX),
        out_shape=jax.ShapeDtypeStruct((bsz, seq, MIX), F32),
        scratch_shapes=[pltpu.VMEM((N_HEADS, HEAD_DIM, HEAD_DIM), F32),
                        pltpu.VMEM((1, 4 * MIX), F32)],
        compiler_params=_params(("parallel", "arbitrary")),
    )(*args)


NSA_TQ = 128
GROUP = NSA_CMP_STRIDE


def _nsa_compress_kernel(xk_ref, xv_ref, pk_ref, pv_ref, k1_ref, k2_ref, v1_ref, v2_ref, o_ref):
    half = GROUP * HEAD_DIM

    def compress(x, pos, w1_ref, w2_ref):
        n = x.shape[0]
        first = _bdot(x + pos[:, :half], w1_ref[0:half, :])
        second = _bdot(x + pos[:, half:], w1_ref[half:2 * half, :])
        hid = first + pltpu.roll(second, n - 1, axis=0)
        return _bdot(_silu(hid), w2_ref[...])

    kc = compress(xk_ref[...], pk_ref[...], k1_ref, k2_ref)
    vc = compress(xv_ref[...], pv_ref[...], v1_ref, v2_ref)
    o_ref[...] = jnp.concatenate([kc, vc], axis=-1)


def _nsa_compress_call(xk, xv, pos_k, pos_v, k1, k2, v1, v2):
    bsz, ng, width = xk.shape
    args = (xk, xv, pos_k.reshape(1, -1), pos_v.reshape(1, -1), k1.astype(BF16), k2.astype(BF16),
            v1.astype(BF16), v2.astype(BF16))
    blk = pl.BlockSpec((None, ng, width), lambda b: (b, 0, 0))
    const = lambda t: pl.BlockSpec(t.shape, lambda b: (0,) * t.ndim)
    return pl.pallas_call(
        _nsa_compress_kernel,
        grid=(bsz,),
        in_specs=[blk, blk] + [const(t) for t in args[2:]],
        out_specs=pl.BlockSpec((None, ng, 2 * HEAD_DIM), lambda b: (b, 0, 0)),
        out_shape=jax.ShapeDtypeStruct((bsz, ng, 2 * HEAD_DIM), F32),
        compiler_params=_params(("parallel",)),
    )(*args)


def _nsa_kernel(q_ref, g_ref, kvs_ref, kvw_ref, kvc_ref, ov_ref, o_ref, *, n_top):
    c = pl.program_id(1)
    tq = q_ref.shape[0]
    tk = tq
    nc = kvc_ref.shape[0]
    ns = ov_ref.shape[1]
    t0 = c * tq
    rows = N_HEADS * tq

    q = q_ref[...] * (HEAD_DIM ** -0.5)
    qs = jnp.concatenate([q[:, h * HEAD_DIM:(h + 1) * HEAD_DIM] for h in range(N_HEADS)], axis=0)
    qs = qs.astype(BF16)
    t_q = t0 + lax.broadcasted_iota(jnp.int32, (tq, 1), 0)
    t_row = jnp.concatenate([t_q] * N_HEADS, axis=0)

    kvc = kvc_ref[...]
    s = _bdot_nt(qs, kvc[:, :HEAD_DIM])
    n_idx = lax.broadcasted_iota(jnp.int32, (1, nc), 1)
    valid = (n_idx * NSA_CMP_STRIDE + (NSA_CMP_BLOCK - 1)) <= t_row
    s = jnp.where(valid, s, NEG_BIG)
    e = jnp.exp(s - jnp.max(s, axis=-1, keepdims=True))
    p = jnp.where(valid, e / jnp.sum(e, axis=-1, keepdims=True), 0.0)
    o_cmp = _bdot(p, kvc[:, HEAD_DIM:])
    ph = p[0:tq]
    for h in range(1, N_HEADS):
        ph = ph + p[h * tq:(h + 1) * tq]
    ph_hi = ph.astype(BF16)
    ph_lo = (ph - ph_hi.astype(F32)).astype(BF16)
    ov = ov_ref[...]
    imp = (jnp.dot(ph_hi, ov, preferred_element_type=F32)
           + jnp.dot(ph_lo, ov, preferred_element_type=F32))

    blk = lax.broadcasted_iota(jnp.int32, (1, ns), 1)
    cur = t_q // NSA_SEL_BLOCK
    forced = (blk == 0) | (blk == cur) | (blk == cur - 1)
    score = jnp.where(forced, POS_BIG, jnp.where(blk <= cur, imp, NEG_BIG))
    score_t = score.T
    blk_col = lax.broadcasted_iota(jnp.int32, (ns, 1), 0)
    rank = jnp.zeros((ns, tq), F32)
    for j in range(ns):
        other = score_t[j:j + 1, :]
        beats = (other > score_t) | ((other == score_t) & (blk_col > j))
        rank = rank + beats.astype(F32)
    sel = (rank < n_top).astype(F32).T.astype(BF16)

    blk_row = lax.broadcasted_iota(jnp.int32, (ns, tk), 0)
    key_lane = lax.broadcasted_iota(jnp.int32, (ns, tk), 1)
    kpos_lane = lax.broadcasted_iota(jnp.int32, (1, tk), 1)

    def attend(kv_ref, j, carry, mask_fn):
        m, l, acc = carry
        start = pl.multiple_of(j * tk, tk)
        kv = kv_ref[pl.ds(start, tk), :]
        sc = _bdot_nt(qs, kv[:, :HEAD_DIM])
        ok = mask_fn(j)
        sc = jnp.where(ok, sc, NEG_BIG)
        m_new = jnp.maximum(m, jnp.max(sc, axis=-1, keepdims=True))
        pr = jnp.where(ok, jnp.exp(sc - m_new), 0.0)
        scale = jnp.exp(m - m_new)
        l = scale * l + jnp.sum(pr, axis=-1, keepdims=True)
        acc = scale * acc + _bdot(pr, kv[:, HEAD_DIM:])
        return m_new, l, acc

    def sel_mask(j):
        expand = ((j * tk + key_lane) // NSA_SEL_BLOCK == blk_row).astype(BF16)
        picked = jnp.dot(sel, expand, preferred_element_type=F32)
        picked = jnp.concatenate([picked] * N_HEADS, axis=0)
        return (picked > 0.5) & ((j * tk + kpos_lane) <= t_row)

    def win_mask(j):
        dist = t_row - (j * tk + kpos_lane)
        return (dist >= 0) & (dist < NSA_WINDOW)

    init = (jnp.full((rows, 1), NEG_BIG, F32), jnp.zeros((rows, 1), F32),
            jnp.zeros((rows, HEAD_DIM), F32))
    m_s, l_s, acc_s = lax.fori_loop(0, c + 1, lambda j, cr: attend(kvs_ref, j, cr, sel_mask), init)
    o_sel = acc_s / l_s
    j0 = jnp.maximum(c - NSA_WINDOW // tk, 0)
    m_w, l_w, acc_w = lax.fori_loop(j0, c + 1, lambda j, cr: attend(kvw_ref, j, cr, win_mask), init)
    o_win = acc_w / l_w

    gates = _sigmoid(g_ref[...])

    def gate_col(branch):
        return jnp.concatenate([gates[:, branch * N_HEADS + h:branch * N_HEADS + h + 1]
                                for h in range(N_HEADS)], axis=0)

    out = gate_col(0) * o_cmp + gate_col(1) * o_sel + gate_col(2) * o_win
    o_ref[...] = jnp.concatenate([out[h * tq:(h + 1) * tq] for h in range(N_HEADS)], axis=-1)


def _nsa_call(u_nsa, pos_k, pos_v, k1, k2, v1, v2):
    bsz, seq, _ = u_nsa.shape
    ng = seq // GROUP
    xk = u_nsa[:, :, MIX:MIX + HEAD_DIM].reshape(bsz, ng, GROUP * HEAD_DIM)
    xv = u_nsa[:, :, MIX + HEAD_DIM:MIX + 2 * HEAD_DIM].reshape(bsz, ng, GROUP * HEAD_DIM)
    kvc = _nsa_compress_call(xk, xv, pos_k, pos_v, k1, k2, v1, v2)

    n_sel = seq // NSA_SEL_BLOCK
    n_top = min(NSA_TOP_N, n_sel)
    cmp_start = jnp.arange(ng) * NSA_CMP_STRIDE
    sel_start = jnp.arange(n_sel) * NSA_SEL_BLOCK
    overlap = ((cmp_start[:, None] < sel_start[None, :] + NSA_SEL_BLOCK)
               & (cmp_start[:, None] + NSA_CMP_BLOCK > sel_start[None, :])).astype(BF16)
    tq = min(NSA_TQ, seq)
    return pl.pallas_call(
        functools.partial(_nsa_kernel, n_top=n_top),
        grid=(bsz, seq // tq),
        in_specs=[
            pl.BlockSpec((None, tq, MIX), lambda b, c: (b, c, 0)),
            pl.BlockSpec((None, tq, 128), lambda b, c: (b, c, 5)),
            pl.BlockSpec((None, seq, 128), lambda b, c: (b, 0, 3)),
            pl.BlockSpec((None, seq, 128), lambda b, c: (b, 0, 4)),
            pl.BlockSpec((None, ng, 128), lambda b, c: (b, 0, 0)),
            pl.BlockSpec((ng, n_sel), lambda b, c: (0, 0)),
        ],
        out_specs=_row_spec(tq, MIX),
        out_shape=jax.ShapeDtypeStruct((bsz, seq, MIX), F32),
        compiler_params=_params(("parallel", "arbitrary")),
    )(u_nsa, u_nsa, u_nsa, u_nsa, kvc, overlap)


HGRN_TB = 256
HGRN_C = 32


def _hgrn_kernel(u_ref, lbl_ref, ng_ref, ones_ref, o_ref, state_ref, *, layer):
    c = pl.program_id(1)

    @pl.when(c == 0)
    def _():
        state_ref[...] = jnp.zeros_like(state_ref)

    logits = lbl_ref[...]
    ex = jnp.exp(logits - jnp.max(logits, axis=0, keepdims=True))
    soft = ex / jnp.sum(ex, axis=0, keepdims=True)
    lb = jnp.sum(soft[0:layer + 1], axis=0, keepdims=True) - soft[0:1]

    u = u_ref[...]
    tb = u.shape[0]
    q = _silu(u[:, 0:MIX])
    f = lb + (1.0 - lb) * _sigmoid(u[:, MIX:2 * MIX])
    logf = jnp.log(jnp.maximum(f, GATE_FLOOR))
    k = 1.0 - f
    v = u[:, 2 * MIX:3 * MIX]
    og = u[:, 3 * MIX:4 * MIX]

    C = HGRN_C
    ri = lax.broadcasted_iota(jnp.int32, (C, C), 0)
    ci = lax.broadcasted_iota(jnp.int32, (C, C), 1)
    tril_f = (ri >= ci).astype(F32)
    row_i = lax.broadcasted_iota(jnp.int32, (C, 1), 0)
    ones_bd = ones_ref[...]

    states = [state_ref[h] for h in range(N_HEADS)]
    o_chunks = []
    for ch in range(tb // C):
        sl = slice(ch * C, (ch + 1) * C)
        qc, kc, vc = q[sl], k[sl], v[sl]
        b = _fdot(tril_f, logf[sl])
        b_last = b[C - 1:C, :]
        pieces = []
        for s in range(C):
            pm = qc * kc[s:s + 1, :] * jnp.exp(jnp.minimum(b - b[s:s + 1, :], 0.0))
            pieces.append(jnp.where(row_i >= s, pm, 0.0).astype(BF16))
        attn = jnp.dot(jnp.concatenate(pieces, axis=0), ones_bd, preferred_element_type=F32)
        o = jnp.zeros((C, MIX), F32)
        for s in range(C):
            o = o + attn[s * C:(s + 1) * C, :] * vc[s:s + 1, :]
        qe = qc * jnp.exp(b)
        kd = kc * jnp.exp(b_last - b)
        g_last = jnp.exp(b_last)
        o_heads = []
        for h in range(N_HEADS):
            hs = slice(h * HEAD_DIM, (h + 1) * HEAD_DIM)
            o_heads.append(_bdot_nt(qe[:, hs], states[h]))
            states[h] = states[h] * g_last[:, hs] + _bdot(vc[:, hs].T, kd[:, hs])
        o_chunks.append(o + jnp.concatenate(o_heads, axis=-1))
    for h in range(N_HEADS):
        state_ref[h] = states[h]
    o = jnp.concatenate(o_chunks, axis=0)
    ms = _head_sum(o * o) * (1.0 / HEAD_DIM)
    o_ref[...] = o * lax.rsqrt(ms + NORM_EPS) * ng_ref[...] * _silu(og)


def _block_ones():
    hid = jnp.arange(MIX) // HEAD_DIM
    return (hid[:, None] == hid[None, :]).astype(BF16)


def _hgrn_call(u, lb_logits, norm_g, layer):
    bsz, seq, _ = u.shape
    tb = min(HGRN_TB, seq)
    args = (u, lb_logits.astype(F32), norm_g.reshape(1, -1).astype(F32), _block_ones())
    return pl.pallas_call(
        functools.partial(_hgrn_kernel, layer=layer),
        grid=(bsz, seq // tb),
        in_specs=[_row_spec(tb, 4 * MIX)] + [_const_spec(t.shape) for t in args[1:]],
        out_specs=_row_spec(tb, MIX),
        out_shape=jax.ShapeDtypeStruct((bsz, seq, MIX), F32),
        scratch_shapes=[pltpu.VMEM((N_HEADS, HEAD_DIM, HEAD_DIM), F32)],
        compiler_params=_params(("parallel", "arbitrary")),
    )(*args)


RET_C = 256


def _ret_kernel(u_ref, cos_ref, sin_ref, dm_ref, qd_ref, kd_ref, cd_ref, ng_ref, nb_ref,
                o_ref, state_ref):
    c = pl.program_id(1)

    @pl.when(c == 0)
    def _():
        state_ref[...] = jnp.zeros_like(state_ref)

    u = u_ref[...]
    cosf = cos_ref[...]
    sins = sin_ref[...]
    half = HEAD_DIM // 2

    def rope(a):
        outs = []
        for j in range(MIX // 128):
            blk = a[:, j * 128:(j + 1) * 128]
            lane = lax.broadcasted_iota(jnp.int32, blk.shape, 1)
            swapped = jnp.where((lane % HEAD_DIM) < half, pltpu.roll(blk, 128 - half, axis=1),
                                pltpu.roll(blk, half, axis=1))
            outs.append(swapped)
        return a * cosf + jnp.concatenate(outs, axis=-1) * sins

    q = rope(u[:, 0:MIX])
    k = rope(u[:, MIX:2 * MIX]) * (HEAD_DIM ** -0.5)
    v = u[:, 2 * MIX:3 * MIX]
    g = u[:, 3 * MIX:4 * MIX]
    qd = q * qd_ref[...]
    kd = k * kd_ref[...]
    cd = cd_ref[...]
    o_heads = []
    for h in range(N_HEADS):
        hs = slice(h * HEAD_DIM, (h + 1) * HEAD_DIM)
        s = _bdot_nt(q[:, hs], k[:, hs]) * dm_ref[h]
        st = state_ref[h]
        o_heads.append(_bdot(s, v[:, hs]) + _bdot(qd[:, hs], st))
        state_ref[h] = st * cd[:, hs] + _bdot(kd[:, hs].T, v[:, hs])
    o = jnp.concatenate(o_heads, axis=-1)
    mean = _head_sum(o) * (1.0 / HEAD_DIM)
    oc = o - mean
    var = _head_sum(oc * oc) * (1.0 / HEAD_DIM)
    y = oc * lax.rsqrt(var + RET_GN_EPS) * ng_ref[...] + nb_ref[...]
    o_ref[...] = y * _silu(g)


def _ret_call(u, norm_g, norm_b):
    bsz, seq, _ = u.shape
    C = min(RET_C, seq)
    half = HEAD_DIM // 2
    pos = jnp.arange(seq, dtype=F32)
    inv_freq = RET_ROPE_BASE ** (-jnp.arange(0, HEAD_DIM, 2, dtype=F32) / HEAD_DIM)
    ang = pos[:, None] * inv_freq[None, :]
    cos, sin = jnp.cos(ang), jnp.sin(ang)
    cosf = jnp.tile(jnp.concatenate([cos, cos], axis=-1), (1, N_HEADS))
    sins = jnp.tile(jnp.concatenate([-sin, sin], axis=-1), (1, N_HEADS))
    log_gamma = jnp.log(1.0 - jnp.exp2(-5.0 - jnp.arange(N_HEADS, dtype=F32)))
    i = jnp.arange(C, dtype=F32)
    dpos = i[:, None] - i[None, :]
    dm = jnp.where(dpos >= 0, jnp.exp(jnp.maximum(dpos, 0.0)[None] * log_gamma[:, None, None]), 0.0)
    lanes = lambda t: jnp.repeat(t, HEAD_DIM, axis=-1)
    qd = lanes(jnp.exp((i + 1.0)[:, None] * log_gamma[None, :]))
    kd = lanes(jnp.exp((C - 1.0 - i)[:, None] * log_gamma[None, :]))
    cd = lanes(jnp.exp(C * log_gamma)[None, :])
    args = (u, cosf, sins, dm, qd, kd, cd, norm_g.reshape(1, -1).astype(F32),
            norm_b.reshape(1, -1).astype(F32))
    in_specs = ([_row_spec(C, 4 * MIX),
                 pl.BlockSpec((C, MIX), lambda b, c: (c, 0)),
                 pl.BlockSpec((C, MIX), lambda b, c: (c, 0))]
                + [_const_spec(t.shape) for t in args[3:]])
    return pl.pallas_call(
        _ret_kernel,
        grid=(bsz, seq // C),
        in_specs=in_specs,
        out_specs=_row_spec(C, MIX),
        out_shape=jax.ShapeDtypeStruct((bsz, seq, MIX), F32),
        scratch_shapes=[pltpu.VMEM((N_HEADS, HEAD_DIM, HEAD_DIM), F32)],
        compiler_params=_params(("parallel", "arbitrary")),
    )(*args)


DENSE_TM = 512
FFN_TF = 256
GROUP_COLS = (NSA_PAD, 4 * MIX, 4 * MIX, 4 * MIX)


def _rmsnorm(x, g):
    return x * lax.rsqrt(jnp.mean(x * x, axis=-1, keepdims=True) + NORM_EPS) * g


def _resident(shape):
    nd = len(shape)
    return pl.BlockSpec(shape, lambda b, c: (0,) * nd, pipeline_mode=pl.Buffered(1))


def _inproj_kernel(h_ref, g_ref, w_ref, nsa_ref, hg_ref, rt_ref, rw_ref):
    xn = _rmsnorm(h_ref[...], g_ref[...]).astype(BF16)
    off = 0
    for ref, width in zip((nsa_ref, hg_ref, rt_ref, rw_ref), GROUP_COLS):
        ref[...] = jnp.dot(xn, w_ref[:, off:off + width], preferred_element_type=F32)
        off += width


def _inproj_call(h, g, w_pad):
    bsz, seq, _ = h.shape
    tm = min(DENSE_TM, seq)
    return pl.pallas_call(
        _inproj_kernel,
        grid=(bsz, seq // tm),
        in_specs=[_row_spec(tm, D_MODEL), _resident((1, D_MODEL)), _resident(w_pad.shape)],
        out_specs=[_row_spec(tm, w) for w in GROUP_COLS],
        out_shape=[jax.ShapeDtypeStruct((bsz, seq, w), F32) for w in GROUP_COLS],
        compiler_params=_params(("parallel", "parallel")),
    )(h, g.reshape(1, -1), w_pad)


def _merge_kernel(h_ref, b0_ref, b1_ref, b2_ref, b3_ref, g_ref, wg_ref, bg_ref, wb_ref, wo_ref,
                  o_ref):
    h = h_ref[...]
    xn = _rmsnorm(h, g_ref[...]).astype(BF16)
    merged = None
    for m, b_ref in enumerate((b0_ref, b1_ref, b2_ref, b3_ref)):
        gate = _sigmoid(jnp.dot(xn, wg_ref[m], preferred_element_type=F32) + bg_ref[m])
        term = gate * _bdot(b_ref[...], wb_ref[m])
        merged = term if merged is None else merged + term
    o_ref[...] = h + _bdot(merged, wo_ref[...])


def _merge_call(h, branches, g, w_gate, b_gate, w_branch, w_out):
    bsz, seq, _ = h.shape
    tm = min(DENSE_TM, seq)
    consts = (g.reshape(1, -1), w_gate, b_gate.reshape(4, 1, D_MODEL), w_branch, w_out)
    return pl.pallas_call(
        _merge_kernel,
        grid=(bsz, seq // tm),
        in_specs=([_row_spec(tm, D_MODEL)] + [_row_spec(tm, MIX)] * 4
                  + [_resident(t.shape) for t in consts]),
        out_specs=_row_spec(tm, D_MODEL),
        out_shape=jax.ShapeDtypeStruct(h.shape, F32),
        compiler_params=_params(("parallel", "parallel")),
    )(h, *branches, *consts)


def _ffn_kernel(h_ref, g_ref, wg_ref, wu_ref, wd_ref, o_ref):
    h = h_ref[...]
    hn = _rmsnorm(h, g_ref[...]).astype(BF16)
    acc = h
    for f in range(0, D_FF, FFN_TF):
        gate = jnp.dot(hn, wg_ref[:, f:f + FFN_TF], preferred_element_type=F32)
        up = jnp.dot(hn, wu_ref[:, f:f + FFN_TF], preferred_element_type=F32)
        acc = acc + _bdot(_silu(gate) * up, wd_ref[f:f + FFN_TF, :])
    o_ref[...] = acc


def _ffn_call(h, g, w_gate, w_up, w_down):
    bsz, seq, _ = h.shape
    tm = min(DENSE_TM, seq)
    consts = (g.reshape(1, -1), w_gate, w_up, w_down)
    return pl.pallas_call(
        _ffn_kernel,
        grid=(bsz, seq // tm),
        in_specs=[_row_spec(tm, D_MODEL)] + [_resident(t.shape) for t in consts],
        out_specs=_row_spec(tm, D_MODEL),
        out_shape=jax.ShapeDtypeStruct(h.shape, F32),
        compiler_params=_params(("parallel", "parallel")),
    )(h, *consts)


def _ple_kernel(h_ref, p_ref, g_ref, wg_ref, wp_ref, gf_ref, o_ref, *, final_norm):
    h = h_ref[...]
    hp = _rmsnorm(h, g_ref[...])
    out = h + _sigmoid(_bdot(hp, wg_ref[...])) * _bdot(p_ref[...], wp_ref[...])
    if final_norm:
        out = _rmsnorm(out, gf_ref[...])
    o_ref[...] = out


def _ple_call(h, p, g, w_gate, w_proj, g_final, final_norm):
    bsz, seq, _ = h.shape
    tm = min(DENSE_TM, seq)
    consts = (g.reshape(1, -1), w_gate, w_proj, g_final.reshape(1, -1))
    return pl.pallas_call(
        functools.partial(_ple_kernel, final_norm=final_norm),
        grid=(bsz, seq // tm),
        in_specs=([_row_spec(tm, D_MODEL), _row_spec(tm, PLE_DIM)]
                  + [_resident(t.shape) for t in consts]),
        out_specs=_row_spec(tm, D_MODEL),
        out_shape=jax.ShapeDtypeStruct(h.shape, F32),
        compiler_params=_params(("parallel", "parallel")),
    )(h, p, *consts)


def kernel(x, p, norm_mix, w_in, nsa_pos_k, nsa_pos_v, nsa_cmp_k1, nsa_cmp_k2, nsa_cmp_v1,
           nsa_cmp_v2, hgrn_lb_logits, hgrn_norm, ret_norm_g, ret_norm_b, rwkv_mu, rwkv_w0,
           rwkv_w_up, rwkv_a0, rwkv_a_up, rwkv_g_up, rwkv_k_k, rwkv_k_a, rwkv_r_k, rwkv_norm_g,
           rwkv_norm_b, w_branch, w_gate, b_gate, w_out, norm_ffn, w_ffn_gate, w_ffn_up,
           w_ffn_down, norm_ple, w_ple_gate, w_ple_proj, norm_final):
    depth = w_in.shape[0]
    w_in_pad = jnp.concatenate(
        [w_in[:, :, :NSA_WIDTH], jnp.zeros((depth, D_MODEL, NSA_PAD - NSA_WIDTH), w_in.dtype),
         w_in[:, :, NSA_WIDTH:]], axis=-1).astype(BF16)
    bf = lambda t: t.astype(BF16)
    h = x
    for i in range(depth):
        u_nsa, u_hgrn, u_ret, u_rwkv = _inproj_call(h, norm_mix[i], w_in_pad[i])
        branches = (
            _nsa_call(u_nsa, nsa_pos_k[i], nsa_pos_v[i], nsa_cmp_k1[i], nsa_cmp_k2[i],
                      nsa_cmp_v1[i], nsa_cmp_v2[i]),
            _hgrn_call(u_hgrn, hgrn_lb_logits, hgrn_norm[i], i),
            _ret_call(u_ret, ret_norm_g[i], ret_norm_b[i]),
            _rwkv_call(u_rwkv, rwkv_mu[i], rwkv_w0[i], rwkv_w_up[i], rwkv_a0[i], rwkv_a_up[i],
                       rwkv_g_up[i], rwkv_k_k[i], rwkv_k_a[i], rwkv_r_k[i], rwkv_norm_g[i],
                       rwkv_norm_b[i]),
        )
        h = _merge_call(h, branches, norm_mix[i], bf(w_gate[i]), b_gate[i], bf(w_branch[i]),
                        bf(w_out[i]))
        h = _ffn_call(h, norm_ffn[i], bf(w_ffn_gate[i]), bf(w_ffn_up[i]), bf(w_ffn_down[i]))
        h = _ple_call(h, p[i], norm_ple[i], bf(w_ple_gate[i]), bf(w_ple_proj[i]), norm_final,
                      final_norm=(i == depth - 1))
    return h
```

```python
import functools
import math

import jax
import jax.numpy as jnp
from jax import lax
from jax.experimental import pallas as pl
from jax.experimental.pallas import tpu as pltpu

F32 = jnp.float32
BF16 = jnp.bfloat16

D_MODEL = 1024
N_HEADS = 4
HEAD_DIM = 64
MIX = N_HEADS * HEAD_DIM
D_FF = 2816
PLE_DIM = 256
NORM_EPS = 1e-6
NEG_BIG = -1e30
POS_BIG = 1e30
GATE_FLOOR = 1e-20

NSA_CMP_BLOCK = 32
NSA_CMP_STRIDE = 16
NSA_SEL_BLOCK = 64
NSA_TOP_N = 16
NSA_WINDOW = 512
NSA_CMP_HIDDEN = 128
NSA_WIDTH = 652
NSA_PAD = 768

RET_ROPE_BASE = 10000.0
RET_GN_EPS = 1e-5
RWKV_GN_EPS = 64e-5

VMEM_LIMIT = 56 * 1024 * 1024


def _bdot(a, b):
    return jnp.dot(a.astype(BF16), b.astype(BF16), preferred_element_type=F32)


def _bdot_nt(a, b):
    return lax.dot_general(a.astype(BF16), b.astype(BF16), (((1,), (1,)), ((), ())),
                           preferred_element_type=F32)


def _fdot(a, b):
    return jnp.dot(a, b, precision=lax.Precision.HIGHEST, preferred_element_type=F32)


def _fdot_nt(a, b):
    return lax.dot_general(a, b, (((1,), (1,)), ((), ())), precision=lax.Precision.HIGHEST,
                           preferred_element_type=F32)


def _sigmoid(x):
    return 1.0 / (1.0 + jnp.exp(-x))


def _silu(x):
    return x * _sigmoid(x)


def _head_sum(x):
    parts = []
    for h in range(N_HEADS):
        s = jnp.sum(x[:, h * HEAD_DIM:(h + 1) * HEAD_DIM], axis=-1, keepdims=True)
        parts.append(jnp.broadcast_to(s, (x.shape[0], HEAD_DIM)))
    return jnp.concatenate(parts, axis=-1)


def _params(sem):
    return pltpu.CompilerParams(dimension_semantics=sem, vmem_limit_bytes=VMEM_LIMIT)


def _row_spec(tile, width):
    return pl.BlockSpec((None, tile, width), lambda b, c: (b, c, 0))


def _const_spec(shape):
    nd = len(shape)
    return pl.BlockSpec(shape, lambda b, c: (0,) * nd)


RWKV_TB = 256
RWKV_C = 32


def _split_dot(a_bf16, x, parts):
    total = None
    rest = x
    for _ in range(parts):
        piece = rest.astype(BF16)
        rest = rest - piece.astype(F32)
        term = jnp.dot(a_bf16, piece, preferred_element_type=F32)
        total = term if total is None else total + term
    return total


def _head_sum_mxu(x, ones_t):
    hi = x.astype(BF16)
    lo = (x - hi.astype(F32)).astype(BF16)
    return (jnp.dot(hi, ones_t, preferred_element_type=F32)
            + jnp.dot(lo, ones_t, preferred_element_type=F32))


def _rwkv_kernel(u_ref, mu_ref, w0_ref, wup_ref, a0_ref, aup_ref, gup_ref, kk_ref, ka_ref,
                 rk_ref, ng_ref, nb_ref, ones_ref, o_ref, state_ref, prev_ref):
    c = pl.program_id(1)
    ones = ones_ref[...]

    @pl.when(c == 0)
    def _():
        state_ref[...] = jnp.zeros_like(state_ref)
        prev_ref[...] = jnp.zeros_like(prev_ref)

    u = u_ref[...]
    tb = u.shape[0]
    row = lax.broadcasted_iota(jnp.int32, u.shape, 0)
    u_prev = jnp.where(row == 0, prev_ref[...], pltpu.roll(u, 1, axis=0))
    prev_ref[...] = u[tb - 1:tb, :]
    xs = u + mu_ref[...] * (u_prev - u)
    r = xs[:, 0:MIX]
    k = xs[:, MIX:2 * MIX]
    v = xs[:, 2 * MIX:3 * MIX]
    w_lo = xs[:, 3 * MIX:3 * MIX + 64]
    a_lo = xs[:, 3 * MIX + 64:3 * MIX + 128]
    g_lo = xs[:, 3 * MIX + 128:3 * MIX + 256]

    logw = -math.exp(-0.5) * _sigmoid(w0_ref[...] + _bdot(jnp.tanh(w_lo), wup_ref[...]))
    a = _sigmoid(a0_ref[...] + _bdot(a_lo, aup_ref[...]))
    g = _bdot(_sigmoid(g_lo), gup_ref[...])
    kk = k * kk_ref[...]
    kk = kk * lax.rsqrt(jnp.maximum(_head_sum_mxu(kk * kk, ones), 1e-24))
    k2 = k * (1.0 + (a - 1.0) * ka_ref[...])
    alpha = -kk
    beta = kk * a
    bonus = _head_sum_mxu(r * k2 * rk_ref[...], ones) * v

    C = RWKV_C
    ti = lax.broadcasted_iota(jnp.int32, (tb, tb), 0)
    si = lax.broadcasted_iota(jnp.int32, (tb, tb), 1)
    same_chunk = (ti // C) == (si // C)
    prefix = (same_chunk & (ti >= si)).astype(BF16)
    whole = same_chunk.astype(BF16)
    cum = _split_dot(prefix, logw, 3)
    cum_last = _split_dot(whole, logw, 3)
    gam_all = jnp.exp(cum_last)
    e_inv = jnp.exp(-cum)
    e_last = jnp.exp(cum_last - cum)
    ag_all = alpha * jnp.exp(cum - logw)
    rg_all = r * jnp.exp(cum)
    bi_all = beta * e_inv
    ki_all = k2 * e_inv
    bl_all = beta * e_last
    kl_all = k2 * e_last

    hc = N_HEADS * C
    row_head = lax.broadcasted_iota(jnp.int32, (hc, MIX), 0) // C
    lane_head = lax.broadcasted_iota(jnp.int32, (hc, MIX), 1) // HEAD_DIM
    own = row_head == lane_head

    def stack(x):
        return jnp.where(own, jnp.concatenate([x] * N_HEADS, axis=0), 0.0).astype(BF16)

    rr = lax.broadcasted_iota(jnp.int32, (hc, hc), 0)
    cc = lax.broadcasted_iota(jnp.int32, (hc, hc), 1)
    strict = rr > cc
    incl = rr >= cc
    eye_hc = (rr == cc).astype(F32)
    kr = lax.broadcasted_iota(jnp.int32, (MIX, MIX), 0)
    kc_ = lax.broadcasted_iota(jnp.int32, (MIX, MIX), 1)
    eye_k = kr == kc_

    state = state_ref[...]
    y_chunks = []
    for ch in range(tb // C):
        sl = slice(ch * C, (ch + 1) * C)
        ag, rg, bi, ki, bl, kl, vm = (stack(t[sl]) for t in (ag_all, rg_all, bi_all, ki_all,
                                                               bl_all, kl_all, v))
        aa = _bdot_nt(jnp.concatenate([ag, rg], axis=0), jnp.concatenate([bi, ki], axis=0))
        a_ab = jnp.where(strict, aa[:hc, :hc], 0.0)
        a_ak = jnp.where(strict, aa[:hc, hc:], 0.0)
        a_rb = jnp.where(incl, aa[hc:, :hc], 0.0)
        a_rk = jnp.where(incl, aa[hc:, hc:], 0.0)
        pw = a_ab
        t_inv = eye_hc + pw
        for _ in range(int(math.log2(C)) - 1):
            pw = _bdot(pw, pw)
            t_inv = t_inv + _bdot(t_inv, pw)
        w12 = _bdot(t_inv, jnp.concatenate([ag.astype(F32), _bdot(a_ak, vm)], axis=-1))
        ry = _bdot(a_rb, w12)
        rq = rg.astype(F32) + ry[:, :MIX]
        y0 = ry[:, MIX:] + _bdot(a_rk, vm)
        mn = _bdot(bl.T, w12)
        m_mat = jnp.where(eye_k, gam_all[ch * C:ch * C + 1, :], 0.0) + mn[:, :MIX]
        n_mat = mn[:, MIX:] + _bdot(kl.T, vm)
        ym = _split_dot(rq.astype(BF16), state, 2) + y0
        y_chunks.append(ym[0:C] + ym[C:2 * C] + ym[2 * C:3 * C] + ym[3 * C:4 * C])
        state = _split_dot(m_mat.astype(BF16), state, 2) + n_mat
    state_ref[...] = state
    y = jnp.concatenate(y_chunks, axis=0)

    mean = _head_sum_mxu(y, ones) * (1.0 / HEAD_DIM)
    yc = y - mean
    var = _head_sum_mxu(yc * yc, ones) * (1.0 / HEAD_DIM)
    yn = yc * lax.rsqrt(var + RWKV_GN_EPS) * ng_ref[...] + nb_ref[...]
    o_ref[...] = (yn + bonus) * g


def _rwkv_call(u, mu, w0, w_up, a0, a_up, g_up, k_k, k_a, r_k, norm_g, norm_b):
    bsz, seq, _ = u.shape
    tb = min(RWKV_TB, seq)
    vec = lambda t: t.reshape(1, -1).astype(F32)
    args = (u, vec(mu), vec(w0), w_up, vec(a0), a_up, g_up, vec(k_k), vec(k_a), vec(r_k),
            vec(norm_g), vec(norm_b), _block_ones())
    in_specs = [_row_spec(tb, 4 * MIX)] + [_const_spec(t.shape) for t in args[1:]]
    return pl.pallas_call(
        _rwkv_kernel,
        grid=(bsz, seq // tb),
        in_specs=in_specs,
        out_specs=_row_spec(tb, MIX),
        out_shape=jax.ShapeDtypeStruct((bsz, seq, MIX), F32),
        scratch_shapes=[pltpu.VMEM((MIX, MIX), F32),
                        pltpu.VMEM((1, 4 * MIX), F32)],
        compiler_params=_params(("parallel", "arbitrary")),
    )(*args)


NSA_TQ = 128
NSA_TK = 512
NSA_CT = 512
GROUP = NSA_CMP_STRIDE


def _nsa_compress_kernel(xk_ref, xv_ref, pk_ref, pv_ref, k1_ref, k2_ref, v1_ref, v2_ref, o_ref):
    half = GROUP * HEAD_DIM

    def compress(x, pos, w1_ref, w2_ref):
        n = x.shape[0]
        first = _bdot(x + pos[:, :half], w1_ref[0:half, :])
        second = _bdot(x + pos[:, half:], w1_ref[half:2 * half, :])
        hid = first + pltpu.roll(second, n - 1, axis=0)
        return _bdot(_silu(hid), w2_ref[...])

    kc = compress(xk_ref[...], pk_ref[...], k1_ref, k2_ref)
    vc = compress(xv_ref[...], pv_ref[...], v1_ref, v2_ref)
    o_ref[...] = jnp.concatenate([kc, vc], axis=-1)


def _nsa_compress_call(xk, xv, pos_k, pos_v, k1, k2, v1, v2):
    bsz, ng, width = xk.shape
    args = (xk, xv, pos_k.reshape(1, -1), pos_v.reshape(1, -1), k1.astype(BF16), k2.astype(BF16),
            v1.astype(BF16), v2.astype(BF16))
    blk = pl.BlockSpec((None, ng, width), lambda b: (b, 0, 0))
    const = lambda t: pl.BlockSpec(t.shape, lambda b: (0,) * t.ndim)
    return pl.pallas_call(
        _nsa_compress_kernel,
        grid=(bsz,),
        in_specs=[blk, blk] + [const(t) for t in args[2:]],
        out_specs=pl.BlockSpec((None, ng, 2 * HEAD_DIM), lambda b: (b, 0, 0)),
        out_shape=jax.ShapeDtypeStruct((bsz, ng, 2 * HEAD_DIM), F32),
        compiler_params=_params(("parallel",)),
    )(*args)


REMOVED = -3e38


def _nsa_kernel(q_ref, g_ref, ks_ref, vs_ref, kw_ref, vw_ref, kc_ref, vc_ref, ov_ref, o_ref,
                selbias_ref, *, n_top):
    c = pl.program_id(1)
    tq = q_ref.shape[0]
    tk = ks_ref.shape[1]
    ct = kc_ref.shape[1]
    ns = ov_ref.shape[1]
    t0 = c * tq
    cols = N_HEADS * tq

    q_t = (q_ref[...] * (HEAD_DIM ** -0.5)).T
    qs = jnp.concatenate([q_t[h * HEAD_DIM:(h + 1) * HEAD_DIM] for h in range(N_HEADS)],
                         axis=1).astype(BF16)
    t_q = t0 + lax.broadcasted_iota(jnp.int32, (1, tq), 1)
    t_col = jnp.concatenate([t_q] * N_HEADS, axis=1)

    def online(carry, s, v_t):
        m, l, acc = carry
        m_new = jnp.maximum(m, jnp.max(s, axis=0, keepdims=True))
        p = jnp.exp(s - m_new)
        scale = jnp.exp(m - m_new)
        l = scale * l + jnp.sum(p, axis=0, keepdims=True)
        acc = scale * acc + jnp.dot(v_t, p.astype(BF16), preferred_element_type=F32)
        return m_new, l, acc

    init = (jnp.full((1, cols), NEG_BIG, F32), jnp.zeros((1, cols), F32),
            jnp.zeros((HEAD_DIM, cols), F32))

    n_row = lax.broadcasted_iota(jnp.int32, (ct, 1), 0)

    def cmp_step(i, carry):
        s = jnp.dot(kc_ref[i], qs, preferred_element_type=F32)
        valid = ((i * ct + n_row) * NSA_CMP_STRIDE + (NSA_CMP_BLOCK - 1)) <= t_col
        s = jnp.where(valid, s, NEG_BIG)
        m, l, acc, imp = carry
        m_new = jnp.maximum(m, jnp.max(s, axis=0, keepdims=True))
        p = jnp.where(valid, jnp.exp(s - m_new), 0.0)
        scale = jnp.exp(m - m_new)
        l = scale * l + jnp.sum(p, axis=0, keepdims=True)
        p_hi = p.astype(BF16)
        p_lo = (p - p_hi.astype(F32)).astype(BF16)
        acc = scale * acc + jnp.dot(vc_ref[i], p_hi, preferred_element_type=F32)
        ov = ov_ref[i]
        imp = (scale * imp + jnp.dot(ov, p_hi, preferred_element_type=F32)
               + jnp.dot(ov, p_lo, preferred_element_type=F32))
        return m_new, l, acc, imp

    last_valid = (t0 + tq - NSA_CMP_BLOCK) // NSA_CMP_STRIDE
    n_ctiles = jnp.minimum(last_valid // ct + 1, kc_ref.shape[0])
    _, l_c, acc_c, imp4 = lax.fori_loop(0, n_ctiles, cmp_step,
                                        init + (jnp.zeros((ns, cols), F32),))
    inv_c = jnp.where(l_c > 0.0, 1.0 / l_c, 0.0)
    o_cmp = acc_c * inv_c
    imp4 = imp4 * inv_c
    imp = imp4[:, 0:tq]
    for h in range(1, N_HEADS):
        imp = imp + imp4[:, h * tq:(h + 1) * tq]

    blk = lax.broadcasted_iota(jnp.int32, (ns, 1), 0)
    blk_f = blk.astype(F32)
    cur = t_q // NSA_SEL_BLOCK
    forced = (blk == 0) | (blk == cur) | (blk == cur - 1)
    score = jnp.where(forced, POS_BIG, jnp.where(blk <= cur, imp, NEG_BIG))
    chosen = jnp.zeros((ns, tq), jnp.bool_)
    for _ in range(n_top):
        best = jnp.max(score, axis=0, keepdims=True)
        first = jnp.min(jnp.where(score == best, blk_f, float(ns)), axis=0, keepdims=True)
        hit = blk_f == first
        chosen = chosen | hit
        score = jnp.where(hit, REMOVED, score)
    selbias_ref[...] = jnp.where(chosen, 0.0, NEG_BIG)

    per_tile = tk // NSA_SEL_BLOCK
    key_row = lax.broadcasted_iota(jnp.int32, (tk, 1), 0)

    def sel_bias(j):
        rows = [jnp.broadcast_to(selbias_ref[pl.ds(j * per_tile + b, 1), :], (NSA_SEL_BLOCK, tq))
                for b in range(per_tile)]
        return jnp.concatenate(rows, axis=0)

    def sel_step(j, carry):
        s = jnp.dot(ks_ref[j], qs, preferred_element_type=F32)
        return online(carry, s + jnp.concatenate([sel_bias(j)] * N_HEADS, axis=1), vs_ref[j])

    j_last = t0 // tk
    carry = lax.fori_loop(0, j_last, sel_step, init)
    causal = jnp.where((j_last * tk + key_row) <= t_q, 0.0, NEG_BIG)
    s = jnp.dot(ks_ref[j_last], qs, preferred_element_type=F32)
    _, l_s, acc_s = online(carry, s + jnp.concatenate([sel_bias(j_last) + causal] * N_HEADS, axis=1),
                           vs_ref[j_last])
    o_sel = acc_s / l_s

    tw = kw_ref.shape[1]
    n_wt = (NSA_WINDOW + tq) // tw
    jw = jnp.maximum((t0 - NSA_WINDOW) // tw, 0)
    wkey_row = lax.broadcasted_iota(jnp.int32, (tw, 1), 0)
    s_parts = []
    for i in range(n_wt):
        dist = t_q - ((jw + i) * tw + wkey_row)
        bias = jnp.where((dist >= 0) & (dist < NSA_WINDOW), 0.0, NEG_BIG)
        s_parts.append(jnp.dot(kw_ref[jw + i], qs, preferred_element_type=F32)
                       + jnp.concatenate([bias] * N_HEADS, axis=1))
    m_w = s_parts[0].max(axis=0, keepdims=True)
    for sp in s_parts[1:]:
        m_w = jnp.maximum(m_w, sp.max(axis=0, keepdims=True))
    l_w = jnp.zeros((1, cols), F32)
    acc_w = jnp.zeros((HEAD_DIM, cols), F32)
    for i, sp in enumerate(s_parts):
        p = jnp.exp(sp - m_w)
        l_w = l_w + jnp.sum(p, axis=0, keepdims=True)
        acc_w = acc_w + jnp.dot(vw_ref[jw + i], p.astype(BF16), preferred_element_type=F32)
    o_win = acc_w / l_w

    gates = _sigmoid(g_ref[...]).T

    def gate_row(branch):
        return jnp.concatenate([gates[branch * N_HEADS + h:branch * N_HEADS + h + 1, :]
                                for h in range(N_HEADS)], axis=1)

    out = gate_row(0) * o_cmp + gate_row(1) * o_sel + gate_row(2) * o_win
    o_ref[...] = jnp.concatenate([out[:, h * tq:(h + 1) * tq].T for h in range(N_HEADS)], axis=-1)


def _nsa_call(u_nsa, pos_k, pos_v, k1, k2, v1, v2):
    bsz, seq, _ = u_nsa.shape
    ng = seq // GROUP
    xk = u_nsa[:, :, MIX:MIX + HEAD_DIM].reshape(bsz, ng, GROUP * HEAD_DIM)
    xv = u_nsa[:, :, MIX + HEAD_DIM:MIX + 2 * HEAD_DIM].reshape(bsz, ng, GROUP * HEAD_DIM)
    kvc = _nsa_compress_call(xk, xv, pos_k, pos_v, k1, k2, v1, v2)

    n_sel = seq // NSA_SEL_BLOCK
    n_top = min(NSA_TOP_N, n_sel)
    cmp_start = jnp.arange(ng) * NSA_CMP_STRIDE
    sel_start = jnp.arange(n_sel) * NSA_SEL_BLOCK
    overlap = ((cmp_start[:, None] < sel_start[None, :] + NSA_SEL_BLOCK)
               & (cmp_start[:, None] + NSA_CMP_BLOCK > sel_start[None, :])).astype(BF16)
    tq = min(NSA_TQ, seq)
    tk = min(NSA_TK, seq)
    tw = tq
    ct = min(NSA_CT, ng)

    def key_tiles(a, tile):
        return a.astype(BF16).reshape(bsz, a.shape[1] // tile, tile, HEAD_DIM)

    def value_tiles(a, tile):
        return jnp.swapaxes(key_tiles(a, tile), -1, -2)

    col = lambda j: u_nsa[:, :, MIX + j * HEAD_DIM:MIX + (j + 1) * HEAD_DIM]
    operands = (u_nsa, u_nsa,
                key_tiles(col(2), tk), value_tiles(col(3), tk),
                key_tiles(col(4), tw), value_tiles(col(5), tw),
                key_tiles(kvc[:, :, :HEAD_DIM], ct), value_tiles(kvc[:, :, HEAD_DIM:], ct),
                jnp.swapaxes(overlap.reshape(ng // ct, ct, n_sel), -1, -2))
    per_batch = lambda t: pl.BlockSpec((None,) + t.shape[1:], lambda b, c: (b, 0, 0, 0))
    return pl.pallas_call(
        functools.partial(_nsa_kernel, n_top=n_top),
        grid=(bsz, seq // tq),
        in_specs=[
            pl.BlockSpec((None, tq, MIX), lambda b, c: (b, c, 0)),
            pl.BlockSpec((None, tq, 128), lambda b, c: (b, c, 5)),
        ] + [per_batch(t) for t in operands[2:8]] + [_const_spec(operands[8].shape)],
        out_specs=_row_spec(tq, MIX),
        out_shape=jax.ShapeDtypeStruct((bsz, seq, MIX), F32),
        scratch_shapes=[pltpu.VMEM((n_sel, tq), F32)],
        compiler_params=_params(("parallel", "arbitrary")),
    )(*operands)


HGRN_TB = 256
HGRN_C = 32


def _hgrn_kernel(u_ref, lbl_ref, ng_ref, ones_ref, o_ref, state_ref, *, layer):
    c = pl.program_id(1)

    @pl.when(c == 0)
    def _():
        state_ref[...] = jnp.zeros_like(state_ref)

    logits = lbl_ref[...]
    ex = jnp.exp(logits - jnp.max(logits, axis=0, keepdims=True))
    soft = ex / jnp.sum(ex, axis=0, keepdims=True)
    lb = jnp.sum(soft[0:layer + 1], axis=0, keepdims=True) - soft[0:1]

    u = u_ref[...]
    tb = u.shape[0]
    q = _silu(u[:, 0:MIX])
    f = lb + (1.0 - lb) * _sigmoid(u[:, MIX:2 * MIX])
    logf = jnp.log(jnp.maximum(f, GATE_FLOOR))
    k = 1.0 - f
    v = u[:, 2 * MIX:3 * MIX]
    og = u[:, 3 * MIX:4 * MIX]

    C = HGRN_C
    ri = lax.broadcasted_iota(jnp.int32, (C, C), 0)
    ci = lax.broadcasted_iota(jnp.int32, (C, C), 1)
    tril_f = (ri >= ci).astype(F32)
    row_i = lax.broadcasted_iota(jnp.int32, (C, 1), 0)
    ones_bd = ones_ref[...]

    states = [state_ref[h] for h in range(N_HEADS)]
    o_chunks = []
    for ch in range(tb // C):
        sl = slice(ch * C, (ch + 1) * C)
        qc, kc, vc = q[sl], k[sl], v[sl]
        b = _fdot(tril_f, logf[sl])
        b_last = b[C - 1:C, :]
        pieces = []
        for s in range(C):
            pm = qc * kc[s:s + 1, :] * jnp.exp(jnp.minimum(b - b[s:s + 1, :], 0.0))
            pieces.append(jnp.where(row_i >= s, pm, 0.0).astype(BF16))
        attn = jnp.dot(jnp.concatenate(pieces, axis=0), ones_bd, preferred_element_type=F32)
        o = jnp.zeros((C, MIX), F32)
        for s in range(C):
            o = o + attn[s * C:(s + 1) * C, :] * vc[s:s + 1, :]
        qe = qc * jnp.exp(b)
        kd = kc * jnp.exp(b_last - b)
        g_last = jnp.exp(b_last)
        o_heads = []
        for h in range(N_HEADS):
            hs = slice(h * HEAD_DIM, (h + 1) * HEAD_DIM)
            o_heads.append(_bdot_nt(qe[:, hs], states[h]))
            states[h] = states[h] * g_last[:, hs] + _bdot(vc[:, hs].T, kd[:, hs])
        o_chunks.append(o + jnp.concatenate(o_heads, axis=-1))
    for h in range(N_HEADS):
        state_ref[h] = states[h]
    o = jnp.concatenate(o_chunks, axis=0)
    ms = _head_sum(o * o) * (1.0 / HEAD_DIM)
    o_ref[...] = o * lax.rsqrt(ms + NORM_EPS) * ng_ref[...] * _silu(og)


def _block_ones():
    hid = jnp.arange(MIX) // HEAD_DIM
    return (hid[:, None] == hid[None, :]).astype(BF16)


def _hgrn_call(u, lb_logits, norm_g, layer):
    bsz, seq, _ = u.shape
    tb = min(HGRN_TB, seq)
    args = (u, lb_logits.astype(F32), norm_g.reshape(1, -1).astype(F32), _block_ones())
    return pl.pallas_call(
        functools.partial(_hgrn_kernel, layer=layer),
        grid=(bsz, seq // tb),
        in_specs=[_row_spec(tb, 4 * MIX)] + [_const_spec(t.shape) for t in args[1:]],
        out_specs=_row_spec(tb, MIX),
        out_shape=jax.ShapeDtypeStruct((bsz, seq, MIX), F32),
        scratch_shapes=[pltpu.VMEM((N_HEADS, HEAD_DIM, HEAD_DIM), F32)],
        compiler_params=_params(("parallel", "arbitrary")),
    )(*args)


RET_C = 256


def _ret_kernel(u_ref, cos_ref, sin_ref, dm_ref, qd_ref, kd_ref, cd_ref, ng_ref, nb_ref,
                o_ref, state_ref):
    c = pl.program_id(1)

    @pl.when(c == 0)
    def _():
        state_ref[...] = jnp.zeros_like(state_ref)

    u = u_ref[...]
    cosf = cos_ref[...]
    sins = sin_ref[...]
    half = HEAD_DIM // 2

    def rope(a):
        outs = []
        for j in range(MIX // 128):
            blk = a[:, j * 128:(j + 1) * 128]
            lane = lax.broadcasted_iota(jnp.int32, blk.shape, 1)
            swapped = jnp.where((lane % HEAD_DIM) < half, pltpu.roll(blk, 128 - half, axis=1),
                                pltpu.roll(blk, half, axis=1))
            outs.append(swapped)
        return a * cosf + jnp.concatenate(outs, axis=-1) * sins

    q = rope(u[:, 0:MIX])
    k = rope(u[:, MIX:2 * MIX]) * (HEAD_DIM ** -0.5)
    v = u[:, 2 * MIX:3 * MIX]
    g = u[:, 3 * MIX:4 * MIX]
    qd = q * qd_ref[...]
    kd = k * kd_ref[...]
    cd = cd_ref[...]
    o_heads = []
    for h in range(N_HEADS):
        hs = slice(h * HEAD_DIM, (h + 1) * HEAD_DIM)
        s = _bdot_nt(q[:, hs], k[:, hs]) * dm_ref[h]
        st = state_ref[h]
        o_heads.append(_bdot(s, v[:, hs]) + _bdot(qd[:, hs], st))
        state_ref[h] = st * cd[:, hs] + _bdot(kd[:, hs].T, v[:, hs])
    o = jnp.concatenate(o_heads, axis=-1)
    mean = _head_sum(o) * (1.0 / HEAD_DIM)
    oc = o - mean
    var = _head_sum(oc * oc) * (1.0 / HEAD_DIM)
    y = oc * lax.rsqrt(var + RET_GN_EPS) * ng_ref[...] + nb_ref[...]
    o_ref[...] = y * _silu(g)


def _ret_call(u, norm_g, norm_b):
    bsz, seq, _ = u.shape
    C = min(RET_C, seq)
    half = HEAD_DIM // 2
    pos = jnp.arange(seq, dtype=F32)
    inv_freq = RET_ROPE_BASE ** (-jnp.arange(0, HEAD_DIM, 2, dtype=F32) / HEAD_DIM)
    ang = pos[:, None] * inv_freq[None, :]
    cos, sin = jnp.cos(ang), jnp.sin(ang)
    cosf = jnp.tile(jnp.concatenate([cos, cos], axis=-1), (1, N_HEADS))
    sins = jnp.tile(jnp.concatenate([-sin, sin], axis=-1), (1, N_HEADS))
    log_gamma = jnp.log(1.0 - jnp.exp2(-5.0 - jnp.arange(N_HEADS, dtype=F32)))
    i = jnp.arange(C, dtype=F32)
    dpos = i[:, None] - i[None, :]
    dm = jnp.where(dpos >= 0, jnp.exp(jnp.maximum(dpos, 0.0)[None] * log_gamma[:, None, None]), 0.0)
    lanes = lambda t: jnp.repeat(t, HEAD_DIM, axis=-1)
    qd = lanes(jnp.exp((i + 1.0)[:, None] * log_gamma[None, :]))
    kd = lanes(jnp.exp((C - 1.0 - i)[:, None] * log_gamma[None, :]))
    cd = lanes(jnp.exp(C * log_gamma)[None, :])
    args = (u, cosf, sins, dm, qd, kd, cd, norm_g.reshape(1, -1).astype(F32),
            norm_b.reshape(1, -1).astype(F32))
    in_specs = ([_row_spec(C, 4 * MIX),
                 pl.BlockSpec((C, MIX), lambda b, c: (c, 0)),
                 pl.BlockSpec((C, MIX), lambda b, c: (c, 0))]
                + [_const_spec(t.shape) for t in args[3:]])
    return pl.pallas_call(
        _ret_kernel,
        grid=(bsz, seq // C),
        in_specs=in_specs,
        out_specs=_row_spec(C, MIX),
        out_shape=jax.ShapeDtypeStruct((bsz, seq, MIX), F32),
        scratch_shapes=[pltpu.VMEM((N_HEADS, HEAD_DIM, HEAD_DIM), F32)],
        compiler_params=_params(("parallel", "arbitrary")),
    )(*args)


DENSE_TM = 512
FFN_TF = 256
GROUP_COLS = (NSA_PAD, 4 * MIX, 4 * MIX, 4 * MIX)


def _rmsnorm(x, g):
    return x * lax.rsqrt(jnp.mean(x * x, axis=-1, keepdims=True) + NORM_EPS) * g


def _resident(shape):
    nd = len(shape)
    return pl.BlockSpec(shape, lambda b, c: (0,) * nd, pipeline_mode=pl.Buffered(1))


def _inproj_kernel(h_ref, g_ref, w_ref, nsa_ref, hg_ref, rt_ref, rw_ref):
    xn = _rmsnorm(h_ref[...], g_ref[...]).astype(BF16)
    off = 0
    for ref, width in zip((nsa_ref, hg_ref, rt_ref, rw_ref), GROUP_COLS):
        ref[...] = jnp.dot(xn, w_ref[:, off:off + width], preferred_element_type=F32)
        off += width


def _inproj_call(h, g, w_pad):
    bsz, seq, _ = h.shape
    tm = min(DENSE_TM, seq)
    return pl.pallas_call(
        _inproj_kernel,
        grid=(bsz, seq // tm),
        in_specs=[_row_spec(tm, D_MODEL), _resident((1, D_MODEL)), _resident(w_pad.shape)],
        out_specs=[_row_spec(tm, w) for w in GROUP_COLS],
        out_shape=[jax.ShapeDtypeStruct((bsz, seq, w), F32) for w in GROUP_COLS],
        compiler_params=_params(("parallel", "parallel")),
    )(h, g.reshape(1, -1), w_pad)


def _merge_kernel(h_ref, b0_ref, b1_ref, b2_ref, b3_ref, g_ref, wg_ref, bg_ref, wb_ref, wo_ref,
                  o_ref):
    h = h_ref[...]
    xn = _rmsnorm(h, g_ref[...]).astype(BF16)
    merged = None
    for m, b_ref in enumerate((b0_ref, b1_ref, b2_ref, b3_ref)):
        gate = _sigmoid(jnp.dot(xn, wg_ref[m], preferred_element_type=F32) + bg_ref[m])
        term = gate * _bdot(b_ref[...], wb_ref[m])
        merged = term if merged is None else merged + term
    o_ref[...] = h + _bdot(merged, wo_ref[...])


def _merge_call(h, branches, g, w_gate, b_gate, w_branch, w_out):
    bsz, seq, _ = h.shape
    tm = min(DENSE_TM, seq)
    consts = (g.reshape(1, -1), w_gate, b_gate.reshape(4, 1, D_MODEL), w_branch, w_out)
    return pl.pallas_call(
        _merge_kernel,
        grid=(bsz, seq // tm),
        in_specs=([_row_spec(tm, D_MODEL)] + [_row_spec(tm, MIX)] * 4
                  + [_resident(t.shape) for t in consts]),
        out_specs=_row_spec(tm, D_MODEL),
        out_shape=jax.ShapeDtypeStruct(h.shape, F32),
        compiler_params=_params(("parallel", "parallel")),
    )(h, *branches, *consts)


def _ffn_kernel(h_ref, g_ref, wg_ref, wu_ref, wd_ref, o_ref):
    h = h_ref[...]
    hn = _rmsnorm(h, g_ref[...]).astype(BF16)
    acc = h
    for f in range(0, D_FF, FFN_TF):
        gate = jnp.dot(hn, wg_ref[:, f:f + FFN_TF], preferred_element_type=F32)
        up = jnp.dot(hn, wu_ref[:, f:f + FFN_TF], preferred_element_type=F32)
        acc = acc + _bdot(_silu(gate) * up, wd_ref[f:f + FFN_TF, :])
    o_ref[...] = acc


def _ffn_call(h, g, w_gate, w_up, w_down):
    bsz, seq, _ = h.shape
    tm = min(DENSE_TM, seq)
    consts = (g.reshape(1, -1), w_gate, w_up, w_down)
    return pl.pallas_call(
        _ffn_kernel,
        grid=(bsz, seq // tm),
        in_specs=[_row_spec(tm, D_MODEL)] + [_resident(t.shape) for t in consts],
        out_specs=_row_spec(tm, D_MODEL),
        out_shape=jax.ShapeDtypeStruct(h.shape, F32),
        compiler_params=_params(("parallel", "parallel")),
    )(h, *consts)


def _ple_kernel(h_ref, p_ref, g_ref, wg_ref, wp_ref, gf_ref, o_ref, *, final_norm):
    h = h_ref[...]
    hp = _rmsnorm(h, g_ref[...])
    out = h + _sigmoid(_bdot(hp, wg_ref[...])) * _bdot(p_ref[...], wp_ref[...])
    if final_norm:
        out = _rmsnorm(out, gf_ref[...])
    o_ref[...] = out


def _ple_call(h, p, g, w_gate, w_proj, g_final, final_norm):
    bsz, seq, _ = h.shape
    tm = min(DENSE_TM, seq)
    consts = (g.reshape(1, -1), w_gate, w_proj, g_final.reshape(1, -1))
    return pl.pallas_call(
        functools.partial(_ple_kernel, final_norm=final_norm),
        grid=(bsz, seq // tm),
        in_specs=([_row_spec(tm, D_MODEL), _row_spec(tm, PLE_DIM)]
                  + [_resident(t.shape) for t in consts]),
        out_specs=_row_spec(tm, D_MODEL),
        out_shape=jax.ShapeDtypeStruct(h.shape, F32),
        compiler_params=_params(("parallel", "parallel")),
    )(h, p, *consts)


def kernel(x, p, norm_mix, w_in, nsa_pos_k, nsa_pos_v, nsa_cmp_k1, nsa_cmp_k2, nsa_cmp_v1,
           nsa_cmp_v2, hgrn_lb_logits, hgrn_norm, ret_norm_g, ret_norm_b, rwkv_mu, rwkv_w0,
           rwkv_w_up, rwkv_a0, rwkv_a_up, rwkv_g_up, rwkv_k_k, rwkv_k_a, rwkv_r_k, rwkv_norm_g,
           rwkv_norm_b, w_branch, w_gate, b_gate, w_out, norm_ffn, w_ffn_gate, w_ffn_up,
           w_ffn_down, norm_ple, w_ple_gate, w_ple_proj, norm_final):
    depth = w_in.shape[0]
    w_in_pad = jnp.concatenate(
        [w_in[:, :, :NSA_WIDTH], jnp.zeros((depth, D_MODEL, NSA_PAD - NSA_WIDTH), w_in.dtype),
         w_in[:, :, NSA_WIDTH:]], axis=-1).astype(BF16)
    bf = lambda t: t.astype(BF16)
    h = x
    for i in range(depth):
        u_nsa, u_hgrn, u_ret, u_rwkv = _inproj_call(h, norm_mix[i], w_in_pad[i])
        branches = (
            _nsa_call(u_nsa, nsa_pos_k[i], nsa_pos_v[i], nsa_cmp_k1[i], nsa_cmp_k2[i],
                      nsa_cmp_v1[i], nsa_cmp_v2[i]),
            _hgrn_call(u_hgrn, hgrn_lb_logits, hgrn_norm[i], i),
            _ret_call(u_ret, ret_norm_g[i], ret_norm_b[i]),
            _rwkv_call(u_rwkv, rwkv_mu[i], rwkv_w0[i], rwkv_w_up[i], rwkv_a0[i], rwkv_a_up[i],
                       rwkv_g_up[i], rwkv_k_k[i], rwkv_k_a[i], rwkv_r_k[i], rwkv_norm_g[i],
                       rwkv_norm_b[i]),
        )
        h = _merge_call(h, branches, norm_mix[i], bf(w_gate[i]), b_gate[i], bf(w_branch[i]),
                        bf(w_out[i]))
        h = _ffn_call(h, norm_ffn[i], bf(w_ffn_gate[i]), bf(w_ffn_up[i]), bf(w_ffn_down[i]))
        h = _ple_call(h, p[i], norm_ple[i], bf(w_ple_gate[i]), bf(w_ple_proj[i]), norm_final,
                      final_norm=(i == depth - 1))
    return h
```

```python
import functools
import math

import jax
import jax.numpy as jnp
from jax import lax
from jax.experimental import pallas as pl
from jax.experimental.pallas import tpu as pltpu

F32 = jnp.float32
BF16 = jnp.bfloat16

D_MODEL = 1024
N_HEADS = 4
HEAD_DIM = 64
MIX = N_HEADS * HEAD_DIM
D_FF = 2816
PLE_DIM = 256
NORM_EPS = 1e-6
NEG_BIG = -1e30
POS_BIG = 1e30
GATE_FLOOR = 1e-20

NSA_CMP_BLOCK = 32
NSA_CMP_STRIDE = 16
NSA_SEL_BLOCK = 64
NSA_TOP_N = 16
NSA_WINDOW = 512
NSA_CMP_HIDDEN = 128
NSA_WIDTH = 652
NSA_PAD = 768

RET_ROPE_BASE = 10000.0
RET_GN_EPS = 1e-5
RWKV_GN_EPS = 64e-5

VMEM_LIMIT = 56 * 1024 * 1024


def _bdot(a, b):
    return jnp.dot(a.astype(BF16), b.astype(BF16), preferred_element_type=F32)


def _bdot_nt(a, b):
    return lax.dot_general(a.astype(BF16), b.astype(BF16), (((1,), (1,)), ((), ())),
                           preferred_element_type=F32)


def _fdot(a, b):
    return jnp.dot(a, b, precision=lax.Precision.HIGHEST, preferred_element_type=F32)


def _fdot_nt(a, b):
    return lax.dot_general(a, b, (((1,), (1,)), ((), ())), precision=lax.Precision.HIGHEST,
                           preferred_element_type=F32)


def _sigmoid(x):
    return 1.0 / (1.0 + jnp.exp(-x))


def _silu(x):
    return x * _sigmoid(x)


def _head_sum(x):
    parts = []
    for h in range(N_HEADS):
        s = jnp.sum(x[:, h * HEAD_DIM:(h + 1) * HEAD_DIM], axis=-1, keepdims=True)
        parts.append(jnp.broadcast_to(s, (x.shape[0], HEAD_DIM)))
    return jnp.concatenate(parts, axis=-1)


def _params(sem):
    return pltpu.CompilerParams(dimension_semantics=sem, vmem_limit_bytes=VMEM_LIMIT)


def _row_spec(tile, width):
    return pl.BlockSpec((None, tile, width), lambda b, c: (b, c, 0))


def _const_spec(shape):
    nd = len(shape)
    return pl.BlockSpec(shape, lambda b, c: (0,) * nd)


RWKV_TB = 256
RWKV_C = 64


def _split_dot(a_bf16, x, parts):
    total = None
    rest = x
    for _ in range(parts):
        piece = rest.astype(BF16)
        rest = rest - piece.astype(F32)
        term = jnp.dot(a_bf16, piece, preferred_element_type=F32)
        total = term if total is None else total + term
    return total


def _head_sum_mxu(x, ones_t):
    hi = x.astype(BF16)
    lo = (x - hi.astype(F32)).astype(BF16)
    return (jnp.dot(hi, ones_t, preferred_element_type=F32)
            + jnp.dot(lo, ones_t, preferred_element_type=F32))


def _rwkv_kernel(u_ref, mu_ref, w0_ref, wup_ref, a0_ref, aup_ref, gup_ref, kk_ref, ka_ref,
                 rk_ref, ng_ref, nb_ref, ones_ref, o_ref, state_ref, prev_ref):
    c = pl.program_id(1)
    ones = ones_ref[...]

    @pl.when(c == 0)
    def _():
        state_ref[...] = jnp.zeros_like(state_ref)
        prev_ref[...] = jnp.zeros_like(prev_ref)

    u = u_ref[...]
    tb = u.shape[0]
    row = lax.broadcasted_iota(jnp.int32, u.shape, 0)
    u_prev = jnp.where(row == 0, prev_ref[...], pltpu.roll(u, 1, axis=0))
    prev_ref[...] = u[tb - 1:tb, :]
    xs = u + mu_ref[...] * (u_prev - u)
    r = xs[:, 0:MIX]
    k = xs[:, MIX:2 * MIX]
    v = xs[:, 2 * MIX:3 * MIX]
    w_lo = xs[:, 3 * MIX:3 * MIX + 64]
    a_lo = xs[:, 3 * MIX + 64:3 * MIX + 128]
    g_lo = xs[:, 3 * MIX + 128:3 * MIX + 256]

    logw = -math.exp(-0.5) * _sigmoid(w0_ref[...] + _bdot(jnp.tanh(w_lo), wup_ref[...]))
    a = _sigmoid(a0_ref[...] + _bdot(a_lo, aup_ref[...]))
    g = _bdot(_sigmoid(g_lo), gup_ref[...])
    kk = k * kk_ref[...]
    kk = kk * lax.rsqrt(jnp.maximum(_head_sum_mxu(kk * kk, ones), 1e-24))
    k2 = k * (1.0 + (a - 1.0) * ka_ref[...])
    alpha = -kk
    beta = kk * a
    bonus = _head_sum_mxu(r * k2 * rk_ref[...], ones) * v

    C = RWKV_C
    ti = lax.broadcasted_iota(jnp.int32, (tb, tb), 0)
    si = lax.broadcasted_iota(jnp.int32, (tb, tb), 1)
    same_chunk = (ti // C) == (si // C)
    prefix = (same_chunk & (ti >= si)).astype(BF16)
    whole = same_chunk.astype(BF16)
    cum = _split_dot(prefix, logw, 3)
    cum_last = _split_dot(whole, logw, 3)
    gam_all = jnp.exp(cum_last)
    e_inv = jnp.exp(-cum)
    e_last = jnp.exp(cum_last - cum)
    ag_all = alpha * jnp.exp(cum - logw)
    rg_all = r * jnp.exp(cum)
    bi_all = beta * e_inv
    ki_all = k2 * e_inv
    bl_all = beta * e_last
    kl_all = k2 * e_last

    hc = N_HEADS * C
    row_head = lax.broadcasted_iota(jnp.int32, (hc, MIX), 0) // C
    lane_head = lax.broadcasted_iota(jnp.int32, (hc, MIX), 1) // HEAD_DIM
    own = row_head == lane_head

    def stack(x):
        return jnp.where(own, jnp.concatenate([x] * N_HEADS, axis=0), 0.0).astype(BF16)

    rr = lax.broadcasted_iota(jnp.int32, (hc, hc), 0)
    cc = lax.broadcasted_iota(jnp.int32, (hc, hc), 1)
    strict = rr > cc
    incl = rr >= cc
    eye_hc = (rr == cc).astype(F32)
    kr = lax.broadcasted_iota(jnp.int32, (MIX, MIX), 0)
    kc_ = lax.broadcasted_iota(jnp.int32, (MIX, MIX), 1)
    eye_k = kr == kc_

    state = state_ref[...]
    y_chunks = []
    for ch in range(tb // C):
        sl = slice(ch * C, (ch + 1) * C)
        ag, rg, bi, ki, bl, kl, vm = (stack(t[sl]) for t in (ag_all, rg_all, bi_all, ki_all,
                                                               bl_all, kl_all, v))
        aa = _bdot_nt(jnp.concatenate([ag, rg], axis=0), jnp.concatenate([bi, ki], axis=0))
        a_ab = jnp.where(strict, aa[:hc, :hc], 0.0)
        a_ak = jnp.where(strict, aa[:hc, hc:], 0.0)
        a_rb = jnp.where(incl, aa[hc:, :hc], 0.0)
        a_rk = jnp.where(incl, aa[hc:, hc:], 0.0)
        pw = a_ab
        t_inv = eye_hc + pw
        for _ in range(int(math.log2(C)) - 1):
            pw = _bdot(pw, pw)
            t_inv = t_inv + _bdot(t_inv, pw)
        w12 = _bdot(t_inv, jnp.concatenate([ag.astype(F32), _bdot(a_ak, vm)], axis=-1))
        ry = _bdot(a_rb, w12)
        rq = rg.astype(F32) + ry[:, :MIX]
        y0 = ry[:, MIX:] + _bdot(a_rk, vm)
        mn = _bdot(bl.T, w12)
        m_mat = jnp.where(eye_k, gam_all[ch * C:ch * C + 1, :], 0.0) + mn[:, :MIX]
        n_mat = mn[:, MIX:] + _bdot(kl.T, vm)
        ym = _split_dot(rq.astype(BF16), state, 2) + y0
        y_chunks.append(ym[0:C] + ym[C:2 * C] + ym[2 * C:3 * C] + ym[3 * C:4 * C])
        state = _split_dot(m_mat.astype(BF16), state, 2) + n_mat
    state_ref[...] = state
    y = jnp.concatenate(y_chunks, axis=0)

    mean = _head_sum_mxu(y, ones) * (1.0 / HEAD_DIM)
    yc = y - mean
    var = _head_sum_mxu(yc * yc, ones) * (1.0 / HEAD_DIM)
    yn = yc * lax.rsqrt(var + RWKV_GN_EPS) * ng_ref[...] + nb_ref[...]
    o_ref[...] = (yn + bonus) * g


def _rwkv_call(u, mu, w0, w_up, a0, a_up, g_up, k_k, k_a, r_k, norm_g, norm_b):
    bsz, seq, _ = u.shape
    tb = min(RWKV_TB, seq)
    vec = lambda t: t.reshape(1, -1).astype(F32)
    args = (u, vec(mu), vec(w0), w_up, vec(a0), a_up, g_up, vec(k_k), vec(k_a), vec(r_k),
            vec(norm_g), vec(norm_b), _block_ones())
    in_specs = [_row_spec(tb, 4 * MIX)] + [_const_spec(t.shape) for t in args[1:]]
    return pl.pallas_call(
        _rwkv_kernel,
        grid=(bsz, seq // tb),
        in_specs=in_specs,
        out_specs=_row_spec(tb, MIX),
        out_shape=jax.ShapeDtypeStruct((bsz, seq, MIX), F32),
        scratch_shapes=[pltpu.VMEM((MIX, MIX), F32),
                        pltpu.VMEM((1, 4 * MIX), F32)],
        compiler_params=_params(("parallel", "arbitrary")),
    )(*args)


NSA_TQ = 128
NSA_TK = 512
NSA_CT = 512
NSA_VROWS = 80
NSA_ONEHOT = 16
GROUP = NSA_CMP_STRIDE


def _nsa_compress_kernel(xk_ref, xv_ref, pk_ref, pv_ref, k1_ref, k2_ref, v1_ref, v2_ref, o_ref):
    half = GROUP * HEAD_DIM

    def compress(x, pos, w1_ref, w2_ref):
        n = x.shape[0]
        first = _bdot(x + pos[:, :half], w1_ref[0:half, :])
        second = _bdot(x + pos[:, half:], w1_ref[half:2 * half, :])
        hid = first + pltpu.roll(second, n - 1, axis=0)
        return _bdot(_silu(hid), w2_ref[...])

    kc = compress(xk_ref[...], pk_ref[...], k1_ref, k2_ref)
    vc = compress(xv_ref[...], pv_ref[...], v1_ref, v2_ref)
    o_ref[...] = jnp.concatenate([kc, vc], axis=-1)


def _nsa_compress_call(xk, xv, pos_k, pos_v, k1, k2, v1, v2):
    bsz, ng, width = xk.shape
    args = (xk, xv, pos_k.reshape(1, -1), pos_v.reshape(1, -1), k1.astype(BF16), k2.astype(BF16),
            v1.astype(BF16), v2.astype(BF16))
    blk = pl.BlockSpec((None, ng, width), lambda b: (b, 0, 0))
    const = lambda t: pl.BlockSpec(t.shape, lambda b: (0,) * t.ndim)
    return pl.pallas_call(
        _nsa_compress_kernel,
        grid=(bsz,),
        in_specs=[blk, blk] + [const(t) for t in args[2:]],
        out_specs=pl.BlockSpec((None, ng, 2 * HEAD_DIM), lambda b: (b, 0, 0)),
        out_shape=jax.ShapeDtypeStruct((bsz, ng, 2 * HEAD_DIM), F32),
        compiler_params=_params(("parallel",)),
    )(*args)


REMOVED = -3e38


def _nsa_kernel(q_ref, g_ref, ks_ref, vs_ref, kw_ref, vw_ref, kc_ref, vc_ref, ov_ref, o_ref,
                selbias_ref, *, n_top):
    c = pl.program_id(1)
    tq = q_ref.shape[0]
    tk = ks_ref.shape[1]
    ct = kc_ref.shape[1]
    ns = ov_ref.shape[1]
    t0 = c * tq
    cols = N_HEADS * tq

    q_t = (q_ref[...] * (HEAD_DIM ** -0.5)).T
    qs = jnp.concatenate([q_t[h * HEAD_DIM:(h + 1) * HEAD_DIM] for h in range(N_HEADS)],
                         axis=1).astype(BF16)
    t_q = t0 + lax.broadcasted_iota(jnp.int32, (1, tq), 1)
    t_col = jnp.concatenate([t_q] * N_HEADS, axis=1)

    def online(carry, s, v_aug):
        m, acc = carry
        m_new = jnp.maximum(m, jnp.max(s, axis=0, keepdims=True))
        p = jnp.exp(s - m_new)
        acc = jnp.exp(m - m_new) * acc + jnp.dot(v_aug, p.astype(BF16), preferred_element_type=F32)
        return m_new, acc

    def normalise(acc):
        return acc[0:HEAD_DIM] / acc[HEAD_DIM:HEAD_DIM + 1]

    init = (jnp.full((1, cols), NEG_BIG, F32), jnp.zeros((1, cols), F32),
            jnp.zeros((HEAD_DIM, cols), F32))
    init_aug = (jnp.full((1, cols), NEG_BIG, F32), jnp.zeros((vs_ref.shape[1], cols), F32))

    n_row = lax.broadcasted_iota(jnp.int32, (ct, 1), 0)

    def cmp_step(i, carry):
        s = jnp.dot(kc_ref[i], qs, preferred_element_type=F32)
        valid = ((i * ct + n_row) * NSA_CMP_STRIDE + (NSA_CMP_BLOCK - 1)) <= t_col
        s = jnp.where(valid, s, NEG_BIG)
        m, l, acc, imp = carry
        m_new = jnp.maximum(m, jnp.max(s, axis=0, keepdims=True))
        p = jnp.where(valid, jnp.exp(s - m_new), 0.0)
        scale = jnp.exp(m - m_new)
        l = scale * l + jnp.sum(p, axis=0, keepdims=True)
        p_hi = p.astype(BF16)
        p_lo = (p - p_hi.astype(F32)).astype(BF16)
        acc = scale * acc + jnp.dot(vc_ref[i], p_hi, preferred_element_type=F32)
        ov = ov_ref[i]
        imp = (scale * imp + jnp.dot(ov, p_hi, preferred_element_type=F32)
               + jnp.dot(ov, p_lo, preferred_element_type=F32))
        return m_new, l, acc, imp

    last_valid = (t0 + tq - NSA_CMP_BLOCK) // NSA_CMP_STRIDE
    n_ctiles = jnp.minimum(last_valid // ct + 1, kc_ref.shape[0])
    _, l_c, acc_c, imp4 = lax.fori_loop(0, n_ctiles, cmp_step,
                                        init + (jnp.zeros((ns, cols), F32),))
    inv_c = jnp.where(l_c > 0.0, 1.0 / l_c, 0.0)
    o_cmp = acc_c * inv_c
    imp4 = imp4 * inv_c
    imp = imp4[:, 0:tq]
    for h in range(1, N_HEADS):
        imp = imp + imp4[:, h * tq:(h + 1) * tq]

    blk = lax.broadcasted_iota(jnp.int32, (ns, 1), 0)
    blk_f = blk.astype(F32)
    cur = t_q // NSA_SEL_BLOCK
    forced = (blk == 0) | (blk == cur) | (blk == cur - 1)
    score = jnp.where(forced, POS_BIG, jnp.where(blk <= cur, imp, NEG_BIG))
    chosen = jnp.zeros((ns, tq), jnp.bool_)
    for _ in range(n_top):
        best = jnp.max(score, axis=0, keepdims=True)
        first = jnp.min(jnp.where(score == best, blk_f, float(ns)), axis=0, keepdims=True)
        hit = blk_f == first
        chosen = chosen | hit
        score = jnp.where(hit, REMOVED, score)
    selbias_ref[...] = jnp.where(chosen, 0.0, NEG_BIG)

    per_tile = tk // NSA_SEL_BLOCK
    key_row = lax.broadcasted_iota(jnp.int32, (tk, 1), 0)

    pad_rows = jnp.zeros((ks_ref.shape[2] - HEAD_DIM - NSA_ONEHOT, cols), BF16)
    bias_pad = jnp.zeros((NSA_ONEHOT - per_tile, tq), F32)

    def sel_scores(j):
        start = pl.multiple_of(j * per_tile, per_tile)
        bias = jnp.concatenate([selbias_ref[pl.ds(start, per_tile), :], bias_pad], axis=0)
        bias = jnp.concatenate([bias.astype(BF16)] * N_HEADS, axis=1)
        rhs = jnp.concatenate([bias, qs, pad_rows], axis=0)
        return jnp.dot(ks_ref[j], rhs, preferred_element_type=F32)

    j_last = t0 // tk
    carry = lax.fori_loop(0, j_last, lambda j, cr: online(cr, sel_scores(j), vs_ref[j]), init_aug)
    causal = jnp.where((j_last * tk + key_row) <= t_q, 0.0, NEG_BIG)
    _, acc_s = online(carry, sel_scores(j_last) + jnp.concatenate([causal] * N_HEADS, axis=1),
                      vs_ref[j_last])
    o_sel = normalise(acc_s)

    tw = kw_ref.shape[1]
    n_wt = (NSA_WINDOW + tq) // tw
    jw = jnp.maximum((t0 - NSA_WINDOW) // tw, 0)
    wkey_row = lax.broadcasted_iota(jnp.int32, (tw, 1), 0)
    s_parts = []
    for i in range(n_wt):
        dist = t_q - ((jw + i) * tw + wkey_row)
        bias = jnp.where((dist >= 0) & (dist < NSA_WINDOW), 0.0, NEG_BIG)
        s_parts.append(jnp.dot(kw_ref[jw + i], qs, preferred_element_type=F32)
                       + jnp.concatenate([bias] * N_HEADS, axis=1))
    m_w = s_parts[0].max(axis=0, keepdims=True)
    for sp in s_parts[1:]:
        m_w = jnp.maximum(m_w, sp.max(axis=0, keepdims=True))
    acc_w = jnp.zeros((vw_ref.shape[1], cols), F32)
    for i, sp in enumerate(s_parts):
        p = jnp.exp(sp - m_w)
        acc_w = acc_w + jnp.dot(vw_ref[jw + i], p.astype(BF16), preferred_element_type=F32)
    o_win = normalise(acc_w)

    gates = _sigmoid(g_ref[...]).T

    def gate_row(branch):
        return jnp.concatenate([gates[branch * N_HEADS + h:branch * N_HEADS + h + 1, :]
                                for h in range(N_HEADS)], axis=1)

    out = gate_row(0) * o_cmp + gate_row(1) * o_sel + gate_row(2) * o_win
    o_ref[...] = jnp.concatenate([out[:, h * tq:(h + 1) * tq].T for h in range(N_HEADS)], axis=-1)


def _nsa_call(u_nsa, pos_k, pos_v, k1, k2, v1, v2):
    bsz, seq, _ = u_nsa.shape
    ng = seq // GROUP
    xk = u_nsa[:, :, MIX:MIX + HEAD_DIM].reshape(bsz, ng, GROUP * HEAD_DIM)
    xv = u_nsa[:, :, MIX + HEAD_DIM:MIX + 2 * HEAD_DIM].reshape(bsz, ng, GROUP * HEAD_DIM)
    kvc = _nsa_compress_call(xk, xv, pos_k, pos_v, k1, k2, v1, v2)

    n_sel = seq // NSA_SEL_BLOCK
    n_top = min(NSA_TOP_N, n_sel)
    cmp_start = jnp.arange(ng) * NSA_CMP_STRIDE
    sel_start = jnp.arange(n_sel) * NSA_SEL_BLOCK
    overlap = ((cmp_start[:, None] < sel_start[None, :] + NSA_SEL_BLOCK)
               & (cmp_start[:, None] + NSA_CMP_BLOCK > sel_start[None, :])).astype(BF16)
    tq = min(NSA_TQ, seq)
    tk = min(NSA_TK, seq)
    tw = tq
    ct = min(NSA_CT, ng)

    def key_tiles(a, tile):
        return a.astype(BF16).reshape(bsz, a.shape[1] // tile, tile, HEAD_DIM)

    def value_tiles(a, tile):
        return jnp.swapaxes(key_tiles(a, tile), -1, -2)

    def value_tiles_aug(a, tile):
        vt = value_tiles(a, tile)
        extra = jnp.zeros(vt.shape[:2] + (NSA_VROWS - HEAD_DIM, tile), BF16).at[:, :, 0, :].set(1.0)
        return jnp.concatenate([vt, extra], axis=2)

    def key_tiles_onehot(a, tile):
        kt = key_tiles(a, tile)
        local = jnp.arange(tile) // NSA_SEL_BLOCK
        onehot = (local[:, None] == jnp.arange(NSA_ONEHOT)[None, :]).astype(BF16)
        zeros = jnp.zeros(kt.shape[:3] + (128 - HEAD_DIM - NSA_ONEHOT,), BF16)
        return jnp.concatenate([jnp.broadcast_to(onehot, kt.shape[:2] + onehot.shape), kt, zeros],
                               axis=-1)

    col = lambda j: u_nsa[:, :, MIX + j * HEAD_DIM:MIX + (j + 1) * HEAD_DIM]
    operands = (u_nsa, u_nsa,
                key_tiles_onehot(col(2), tk), value_tiles_aug(col(3), tk),
                key_tiles(col(4), tw), value_tiles_aug(col(5), tw),
                key_tiles(kvc[:, :, :HEAD_DIM], ct), value_tiles(kvc[:, :, HEAD_DIM:], ct),
                jnp.swapaxes(overlap.reshape(ng // ct, ct, n_sel), -1, -2))
    per_batch = lambda t: pl.BlockSpec((None,) + t.shape[1:], lambda b, c: (b, 0, 0, 0))
    return pl.pallas_call(
        functools.partial(_nsa_kernel, n_top=n_top),
        grid=(bsz, seq // tq),
        in_specs=[
            pl.BlockSpec((None, tq, MIX), lambda b, c: (b, c, 0)),
            pl.BlockSpec((None, tq, 128), lambda b, c: (b, c, 5)),
        ] + [per_batch(t) for t in operands[2:8]] + [_const_spec(operands[8].shape)],
        out_specs=_row_spec(tq, MIX),
        out_shape=jax.ShapeDtypeStruct((bsz, seq, MIX), F32),
        scratch_shapes=[pltpu.VMEM((n_sel, tq), F32)],
        compiler_params=_params(("parallel", "arbitrary")),
    )(*operands)


HGRN_TB = 256
HGRN_C = 32


def _hgrn_kernel(u_ref, lbl_ref, ng_ref, ones_ref, o_ref, state_ref, *, layer):
    c = pl.program_id(1)

    @pl.when(c == 0)
    def _():
        state_ref[...] = jnp.zeros_like(state_ref)

    logits = lbl_ref[...]
    ex = jnp.exp(logits - jnp.max(logits, axis=0, keepdims=True))
    soft = ex / jnp.sum(ex, axis=0, keepdims=True)
    lb = jnp.sum(soft[0:layer + 1], axis=0, keepdims=True) - soft[0:1]

    u = u_ref[...]
    tb = u.shape[0]
    q = _silu(u[:, 0:MIX])
    f = lb + (1.0 - lb) * _sigmoid(u[:, MIX:2 * MIX])
    logf = jnp.log(jnp.maximum(f, GATE_FLOOR))
    k = 1.0 - f
    v = u[:, 2 * MIX:3 * MIX]
    og = u[:, 3 * MIX:4 * MIX]

    C = HGRN_C
    SUB = 8
    ones_bd = ones_ref[...]
    ti = lax.broadcasted_iota(jnp.int32, (tb, tb), 0)
    si = lax.broadcasted_iota(jnp.int32, (tb, tb), 1)
    same_chunk = (ti // C) == (si // C)
    b_all = _split_dot((same_chunk & (ti >= si)).astype(BF16), logf, 3)
    b_last_all = _split_dot(same_chunk.astype(BF16), logf, 3)
    b2_all = b_all * math.log2(math.e)
    qe_all = q * jnp.exp(b_all)
    kd_all = k * jnp.exp(b_last_all - b_all)
    g_last_all = jnp.exp(b_last_all)
    row8 = lax.broadcasted_iota(jnp.int32, (SUB, 1), 0)
    hr = lax.broadcasted_iota(jnp.int32, (MIX, MIX), 0) // HEAD_DIM
    hc_ = lax.broadcasted_iota(jnp.int32, (MIX, MIX), 1) // HEAD_DIM
    same_head = hr == hc_

    state = state_ref[...]
    o_chunks = []
    for ch in range(tb // C):
        sl = slice(ch * C, (ch + 1) * C)
        qc, kc, vc, b2 = q[sl], k[sl], v[sl], b2_all[sl]
        pieces = []
        for s in range(C):
            r0 = (s // SUB) * SUB
            pm = qc[r0:] * (kc[s:s + 1, :] * jnp.exp2(b2[r0:] - b2[s:s + 1, :]))
            top = jnp.where(row8 + r0 >= s, pm[0:SUB], 0.0)
            pieces.append(top if C - r0 == SUB else jnp.concatenate([top, pm[SUB:]], axis=0))
        attn = jnp.dot(jnp.concatenate(pieces, axis=0).astype(BF16), ones_bd,
                       preferred_element_type=F32)
        groups = [jnp.zeros((SUB, MIX), F32) for _ in range(C // SUB)]
        off = 0
        for s in range(C):
            g0 = s // SUB
            for g in range(g0, C // SUB):
                groups[g] = groups[g] + attn[off:off + SUB, :] * vc[s:s + 1, :]
                off += SUB
        o = jnp.concatenate(groups, axis=0)
        o_chunks.append(o + _bdot_nt(qe_all[sl], state))
        outer = jnp.dot(vc.T.astype(BF16), kd_all[sl].astype(BF16), preferred_element_type=F32)
        state = state * g_last_all[ch * C:ch * C + 1, :] + jnp.where(same_head, outer, 0.0)
    state_ref[...] = state
    o = jnp.concatenate(o_chunks, axis=0)
    ms = _head_sum_mxu(o * o, ones_bd) * (1.0 / HEAD_DIM)
    o_ref[...] = o * lax.rsqrt(ms + NORM_EPS) * ng_ref[...] * _silu(og)


def _block_ones():
    hid = jnp.arange(MIX) // HEAD_DIM
    return (hid[:, None] == hid[None, :]).astype(BF16)


def _hgrn_call(u, lb_logits, norm_g, layer):
    bsz, seq, _ = u.shape
    tb = min(HGRN_TB, seq)
    args = (u, lb_logits.astype(F32), norm_g.reshape(1, -1).astype(F32), _block_ones())
    return pl.pallas_call(
        functools.partial(_hgrn_kernel, layer=layer),
        grid=(bsz, seq // tb),
        in_specs=[_row_spec(tb, 4 * MIX)] + [_const_spec(t.shape) for t in args[1:]],
        out_specs=_row_spec(tb, MIX),
        out_shape=jax.ShapeDtypeStruct((bsz, seq, MIX), F32),
        scratch_shapes=[pltpu.VMEM((MIX, MIX), F32)],
        compiler_params=_params(("parallel", "arbitrary")),
    )(*args)


RET_C = 256


def _ret_kernel(u_ref, cos_ref, sin_ref, dm_ref, qd_ref, kd_ref, cd_ref, ng_ref, nb_ref,
                o_ref, state_ref):
    c = pl.program_id(1)

    @pl.when(c == 0)
    def _():
        state_ref[...] = jnp.zeros_like(state_ref)

    u = u_ref[...]
    cosf = cos_ref[...]
    sins = sin_ref[...]
    half = HEAD_DIM // 2

    def rope(a):
        outs = []
        for j in range(MIX // 128):
            blk = a[:, j * 128:(j + 1) * 128]
            lane = lax.broadcasted_iota(jnp.int32, blk.shape, 1)
            swapped = jnp.where((lane % HEAD_DIM) < half, pltpu.roll(blk, 128 - half, axis=1),
                                pltpu.roll(blk, half, axis=1))
            outs.append(swapped)
        return a * cosf + jnp.concatenate(outs, axis=-1) * sins

    q = rope(u[:, 0:MIX])
    k = rope(u[:, MIX:2 * MIX]) * (HEAD_DIM ** -0.5)
    v = u[:, 2 * MIX:3 * MIX]
    g = u[:, 3 * MIX:4 * MIX]
    qd = q * qd_ref[...]
    kd = k * kd_ref[...]
    cd = cd_ref[...]
    o_heads = []
    for h in range(N_HEADS):
        hs = slice(h * HEAD_DIM, (h + 1) * HEAD_DIM)
        s = _bdot_nt(q[:, hs], k[:, hs]) * dm_ref[h]
        st = state_ref[h]
        o_heads.append(_bdot(s, v[:, hs]) + _bdot(qd[:, hs], st))
        state_ref[h] = st * cd[:, hs] + _bdot(kd[:, hs].T, v[:, hs])
    o = jnp.concatenate(o_heads, axis=-1)
    mean = _head_sum(o) * (1.0 / HEAD_DIM)
    oc = o - mean
    var = _head_sum(oc * oc) * (1.0 / HEAD_DIM)
    y = oc * lax.rsqrt(var + RET_GN_EPS) * ng_ref[...] + nb_ref[...]
    o_ref[...] = y * _silu(g)


def _ret_call(u, norm_g, norm_b):
    bsz, seq, _ = u.shape
    C = min(RET_C, seq)
    half = HEAD_DIM // 2
    pos = jnp.arange(seq, dtype=F32)
    inv_freq = RET_ROPE_BASE ** (-jnp.arange(0, HEAD_DIM, 2, dtype=F32) / HEAD_DIM)
    ang = pos[:, None] * inv_freq[None, :]
    cos, sin = jnp.cos(ang), jnp.sin(ang)
    cosf = jnp.tile(jnp.concatenate([cos, cos], axis=-1), (1, N_HEADS))
    sins = jnp.tile(jnp.concatenate([-sin, sin], axis=-1), (1, N_HEADS))
    log_gamma = jnp.log(1.0 - jnp.exp2(-5.0 - jnp.arange(N_HEADS, dtype=F32)))
    i = jnp.arange(C, dtype=F32)
    dpos = i[:, None] - i[None, :]
    dm = jnp.where(dpos >= 0, jnp.exp(jnp.maximum(dpos, 0.0)[None] * log_gamma[:, None, None]), 0.0)
    lanes = lambda t: jnp.repeat(t, HEAD_DIM, axis=-1)
    qd = lanes(jnp.exp((i + 1.0)[:, None] * log_gamma[None, :]))
    kd = lanes(jnp.exp((C - 1.0 - i)[:, None] * log_gamma[None, :]))
    cd = lanes(jnp.exp(C * log_gamma)[None, :])
    args = (u, cosf, sins, dm, qd, kd, cd, norm_g.reshape(1, -1).astype(F32),
            norm_b.reshape(1, -1).astype(F32))
    in_specs = ([_row_spec(C, 4 * MIX),
                 pl.BlockSpec((C, MIX), lambda b, c: (c, 0)),
                 pl.BlockSpec((C, MIX), lambda b, c: (c, 0))]
                + [_const_spec(t.shape) for t in args[3:]])
    return pl.pallas_call(
        _ret_kernel,
        grid=(bsz, seq // C),
        in_specs=in_specs,
        out_specs=_row_spec(C, MIX),
        out_shape=jax.ShapeDtypeStruct((bsz, seq, MIX), F32),
        scratch_shapes=[pltpu.VMEM((N_HEADS, HEAD_DIM, HEAD_DIM), F32)],
        compiler_params=_params(("parallel", "arbitrary")),
    )(*args)


DENSE_TM = 512
FFN_TF = 256
GROUP_COLS = (NSA_PAD, 4 * MIX, 4 * MIX, 4 * MIX)


def _rmsnorm(x, g):
    return x * lax.rsqrt(jnp.mean(x * x, axis=-1, keepdims=True) + NORM_EPS) * g


def _resident(shape):
    nd = len(shape)
    return pl.BlockSpec(shape, lambda b, c: (0,) * nd, pipeline_mode=pl.Buffered(1))


def _inproj_kernel(h_ref, g_ref, w_ref, nsa_ref, hg_ref, rt_ref, rw_ref):
    xn = _rmsnorm(h_ref[...], g_ref[...]).astype(BF16)
    off = 0
    for ref, width in zip((nsa_ref, hg_ref, rt_ref, rw_ref), GROUP_COLS):
        ref[...] = jnp.dot(xn, w_ref[:, off:off + width], preferred_element_type=F32)
        off += width


def _inproj_call(h, g, w_pad):
    bsz, seq, _ = h.shape
    tm = min(DENSE_TM, seq)
    return pl.pallas_call(
        _inproj_kernel,
        grid=(bsz, seq // tm),
        in_specs=[_row_spec(tm, D_MODEL), _resident((1, D_MODEL)), _resident(w_pad.shape)],
        out_specs=[_row_spec(tm, w) for w in GROUP_COLS],
        out_shape=[jax.ShapeDtypeStruct((bsz, seq, w), F32) for w in GROUP_COLS],
        compiler_params=_params(("parallel", "parallel")),
    )(h, g.reshape(1, -1), w_pad)


def _merge_kernel(h_ref, b0_ref, b1_ref, b2_ref, b3_ref, g_ref, wg_ref, bg_ref, wb_ref, wo_ref,
                  o_ref):
    h = h_ref[...]
    xn = _rmsnorm(h, g_ref[...]).astype(BF16)
    merged = None
    for m, b_ref in enumerate((b0_ref, b1_ref, b2_ref, b3_ref)):
        gate = _sigmoid(jnp.dot(xn, wg_ref[m], preferred_element_type=F32) + bg_ref[m])
        term = gate * _bdot(b_ref[...], wb_ref[m])
        merged = term if merged is None else merged + term
    o_ref[...] = h + _bdot(merged, wo_ref[...])


def _merge_call(h, branches, g, w_gate, b_gate, w_branch, w_out):
    bsz, seq, _ = h.shape
    tm = min(DENSE_TM, seq)
    consts = (g.reshape(1, -1), w_gate, b_gate.reshape(4, 1, D_MODEL), w_branch, w_out)
    return pl.pallas_call(
        _merge_kernel,
        grid=(bsz, seq // tm),
        in_specs=([_row_spec(tm, D_MODEL)] + [_row_spec(tm, MIX)] * 4
                  + [_resident(t.shape) for t in consts]),
        out_specs=_row_spec(tm, D_MODEL),
        out_shape=jax.ShapeDtypeStruct(h.shape, F32),
        compiler_params=_params(("parallel", "parallel")),
    )(h, *branches, *consts)


def _ffn_kernel(h_ref, g_ref, wg_ref, wu_ref, wd_ref, o_ref):
    h = h_ref[...]
    hn = _rmsnorm(h, g_ref[...]).astype(BF16)
    acc = h
    for f in range(0, D_FF, FFN_TF):
        gate = jnp.dot(hn, wg_ref[:, f:f + FFN_TF], preferred_element_type=F32)
        up = jnp.dot(hn, wu_ref[:, f:f + FFN_TF], preferred_element_type=F32)
        acc = acc + _bdot(_silu(gate) * up, wd_ref[f:f + FFN_TF, :])
    o_ref[...] = acc


def _ffn_call(h, g, w_gate, w_up, w_down):
    bsz, seq, _ = h.shape
    tm = min(DENSE_TM, seq)
    consts = (g.reshape(1, -1), w_gate, w_up, w_down)
    return pl.pallas_call(
        _ffn_kernel,
        grid=(bsz, seq // tm),
        in_specs=[_row_spec(tm, D_MODEL)] + [_resident(t.shape) for t in consts],
        out_specs=_row_spec(tm, D_MODEL),
        out_shape=jax.ShapeDtypeStruct(h.shape, F32),
        compiler_params=_params(("parallel", "parallel")),
    )(h, *consts)


def _ple_kernel(h_ref, p_ref, g_ref, wg_ref, wp_ref, gf_ref, o_ref, *, final_norm):
    h = h_ref[...]
    hp = _rmsnorm(h, g_ref[...])
    out = h + _sigmoid(_bdot(hp, wg_ref[...])) * _bdot(p_ref[...], wp_ref[...])
    if final_norm:
        out = _rmsnorm(out, gf_ref[...])
    o_ref[...] = out


def _ple_call(h, p, g, w_gate, w_proj, g_final, final_norm):
    bsz, seq, _ = h.shape
    tm = min(DENSE_TM, seq)
    consts = (g.reshape(1, -1), w_gate, w_proj, g_final.reshape(1, -1))
    return pl.pallas_call(
        functools.partial(_ple_kernel, final_norm=final_norm),
        grid=(bsz, seq // tm),
        in_specs=([_row_spec(tm, D_MODEL), _row_spec(tm, PLE_DIM)]
                  + [_resident(t.shape) for t in consts]),
        out_specs=_row_spec(tm, D_MODEL),
        out_shape=jax.ShapeDtypeStruct(h.shape, F32),
        compiler_params=_params(("parallel", "parallel")),
    )(h, p, *consts)


def kernel(x, p, norm_mix, w_in, nsa_pos_k, nsa_pos_v, nsa_cmp_k1, nsa_cmp_k2, nsa_cmp_v1,
           nsa_cmp_v2, hgrn_lb_logits, hgrn_norm, ret_norm_g, ret_norm_b, rwkv_mu, rwkv_w0,
           rwkv_w_up, rwkv_a0, rwkv_a_up, rwkv_g_up, rwkv_k_k, rwkv_k_a, rwkv_r_k, rwkv_norm_g,
           rwkv_norm_b, w_branch, w_gate, b_gate, w_out, norm_ffn, w_ffn_gate, w_ffn_up,
           w_ffn_down, norm_ple, w_ple_gate, w_ple_proj, norm_final):
    depth = w_in.shape[0]
    w_in_pad = jnp.concatenate(
        [w_in[:, :, :NSA_WIDTH], jnp.zeros((depth, D_MODEL, NSA_PAD - NSA_WIDTH), w_in.dtype),
         w_in[:, :, NSA_WIDTH:]], axis=-1).astype(BF16)
    bf = lambda t: t.astype(BF16)
    h = x
    for i in range(depth):
        u_nsa, u_hgrn, u_ret, u_rwkv = _inproj_call(h, norm_mix[i], w_in_pad[i])
        branches = (
            _nsa_call(u_nsa, nsa_pos_k[i], nsa_pos_v[i], nsa_cmp_k1[i], nsa_cmp_k2[i],
                      nsa_cmp_v1[i], nsa_cmp_v2[i]),
            _hgrn_call(u_hgrn, hgrn_lb_logits, hgrn_norm[i], i),
            _ret_call(u_ret, ret_norm_g[i], ret_norm_b[i]),
            _rwkv_call(u_rwkv, rwkv_mu[i], rwkv_w0[i], rwkv_w_up[i], rwkv_a0[i], rwkv_a_up[i],
                       rwkv_g_up[i], rwkv_k_k[i], rwkv_k_a[i], rwkv_r_k[i], rwkv_norm_g[i],
                       rwkv_norm_b[i]),
        )
        h = _merge_call(h, branches, norm_mix[i], bf(w_gate[i]), b_gate[i], bf(w_branch[i]),
                        bf(w_out[i]))
        h = _ffn_call(h, norm_ffn[i], bf(w_ffn_gate[i]), bf(w_ffn_up[i]), bf(w_ffn_down[i]))
        h = _ple_call(h, p[i], norm_ple[i], bf(w_ple_gate[i]), bf(w_ple_proj[i]), norm_final,
                      final_norm=(i == depth - 1))
    return h
```

```python
import functools
import math

import jax
import jax.numpy as jnp
from jax import lax
from jax.experimental import pallas as pl
from jax.experimental.pallas import tpu as pltpu

F32 = jnp.float32
BF16 = jnp.bfloat16

D_MODEL = 1024
N_HEADS = 4
HEAD_DIM = 64
MIX = N_HEADS * HEAD_DIM
D_FF = 2816
PLE_DIM = 256
NORM_EPS = 1e-6
NEG_BIG = -1e30
POS_BIG = 1e30
GATE_FLOOR = 1e-20

NSA_CMP_BLOCK = 32
NSA_CMP_STRIDE = 16
NSA_SEL_BLOCK = 64
NSA_TOP_N = 16
NSA_WINDOW = 512
NSA_CMP_HIDDEN = 128
NSA_WIDTH = 652
NSA_PAD = 768

RET_ROPE_BASE = 10000.0
RET_GN_EPS = 1e-5
RWKV_GN_EPS = 64e-5

VMEM_LIMIT = 56 * 1024 * 1024


def _bdot(a, b):
    return jnp.dot(a.astype(BF16), b.astype(BF16), preferred_element_type=F32)


def _bdot_nt(a, b):
    return lax.dot_general(a.astype(BF16), b.astype(BF16), (((1,), (1,)), ((), ())),
                           preferred_element_type=F32)


def _sigmoid(x):
    return 1.0 / (1.0 + jnp.exp(-x))


def _silu(x):
    return x * _sigmoid(x)


def _params(sem):
    return pltpu.CompilerParams(dimension_semantics=sem, vmem_limit_bytes=VMEM_LIMIT)


def _row_spec(tile, width):
    return pl.BlockSpec((None, tile, width), lambda b, c: (b, c, 0))


def _const_spec(shape):
    nd = len(shape)
    return pl.BlockSpec(shape, lambda b, c: (0,) * nd)


RWKV_TB = 256
RWKV_C = 64


def _split_dot(a_bf16, x, parts):
    total = None
    rest = x
    for _ in range(parts):
        piece = rest.astype(BF16)
        rest = rest - piece.astype(F32)
        term = jnp.dot(a_bf16, piece, preferred_element_type=F32)
        total = term if total is None else total + term
    return total


def _head_sum_mxu(x, ones_t):
    hi = x.astype(BF16)
    lo = (x - hi.astype(F32)).astype(BF16)
    return (jnp.dot(hi, ones_t, preferred_element_type=F32)
            + jnp.dot(lo, ones_t, preferred_element_type=F32))


def _rwkv_kernel(u_ref, mu_ref, w0_ref, wup_ref, a0_ref, aup_ref, gup_ref, kk_ref, ka_ref,
                 rk_ref, ng_ref, nb_ref, ones_ref, o_ref, state_ref, prev_ref):
    c = pl.program_id(1)
    ones = ones_ref[...]

    @pl.when(c == 0)
    def _():
        state_ref[...] = jnp.zeros_like(state_ref)
        prev_ref[...] = jnp.zeros_like(prev_ref)

    u = u_ref[...]
    tb = u.shape[0]
    row = lax.broadcasted_iota(jnp.int32, u.shape, 0)
    u_prev = jnp.where(row == 0, prev_ref[...], pltpu.roll(u, 1, axis=0))
    prev_ref[...] = u[tb - 1:tb, :]
    xs = u + mu_ref[...] * (u_prev - u)
    r = xs[:, 0:MIX]
    k = xs[:, MIX:2 * MIX]
    v = xs[:, 2 * MIX:3 * MIX]
    w_lo = xs[:, 3 * MIX:3 * MIX + 64]
    a_lo = xs[:, 3 * MIX + 64:3 * MIX + 128]
    g_lo = xs[:, 3 * MIX + 128:3 * MIX + 256]

    logw = -math.exp(-0.5) * _sigmoid(w0_ref[...] + _bdot(jnp.tanh(w_lo), wup_ref[...]))
    a = _sigmoid(a0_ref[...] + _bdot(a_lo, aup_ref[...]))
    g = _bdot(_sigmoid(g_lo), gup_ref[...])
    kk = k * kk_ref[...]
    kk = kk * lax.rsqrt(jnp.maximum(_head_sum_mxu(kk * kk, ones), 1e-24))
    k2 = k * (1.0 + (a - 1.0) * ka_ref[...])
    alpha = -kk
    beta = kk * a
    bonus = _head_sum_mxu(r * k2 * rk_ref[...], ones) * v

    C = RWKV_C
    ti = lax.broadcasted_iota(jnp.int32, (tb, tb), 0)
    si = lax.broadcasted_iota(jnp.int32, (tb, tb), 1)
    same_chunk = (ti // C) == (si // C)
    prefix = (same_chunk & (ti >= si)).astype(BF16)
    whole = same_chunk.astype(BF16)
    cum = _split_dot(prefix, logw, 3)
    cum_last = _split_dot(whole, logw, 3)
    gam_all = jnp.exp(cum_last)
    e_inv = jnp.exp(-cum)
    e_last = jnp.exp(cum_last - cum)
    ag_all = alpha * jnp.exp(cum - logw)
    rg_all = r * jnp.exp(cum)
    bi_all = beta * e_inv
    ki_all = k2 * e_inv
    bl_all = beta * e_last
    kl_all = k2 * e_last

    hc = N_HEADS * C
    row_head = lax.broadcasted_iota(jnp.int32, (hc, MIX), 0) // C
    lane_head = lax.broadcasted_iota(jnp.int32, (hc, MIX), 1) // HEAD_DIM
    own = row_head == lane_head

    def stack(x):
        return jnp.where(own, jnp.concatenate([x] * N_HEADS, axis=0), 0.0).astype(BF16)

    rr = lax.broadcasted_iota(jnp.int32, (hc, hc), 0)
    cc = lax.broadcasted_iota(jnp.int32, (hc, hc), 1)
    strict = rr > cc
    incl = rr >= cc
    eye_hc = (rr == cc).astype(F32)
    kr = lax.broadcasted_iota(jnp.int32, (MIX, MIX), 0)
    kc_ = lax.broadcasted_iota(jnp.int32, (MIX, MIX), 1)
    eye_k = kr == kc_

    state = state_ref[...]
    y_chunks = []
    for ch in range(tb // C):
        sl = slice(ch * C, (ch + 1) * C)
        ag, rg, bi, ki, bl, kl, vm = (stack(t[sl]) for t in (ag_all, rg_all, bi_all, ki_all,
                                                               bl_all, kl_all, v))
        aa = _bdot_nt(jnp.concatenate([ag, rg], axis=0), jnp.concatenate([bi, ki], axis=0))
        a_ab = jnp.where(strict, aa[:hc, :hc], 0.0)
        a_ak = jnp.where(strict, aa[:hc, hc:], 0.0)
        a_rb = jnp.where(incl, aa[hc:, :hc], 0.0)
        a_rk = jnp.where(incl, aa[hc:, hc:], 0.0)
        pw = a_ab
        t_inv = eye_hc + pw
        for _ in range(int(math.log2(C)) - 1):
            pw = _bdot(pw, pw)
            t_inv = t_inv + _bdot(t_inv, pw)
        w12 = _bdot(t_inv, jnp.concatenate([ag.astype(F32), _bdot(a_ak, vm)], axis=-1))
        ry = _bdot(a_rb, w12)
        rq = rg.astype(F32) + ry[:, :MIX]
        y0 = ry[:, MIX:] + _bdot(a_rk, vm)
        mn = _bdot(bl.T, w12)
        m_mat = jnp.where(eye_k, gam_all[ch * C:ch * C + 1, :], 0.0) + mn[:, :MIX]
        n_mat = mn[:, MIX:] + _bdot(kl.T, vm)
        ym = _split_dot(rq.astype(BF16), state, 2) + y0
        y_chunks.append(ym[0:C] + ym[C:2 * C] + ym[2 * C:3 * C] + ym[3 * C:4 * C])
        state = _split_dot(m_mat.astype(BF16), state, 2) + n_mat
    state_ref[...] = state
    y = jnp.concatenate(y_chunks, axis=0)

    mean = _head_sum_mxu(y, ones) * (1.0 / HEAD_DIM)
    yc = y - mean
    var = _head_sum_mxu(yc * yc, ones) * (1.0 / HEAD_DIM)
    yn = yc * lax.rsqrt(var + RWKV_GN_EPS) * ng_ref[...] + nb_ref[...]
    o_ref[...] = (yn + bonus) * g


def _rwkv_call(u, mu, w0, w_up, a0, a_up, g_up, k_k, k_a, r_k, norm_g, norm_b):
    bsz, seq, _ = u.shape
    tb = min(RWKV_TB, seq)
    vec = lambda t: t.reshape(1, -1).astype(F32)
    args = (u, vec(mu), vec(w0), w_up, vec(a0), a_up, g_up, vec(k_k), vec(k_a), vec(r_k),
            vec(norm_g), vec(norm_b), _block_ones())
    in_specs = [_row_spec(tb, 4 * MIX)] + [_const_spec(t.shape) for t in args[1:]]
    return pl.pallas_call(
        _rwkv_kernel,
        grid=(bsz, seq // tb),
        in_specs=in_specs,
        out_specs=_row_spec(tb, MIX),
        out_shape=jax.ShapeDtypeStruct((bsz, seq, MIX), F32),
        scratch_shapes=[pltpu.VMEM((MIX, MIX), F32),
                        pltpu.VMEM((1, 4 * MIX), F32)],
        compiler_params=_params(("parallel", "arbitrary")),
    )(*args)


NSA_TQ = 256
NSA_TK = 512
NSA_CT = 512
NSA_VROWS = 80
NSA_UNROLL = 2
NSA_ONEHOT = 16
GROUP = NSA_CMP_STRIDE


def _nsa_compress_kernel(xk_ref, xv_ref, pk_ref, pv_ref, k1_ref, k2_ref, v1_ref, v2_ref, o_ref):
    half = GROUP * HEAD_DIM

    def compress(x, pos, w1_ref, w2_ref):
        n = x.shape[0]
        first = _bdot(x + pos[:, :half], w1_ref[0:half, :])
        second = _bdot(x + pos[:, half:], w1_ref[half:2 * half, :])
        hid = first + pltpu.roll(second, n - 1, axis=0)
        return _bdot(_silu(hid), w2_ref[...])

    kc = compress(xk_ref[...], pk_ref[...], k1_ref, k2_ref)
    vc = compress(xv_ref[...], pv_ref[...], v1_ref, v2_ref)
    o_ref[...] = jnp.concatenate([kc, vc], axis=-1)


def _nsa_compress_call(xk, xv, pos_k, pos_v, k1, k2, v1, v2):
    bsz, ng, width = xk.shape
    args = (xk, xv, pos_k.reshape(1, -1), pos_v.reshape(1, -1), k1.astype(BF16), k2.astype(BF16),
            v1.astype(BF16), v2.astype(BF16))
    blk = pl.BlockSpec((None, ng, width), lambda b: (b, 0, 0))
    const = lambda t: pl.BlockSpec(t.shape, lambda b: (0,) * t.ndim)
    return pl.pallas_call(
        _nsa_compress_kernel,
        grid=(bsz,),
        in_specs=[blk, blk] + [const(t) for t in args[2:]],
        out_specs=pl.BlockSpec((None, ng, 2 * HEAD_DIM), lambda b: (b, 0, 0)),
        out_shape=jax.ShapeDtypeStruct((bsz, ng, 2 * HEAD_DIM), F32),
        compiler_params=_params(("parallel",)),
    )(*args)


REMOVED = -3e38


def _nsa_kernel(q_ref, g_ref, ks_ref, vs_ref, kw_ref, vw_ref, kc_ref, vc_ref, ov_ref, o_ref,
                selbias_ref, *, n_top):
    c = pl.program_id(1)
    tq = q_ref.shape[0]
    tk = ks_ref.shape[1]
    ct = kc_ref.shape[1]
    ns = ov_ref.shape[1]
    t0 = c * tq
    cols = N_HEADS * tq

    q_t = (q_ref[...] * (HEAD_DIM ** -0.5)).T
    qs = jnp.concatenate([q_t[h * HEAD_DIM:(h + 1) * HEAD_DIM] for h in range(N_HEADS)],
                         axis=1).astype(BF16)
    t_q = t0 + lax.broadcasted_iota(jnp.int32, (1, tq), 1)
    t_col = jnp.concatenate([t_q] * N_HEADS, axis=1)

    def online(carry, s, v_aug):
        m, acc = carry
        m_new = jnp.maximum(m, jnp.max(s, axis=0, keepdims=True))
        p = jnp.exp(s - m_new)
        acc = jnp.exp(m - m_new) * acc + jnp.dot(v_aug, p.astype(BF16), preferred_element_type=F32)
        return m_new, acc

    def normalise(acc):
        return acc[0:HEAD_DIM] / acc[HEAD_DIM:HEAD_DIM + 1]

    init = (jnp.full((1, cols), NEG_BIG, F32), jnp.zeros((1, cols), F32),
            jnp.zeros((HEAD_DIM, cols), F32))
    init_aug = (jnp.full((1, cols), NEG_BIG, F32), jnp.zeros((vs_ref.shape[1], cols), F32))

    n_row = lax.broadcasted_iota(jnp.int32, (ct, 1), 0)

    def cmp_step(i, carry):
        s = jnp.dot(kc_ref[i], qs, preferred_element_type=F32)
        valid = ((i * ct + n_row) * NSA_CMP_STRIDE + (NSA_CMP_BLOCK - 1)) <= t_col
        s = jnp.where(valid, s, NEG_BIG)
        m, l, acc, imp = carry
        m_new = jnp.maximum(m, jnp.max(s, axis=0, keepdims=True))
        p = jnp.where(valid, jnp.exp(s - m_new), 0.0)
        scale = jnp.exp(m - m_new)
        l = scale * l + jnp.sum(p, axis=0, keepdims=True)
        p_hi = p.astype(BF16)
        p_lo = (p - p_hi.astype(F32)).astype(BF16)
        acc = scale * acc + jnp.dot(vc_ref[i], p_hi, preferred_element_type=F32)
        ov = ov_ref[i]
        imp = (scale * imp + jnp.dot(ov, p_hi, preferred_element_type=F32)
               + jnp.dot(ov, p_lo, preferred_element_type=F32))
        return m_new, l, acc, imp

    last_valid = (t0 + tq - NSA_CMP_BLOCK) // NSA_CMP_STRIDE
    n_ctiles = jnp.minimum(last_valid // ct + 1, kc_ref.shape[0])
    _, l_c, acc_c, imp4 = lax.fori_loop(0, n_ctiles, cmp_step,
                                        init + (jnp.zeros((ns, cols), F32),))
    inv_c = jnp.where(l_c > 0.0, 1.0 / l_c, 0.0)
    o_cmp = acc_c * inv_c
    imp4 = imp4 * inv_c
    imp = imp4[:, 0:tq]
    for h in range(1, N_HEADS):
        imp = imp + imp4[:, h * tq:(h + 1) * tq]

    blk = lax.broadcasted_iota(jnp.int32, (ns, 1), 0)
    blk_f = blk.astype(F32)
    cur = t_q // NSA_SEL_BLOCK
    forced = (blk == 0) | (blk == cur) | (blk == cur - 1)
    score = jnp.where(forced, POS_BIG, jnp.where(blk <= cur, imp, NEG_BIG))
    chosen = jnp.zeros((ns, tq), jnp.bool_)
    for _ in range(n_top):
        best = jnp.max(score, axis=0, keepdims=True)
        first = jnp.min(jnp.where(score == best, blk_f, float(ns)), axis=0, keepdims=True)
        hit = blk_f == first
        chosen = chosen | hit
        score = jnp.where(hit, REMOVED, score)
    selbias_ref[...] = jnp.where(chosen, 0.0, NEG_BIG)

    per_tile = tk // NSA_SEL_BLOCK
    key_row = lax.broadcasted_iota(jnp.int32, (tk, 1), 0)

    pad_rows = jnp.zeros((ks_ref.shape[2] - HEAD_DIM - NSA_ONEHOT, cols), BF16)
    bias_pad = jnp.zeros((NSA_ONEHOT - per_tile, tq), F32)

    def sel_scores(j, live=None):
        start = pl.multiple_of(j * per_tile, per_tile)
        bias = selbias_ref[pl.ds(start, per_tile), :]
        if live is not None:
            bias = jnp.where(live, bias, NEG_BIG)
        bias = jnp.concatenate([bias, bias_pad], axis=0)
        bias = jnp.concatenate([bias.astype(BF16)] * N_HEADS, axis=1)
        rhs = jnp.concatenate([bias, qs, pad_rows], axis=0)
        return jnp.dot(ks_ref[j], rhs, preferred_element_type=F32)

    j_last = t0 // tk

    def sel_group(i, carry):
        tiles = []
        for u in range(NSA_UNROLL):
            j = i * NSA_UNROLL + u
            tiles.append((jnp.minimum(j, j_last - 1), None if u == 0 else j < j_last))
        scores = [sel_scores(j, live) for j, live in tiles]
        for (j, _), s in zip(tiles, scores):
            carry = online(carry, s, vs_ref[j])
        return carry

    carry = lax.fori_loop(0, (j_last + NSA_UNROLL - 1) // NSA_UNROLL, sel_group, init_aug)
    causal = jnp.where((j_last * tk + key_row) <= t_q, 0.0, NEG_BIG)
    _, acc_s = online(carry, sel_scores(j_last) + jnp.concatenate([causal] * N_HEADS, axis=1),
                      vs_ref[j_last])
    o_sel = normalise(acc_s)

    tw = kw_ref.shape[1]
    n_wt = (NSA_WINDOW + tq) // tw
    jw = jnp.maximum((t0 - NSA_WINDOW) // tw, 0)
    wkey_row = lax.broadcasted_iota(jnp.int32, (tw, 1), 0)
    s_parts = []
    for i in range(n_wt):
        dist = t_q - ((jw + i) * tw + wkey_row)
        bias = jnp.where((dist >= 0) & (dist < NSA_WINDOW), 0.0, NEG_BIG)
        s_parts.append(jnp.dot(kw_ref[jw + i], qs, preferred_element_type=F32)
                       + jnp.concatenate([bias] * N_HEADS, axis=1))
    m_w = s_parts[0].max(axis=0, keepdims=True)
    for sp in s_parts[1:]:
        m_w = jnp.maximum(m_w, sp.max(axis=0, keepdims=True))
    acc_w = jnp.zeros((vw_ref.shape[1], cols), F32)
    for i, sp in enumerate(s_parts):
        p = jnp.exp(sp - m_w)
        acc_w = acc_w + jnp.dot(vw_ref[jw + i], p.astype(BF16), preferred_element_type=F32)
    o_win = normalise(acc_w)

    gates = _sigmoid(g_ref[...]).T

    def gate_row(branch):
        return jnp.concatenate([gates[branch * N_HEADS + h:branch * N_HEADS + h + 1, :]
                                for h in range(N_HEADS)], axis=1)

    out = gate_row(0) * o_cmp + gate_row(1) * o_sel + gate_row(2) * o_win
    o_ref[...] = jnp.concatenate([out[:, h * tq:(h + 1) * tq].T for h in range(N_HEADS)], axis=-1)


def _nsa_call(u_nsa, pos_k, pos_v, k1, k2, v1, v2):
    bsz, seq, _ = u_nsa.shape
    ng = seq // GROUP
    xk = u_nsa[:, :, MIX:MIX + HEAD_DIM].reshape(bsz, ng, GROUP * HEAD_DIM)
    xv = u_nsa[:, :, MIX + HEAD_DIM:MIX + 2 * HEAD_DIM].reshape(bsz, ng, GROUP * HEAD_DIM)
    kvc = _nsa_compress_call(xk, xv, pos_k, pos_v, k1, k2, v1, v2)

    n_sel = seq // NSA_SEL_BLOCK
    n_top = min(NSA_TOP_N, n_sel)
    cmp_start = jnp.arange(ng) * NSA_CMP_STRIDE
    sel_start = jnp.arange(n_sel) * NSA_SEL_BLOCK
    overlap = ((cmp_start[:, None] < sel_start[None, :] + NSA_SEL_BLOCK)
               & (cmp_start[:, None] + NSA_CMP_BLOCK > sel_start[None, :])).astype(BF16)
    tq = min(NSA_TQ, seq)
    tk = min(NSA_TK, seq)
    tw = tq
    ct = min(NSA_CT, ng)

    def key_tiles(a, tile):
        return a.astype(BF16).reshape(bsz, a.shape[1] // tile, tile, HEAD_DIM)

    def value_tiles(a, tile):
        return jnp.swapaxes(key_tiles(a, tile), -1, -2)

    def value_tiles_aug(a, tile):
        vt = value_tiles(a, tile)
        extra = jnp.zeros(vt.shape[:2] + (NSA_VROWS - HEAD_DIM, tile), BF16).at[:, :, 0, :].set(1.0)
        return jnp.concatenate([vt, extra], axis=2)

    def key_tiles_onehot(a, tile):
        kt = key_tiles(a, tile)
        local = jnp.arange(tile) // NSA_SEL_BLOCK
        onehot = (local[:, None] == jnp.arange(NSA_ONEHOT)[None, :]).astype(BF16)
        zeros = jnp.zeros(kt.shape[:3] + (128 - HEAD_DIM - NSA_ONEHOT,), BF16)
        return jnp.concatenate([jnp.broadcast_to(onehot, kt.shape[:2] + onehot.shape), kt, zeros],
                               axis=-1)

    col = lambda j: u_nsa[:, :, MIX + j * HEAD_DIM:MIX + (j + 1) * HEAD_DIM]
    operands = (u_nsa, u_nsa,
                key_tiles_onehot(col(2), tk), value_tiles_aug(col(3), tk),
                key_tiles(col(4), tw), value_tiles_aug(col(5), tw),
                key_tiles(kvc[:, :, :HEAD_DIM], ct), value_tiles(kvc[:, :, HEAD_DIM:], ct),
                jnp.swapaxes(overlap.reshape(ng // ct, ct, n_sel), -1, -2))
    per_batch = lambda t: pl.BlockSpec((None,) + t.shape[1:], lambda b, c: (b, 0, 0, 0))
    return pl.pallas_call(
        functools.partial(_nsa_kernel, n_top=n_top),
        grid=(bsz, seq // tq),
        in_specs=[
            pl.BlockSpec((None, tq, MIX), lambda b, c: (b, c, 0)),
            pl.BlockSpec((None, tq, 128), lambda b, c: (b, c, 5)),
        ] + [per_batch(t) for t in operands[2:8]] + [_const_spec(operands[8].shape)],
        out_specs=_row_spec(tq, MIX),
        out_shape=jax.ShapeDtypeStruct((bsz, seq, MIX), F32),
        scratch_shapes=[pltpu.VMEM((n_sel, tq), F32)],
        compiler_params=_params(("parallel", "arbitrary")),
    )(*operands)


HGRN_TB = 256
HGRN_C = 32


def _hgrn_kernel(u_ref, lbl_ref, ng_ref, ones_ref, o_ref, state_ref, *, layer):
    c = pl.program_id(1)

    @pl.when(c == 0)
    def _():
        state_ref[...] = jnp.zeros_like(state_ref)

    logits = lbl_ref[...]
    ex = jnp.exp(logits - jnp.max(logits, axis=0, keepdims=True))
    soft = ex / jnp.sum(ex, axis=0, keepdims=True)
    lb = jnp.sum(soft[0:layer + 1], axis=0, keepdims=True) - soft[0:1]

    u = u_ref[...]
    tb = u.shape[0]
    q = _silu(u[:, 0:MIX])
    f = lb + (1.0 - lb) * _sigmoid(u[:, MIX:2 * MIX])
    logf = jnp.log(jnp.maximum(f, GATE_FLOOR))
    k = 1.0 - f
    v = u[:, 2 * MIX:3 * MIX]
    og = u[:, 3 * MIX:4 * MIX]

    C = HGRN_C
    SUB = 8
    ones_bd = ones_ref[...]
    ti = lax.broadcasted_iota(jnp.int32, (tb, tb), 0)
    si = lax.broadcasted_iota(jnp.int32, (tb, tb), 1)
    same_chunk = (ti // C) == (si // C)
    b_all = _split_dot((same_chunk & (ti >= si)).astype(BF16), logf, 3)
    b_last_all = _split_dot(same_chunk.astype(BF16), logf, 3)
    b2_all = b_all * math.log2(math.e)
    qe_all = q * jnp.exp(b_all)
    kd_all = k * jnp.exp(b_last_all - b_all)
    g_last_all = jnp.exp(b_last_all)
    row8 = lax.broadcasted_iota(jnp.int32, (SUB, 1), 0)
    hr = lax.broadcasted_iota(jnp.int32, (MIX, MIX), 0) // HEAD_DIM
    hc_ = lax.broadcasted_iota(jnp.int32, (MIX, MIX), 1) // HEAD_DIM
    same_head = hr == hc_

    state = state_ref[...]
    o_chunks = []
    for ch in range(tb // C):
        sl = slice(ch * C, (ch + 1) * C)
        qc, kc, vc, b2 = q[sl], k[sl], v[sl], b2_all[sl]
        pieces = []
        for s in range(C):
            r0 = (s // SUB) * SUB
            pm = qc[r0:] * (kc[s:s + 1, :] * jnp.exp2(b2[r0:] - b2[s:s + 1, :]))
            top = jnp.where(row8 + r0 >= s, pm[0:SUB], 0.0)
            pieces.append(top if C - r0 == SUB else jnp.concatenate([top, pm[SUB:]], axis=0))
        attn = jnp.dot(jnp.concatenate(pieces, axis=0).astype(BF16), ones_bd,
                       preferred_element_type=F32)
        groups = [jnp.zeros((SUB, MIX), F32) for _ in range(C // SUB)]
        off = 0
        for s in range(C):
            g0 = s // SUB
            for g in range(g0, C // SUB):
                groups[g] = groups[g] + attn[off:off + SUB, :] * vc[s:s + 1, :]
                off += SUB
        o = jnp.concatenate(groups, axis=0)
        o_chunks.append(o + _bdot_nt(qe_all[sl], state))
        outer = jnp.dot(vc.T.astype(BF16), kd_all[sl].astype(BF16), preferred_element_type=F32)
        state = state * g_last_all[ch * C:ch * C + 1, :] + jnp.where(same_head, outer, 0.0)
    state_ref[...] = state
    o = jnp.concatenate(o_chunks, axis=0)
    ms = _head_sum_mxu(o * o, ones_bd) * (1.0 / HEAD_DIM)
    o_ref[...] = o * lax.rsqrt(ms + NORM_EPS) * ng_ref[...] * _silu(og)


def _block_ones():
    hid = jnp.arange(MIX) // HEAD_DIM
    return (hid[:, None] == hid[None, :]).astype(BF16)


def _hgrn_call(u, lb_logits, norm_g, layer):
    bsz, seq, _ = u.shape
    tb = min(HGRN_TB, seq)
    args = (u, lb_logits.astype(F32), norm_g.reshape(1, -1).astype(F32), _block_ones())
    return pl.pallas_call(
        functools.partial(_hgrn_kernel, layer=layer),
        grid=(bsz, seq // tb),
        in_specs=[_row_spec(tb, 4 * MIX)] + [_const_spec(t.shape) for t in args[1:]],
        out_specs=_row_spec(tb, MIX),
        out_shape=jax.ShapeDtypeStruct((bsz, seq, MIX), F32),
        scratch_shapes=[pltpu.VMEM((MIX, MIX), F32)],
        compiler_params=_params(("parallel", "arbitrary")),
    )(*args)


RET_C = 256


def _ret_kernel(u_ref, cos_ref, sin_ref, dm_ref, qd_ref, kd_ref, cd_ref, ng_ref, nb_ref,
                ones_ref, o_ref, state_ref):
    c = pl.program_id(1)

    @pl.when(c == 0)
    def _():
        state_ref[...] = jnp.zeros_like(state_ref)

    u = u_ref[...]
    cosf = cos_ref[...]
    sins = sin_ref[...]
    half = HEAD_DIM // 2

    def rope(a):
        outs = []
        for j in range(MIX // 128):
            blk = a[:, j * 128:(j + 1) * 128]
            lane = lax.broadcasted_iota(jnp.int32, blk.shape, 1)
            swapped = jnp.where((lane % HEAD_DIM) < half, pltpu.roll(blk, 128 - half, axis=1),
                                pltpu.roll(blk, half, axis=1))
            outs.append(swapped)
        return a * cosf + jnp.concatenate(outs, axis=-1) * sins

    q = rope(u[:, 0:MIX])
    k = rope(u[:, MIX:2 * MIX]) * (HEAD_DIM ** -0.5)
    v = u[:, 2 * MIX:3 * MIX]
    g = u[:, 3 * MIX:4 * MIX]
    qd = q * qd_ref[...]
    kd = k * kd_ref[...]
    C = u.shape[0]
    row_head = lax.broadcasted_iota(jnp.int32, (N_HEADS * C, MIX), 0) // C
    lane_head = lax.broadcasted_iota(jnp.int32, (N_HEADS * C, MIX), 1) // HEAD_DIM
    own = row_head == lane_head
    q_stack = jnp.where(own, jnp.concatenate([q] * N_HEADS, axis=0), 0.0)
    s = _bdot_nt(q_stack, k) * dm_ref[...]
    sv = jnp.where(own, _bdot(s, v), 0.0)
    o = sv[0:C]
    for h in range(1, N_HEADS):
        o = o + sv[h * C:(h + 1) * C]
    state = state_ref[...]
    o = o + _bdot(qd, state)
    kr = lax.broadcasted_iota(jnp.int32, (MIX, MIX), 0) // HEAD_DIM
    kc = lax.broadcasted_iota(jnp.int32, (MIX, MIX), 1) // HEAD_DIM
    state_ref[...] = state * cd_ref[...] + jnp.where(kr == kc, _bdot(kd.T, v), 0.0)
    ones = ones_ref[...]
    mean = _head_sum_mxu(o, ones) * (1.0 / HEAD_DIM)
    oc = o - mean
    var = _head_sum_mxu(oc * oc, ones) * (1.0 / HEAD_DIM)
    y = oc * lax.rsqrt(var + RET_GN_EPS) * ng_ref[...] + nb_ref[...]
    o_ref[...] = y * _silu(g)


def _ret_call(u, norm_g, norm_b):
    bsz, seq, _ = u.shape
    C = min(RET_C, seq)
    half = HEAD_DIM // 2
    pos = jnp.arange(seq, dtype=F32)
    inv_freq = RET_ROPE_BASE ** (-jnp.arange(0, HEAD_DIM, 2, dtype=F32) / HEAD_DIM)
    ang = pos[:, None] * inv_freq[None, :]
    cos, sin = jnp.cos(ang), jnp.sin(ang)
    cosf = jnp.tile(jnp.concatenate([cos, cos], axis=-1), (1, N_HEADS))
    sins = jnp.tile(jnp.concatenate([-sin, sin], axis=-1), (1, N_HEADS))
    log_gamma = jnp.log(1.0 - jnp.exp2(-5.0 - jnp.arange(N_HEADS, dtype=F32)))
    i = jnp.arange(C, dtype=F32)
    dpos = i[:, None] - i[None, :]
    dm = jnp.where(dpos >= 0, jnp.exp(jnp.maximum(dpos, 0.0)[None] * log_gamma[:, None, None]), 0.0)
    lanes = lambda t: jnp.repeat(t, HEAD_DIM, axis=-1)
    qd = lanes(jnp.exp((i + 1.0)[:, None] * log_gamma[None, :]))
    kd = lanes(jnp.exp((C - 1.0 - i)[:, None] * log_gamma[None, :]))
    cd = lanes(jnp.exp(C * log_gamma)[None, :])
    args = (u, cosf, sins, dm.reshape(N_HEADS * C, C), qd, kd, cd,
            norm_g.reshape(1, -1).astype(F32), norm_b.reshape(1, -1).astype(F32), _block_ones())
    in_specs = ([_row_spec(C, 4 * MIX),
                 pl.BlockSpec((C, MIX), lambda b, c: (c, 0)),
                 pl.BlockSpec((C, MIX), lambda b, c: (c, 0))]
                + [_const_spec(t.shape) for t in args[3:]])
    return pl.pallas_call(
        _ret_kernel,
        grid=(bsz, seq // C),
        in_specs=in_specs,
        out_specs=_row_spec(C, MIX),
        out_shape=jax.ShapeDtypeStruct((bsz, seq, MIX), F32),
        scratch_shapes=[pltpu.VMEM((MIX, MIX), F32)],
        compiler_params=_params(("parallel", "arbitrary")),
    )(*args)


DENSE_TM = 512
FFN_TF = 256
GROUP_COLS = (NSA_PAD, 4 * MIX, 4 * MIX, 4 * MIX)


def _rmsnorm(x, g):
    return x * lax.rsqrt(jnp.mean(x * x, axis=-1, keepdims=True) + NORM_EPS) * g


def _resident(shape):
    nd = len(shape)
    return pl.BlockSpec(shape, lambda b, c: (0,) * nd, pipeline_mode=pl.Buffered(1))


def _inproj_kernel(h_ref, g_ref, w_ref, nsa_ref, hg_ref, rt_ref, rw_ref):
    xn = _rmsnorm(h_ref[...], g_ref[...]).astype(BF16)
    off = 0
    for ref, width in zip((nsa_ref, hg_ref, rt_ref, rw_ref), GROUP_COLS):
        ref[...] = jnp.dot(xn, w_ref[:, off:off + width], preferred_element_type=F32)
        off += width


def _inproj_call(h, g, w_pad):
    bsz, seq, _ = h.shape
    tm = min(DENSE_TM, seq)
    return pl.pallas_call(
        _inproj_kernel,
        grid=(bsz, seq // tm),
        in_specs=[_row_spec(tm, D_MODEL), _resident((1, D_MODEL)), _resident(w_pad.shape)],
        out_specs=[_row_spec(tm, w) for w in GROUP_COLS],
        out_shape=[jax.ShapeDtypeStruct((bsz, seq, w), F32) for w in GROUP_COLS],
        compiler_params=_params(("parallel", "parallel")),
    )(h, g.reshape(1, -1), w_pad)


def _merge_kernel(h_ref, b0_ref, b1_ref, b2_ref, b3_ref, g_ref, wg_ref, bg_ref, wb_ref, wo_ref,
                  o_ref):
    h = h_ref[...]
    xn = _rmsnorm(h, g_ref[...]).astype(BF16)
    merged = None
    for m, b_ref in enumerate((b0_ref, b1_ref, b2_ref, b3_ref)):
        gate = _sigmoid(jnp.dot(xn, wg_ref[m], preferred_element_type=F32) + bg_ref[m])
        term = gate * _bdot(b_ref[...], wb_ref[m])
        merged = term if merged is None else merged + term
    o_ref[...] = h + _bdot(merged, wo_ref[...])


def _merge_call(h, branches, g, w_gate, b_gate, w_branch, w_out):
    bsz, seq, _ = h.shape
    tm = min(DENSE_TM, seq)
    consts = (g.reshape(1, -1), w_gate, b_gate.reshape(4, 1, D_MODEL), w_branch, w_out)
    return pl.pallas_call(
        _merge_kernel,
        grid=(bsz, seq // tm),
        in_specs=([_row_spec(tm, D_MODEL)] + [_row_spec(tm, MIX)] * 4
                  + [_resident(t.shape) for t in consts]),
        out_specs=_row_spec(tm, D_MODEL),
        out_shape=jax.ShapeDtypeStruct(h.shape, F32),
        compiler_params=_params(("parallel", "parallel")),
    )(h, *branches, *consts)


def _ffn_kernel(h_ref, g_ref, wg_ref, wu_ref, wd_ref, o_ref):
    h = h_ref[...]
    hn = _rmsnorm(h, g_ref[...]).astype(BF16)
    acc = h
    for f in range(0, D_FF, FFN_TF):
        gate = jnp.dot(hn, wg_ref[:, f:f + FFN_TF], preferred_element_type=F32)
        up = jnp.dot(hn, wu_ref[:, f:f + FFN_TF], preferred_element_type=F32)
        acc = acc + _bdot(_silu(gate) * up, wd_ref[f:f + FFN_TF, :])
    o_ref[...] = acc


def _ffn_call(h, g, w_gate, w_up, w_down):
    bsz, seq, _ = h.shape
    tm = min(DENSE_TM, seq)
    consts = (g.reshape(1, -1), w_gate, w_up, w_down)
    return pl.pallas_call(
        _ffn_kernel,
        grid=(bsz, seq // tm),
        in_specs=[_row_spec(tm, D_MODEL)] + [_resident(t.shape) for t in consts],
        out_specs=_row_spec(tm, D_MODEL),
        out_shape=jax.ShapeDtypeStruct(h.shape, F32),
        compiler_params=_params(("parallel", "parallel")),
    )(h, *consts)


def _ple_kernel(h_ref, p_ref, g_ref, wg_ref, wp_ref, gf_ref, o_ref, *, final_norm):
    h = h_ref[...]
    hp = _rmsnorm(h, g_ref[...])
    out = h + _sigmoid(_bdot(hp, wg_ref[...])) * _bdot(p_ref[...], wp_ref[...])
    if final_norm:
        out = _rmsnorm(out, gf_ref[...])
    o_ref[...] = out


def _ple_call(h, p, g, w_gate, w_proj, g_final, final_norm):
    bsz, seq, _ = h.shape
    tm = min(DENSE_TM, seq)
    consts = (g.reshape(1, -1), w_gate, w_proj, g_final.reshape(1, -1))
    return pl.pallas_call(
        functools.partial(_ple_kernel, final_norm=final_norm),
        grid=(bsz, seq // tm),
        in_specs=([_row_spec(tm, D_MODEL), _row_spec(tm, PLE_DIM)]
                  + [_resident(t.shape) for t in consts]),
        out_specs=_row_spec(tm, D_MODEL),
        out_shape=jax.ShapeDtypeStruct(h.shape, F32),
        compiler_params=_params(("parallel", "parallel")),
    )(h, p, *consts)


def kernel(x, p, norm_mix, w_in, nsa_pos_k, nsa_pos_v, nsa_cmp_k1, nsa_cmp_k2, nsa_cmp_v1,
           nsa_cmp_v2, hgrn_lb_logits, hgrn_norm, ret_norm_g, ret_norm_b, rwkv_mu, rwkv_w0,
           rwkv_w_up, rwkv_a0, rwkv_a_up, rwkv_g_up, rwkv_k_k, rwkv_k_a, rwkv_r_k, rwkv_norm_g,
           rwkv_norm_b, w_branch, w_gate, b_gate, w_out, norm_ffn, w_ffn_gate, w_ffn_up,
           w_ffn_down, norm_ple, w_ple_gate, w_ple_proj, norm_final):
    depth = w_in.shape[0]
    w_in_pad = jnp.concatenate(
        [w_in[:, :, :NSA_WIDTH], jnp.zeros((depth, D_MODEL, NSA_PAD - NSA_WIDTH), w_in.dtype),
         w_in[:, :, NSA_WIDTH:]], axis=-1).astype(BF16)
    bf = lambda t: t.astype(BF16)
    h = x
    for i in range(depth):
        u_nsa, u_hgrn, u_ret, u_rwkv = _inproj_call(h, norm_mix[i], w_in_pad[i])
        branches = (
            _nsa_call(u_nsa, nsa_pos_k[i], nsa_pos_v[i], nsa_cmp_k1[i], nsa_cmp_k2[i],
                      nsa_cmp_v1[i], nsa_cmp_v2[i]),
            _hgrn_call(u_hgrn, hgrn_lb_logits, hgrn_norm[i], i),
            _ret_call(u_ret, ret_norm_g[i], ret_norm_b[i]),
            _rwkv_call(u_rwkv, rwkv_mu[i], rwkv_w0[i], rwkv_w_up[i], rwkv_a0[i], rwkv_a_up[i],
                       rwkv_g_up[i], rwkv_k_k[i], rwkv_k_a[i], rwkv_r_k[i], rwkv_norm_g[i],
                       rwkv_norm_b[i]),
        )
        h = _merge_call(h, branches, norm_mix[i], bf(w_gate[i]), b_gate[i], bf(w_branch[i]),
                        bf(w_out[i]))
        h = _ffn_call(h, norm_ffn[i], bf(w_ffn_gate[i]), bf(w_ffn_up[i]), bf(w_ffn_down[i]))
        h = _ple_call(h, p[i], norm_ple[i], bf(w_ple_gate[i]), bf(w_ple_proj[i]), norm_final,
                      final_norm=(i == depth - 1))
    return h
```

```python
import functools
import math

import jax
import jax.numpy as jnp
from jax import lax
from jax.experimental import pallas as pl
from jax.experimental.pallas import tpu as pltpu

F32 = jnp.float32
BF16 = jnp.bfloat16

D_MODEL = 1024
N_HEADS = 4
HEAD_DIM = 64
MIX = N_HEADS * HEAD_DIM
D_FF = 2816
PLE_DIM = 256
NORM_EPS = 1e-6
NEG_BIG = -1e30
POS_BIG = 1e30
GATE_FLOOR = 1e-20

NSA_CMP_BLOCK = 32
NSA_CMP_STRIDE = 16
NSA_SEL_BLOCK = 64
NSA_TOP_N = 16
NSA_WINDOW = 512
NSA_CMP_HIDDEN = 128
NSA_WIDTH = 652
NSA_PAD = 768

RET_ROPE_BASE = 10000.0
RET_GN_EPS = 1e-5
RWKV_GN_EPS = 64e-5

VMEM_LIMIT = 56 * 1024 * 1024


def _bdot(a, b):
    return jnp.dot(a.astype(BF16), b.astype(BF16), preferred_element_type=F32)


def _bdot_nt(a, b):
    return lax.dot_general(a.astype(BF16), b.astype(BF16), (((1,), (1,)), ((), ())),
                           preferred_element_type=F32)


def _sigmoid(x):
    return 0.5 * jnp.tanh(0.5 * x) + 0.5


def _silu(x):
    return x * _sigmoid(x)


def _params(sem):
    return pltpu.CompilerParams(dimension_semantics=sem, vmem_limit_bytes=VMEM_LIMIT)


def _row_spec(tile, width):
    return pl.BlockSpec((None, tile, width), lambda b, c: (b, c, 0))


def _const_spec(shape):
    nd = len(shape)
    return pl.BlockSpec(shape, lambda b, c: (0,) * nd)


RWKV_TB = 256
RWKV_C = 64


def _split_dot(a_bf16, x, parts):
    total = None
    rest = x
    for _ in range(parts):
        piece = rest.astype(BF16)
        rest = rest - piece.astype(F32)
        term = jnp.dot(a_bf16, piece, preferred_element_type=F32)
        total = term if total is None else total + term
    return total


def _head_sum_mxu(x, ones_t):
    hi = x.astype(BF16)
    lo = (x - hi.astype(F32)).astype(BF16)
    return (jnp.dot(hi, ones_t, preferred_element_type=F32)
            + jnp.dot(lo, ones_t, preferred_element_type=F32))


def _rwkv_kernel(u_ref, mu_ref, w0_ref, wup_ref, a0_ref, aup_ref, gup_ref, kk_ref, ka_ref,
                 rk_ref, ng_ref, nb_ref, ones_ref, o_ref, state_ref, prev_ref):
    c = pl.program_id(1)
    ones = ones_ref[...]

    @pl.when(c == 0)
    def _():
        state_ref[...] = jnp.zeros_like(state_ref)
        prev_ref[...] = jnp.zeros_like(prev_ref)

    u = u_ref[...]
    tb = u.shape[0]
    row = lax.broadcasted_iota(jnp.int32, u.shape, 0)
    u_prev = jnp.where(row == 0, prev_ref[...], pltpu.roll(u, 1, axis=0))
    prev_ref[...] = u[tb - 1:tb, :]
    xs = u + mu_ref[...] * (u_prev - u)
    r = xs[:, 0:MIX]
    k = xs[:, MIX:2 * MIX]
    v = xs[:, 2 * MIX:3 * MIX]
    w_lo = xs[:, 3 * MIX:3 * MIX + 64]
    a_lo = xs[:, 3 * MIX + 64:3 * MIX + 128]
    g_lo = xs[:, 3 * MIX + 128:3 * MIX + 256]

    logw = -math.exp(-0.5) * _sigmoid(w0_ref[...] + _bdot(jnp.tanh(w_lo), wup_ref[...]))
    a = _sigmoid(a0_ref[...] + _bdot(a_lo, aup_ref[...]))
    g = _bdot(_sigmoid(g_lo), gup_ref[...])
    kk = k * kk_ref[...]
    kk = kk * lax.rsqrt(jnp.maximum(_head_sum_mxu(kk * kk, ones), 1e-24))
    k2 = k * (1.0 + (a - 1.0) * ka_ref[...])
    alpha = -kk
    beta = kk * a
    bonus = _head_sum_mxu(r * k2 * rk_ref[...], ones) * v

    C = RWKV_C
    ti = lax.broadcasted_iota(jnp.int32, (tb, tb), 0)
    si = lax.broadcasted_iota(jnp.int32, (tb, tb), 1)
    same_chunk = (ti // C) == (si // C)
    prefix = (same_chunk & (ti >= si)).astype(BF16)
    cum = _split_dot(prefix, logw, 3)
    cum_last = _split_dot(same_chunk.astype(BF16), logw, 3)
    gam_all = jnp.exp(cum_last)
    e_inv = jnp.exp(-cum)
    e_last = jnp.exp(cum_last - cum)
    ag_all = alpha * jnp.exp(cum - logw)
    rg_all = r * jnp.exp(cum)
    bi_all = beta * e_inv
    ki_all = k2 * e_inv
    bl_all = beta * e_last
    kl_all = k2 * e_last

    hc = N_HEADS * C
    row_head = lax.broadcasted_iota(jnp.int32, (hc, MIX), 0) // C
    lane_head = lax.broadcasted_iota(jnp.int32, (hc, MIX), 1) // HEAD_DIM
    own = row_head == lane_head

    def stack(x):
        return jnp.where(own, jnp.concatenate([x] * N_HEADS, axis=0), 0.0).astype(BF16)

    rr = lax.broadcasted_iota(jnp.int32, (hc, hc), 0)
    cc = lax.broadcasted_iota(jnp.int32, (hc, hc), 1)
    strict = rr > cc
    incl = rr >= cc
    eye_hc = (rr == cc).astype(F32)
    kr = lax.broadcasted_iota(jnp.int32, (MIX, MIX), 0)
    kc_ = lax.broadcasted_iota(jnp.int32, (MIX, MIX), 1)
    eye_k = kr == kc_

    state = state_ref[...]
    y_chunks = []
    for ch in range(tb // C):
        sl = slice(ch * C, (ch + 1) * C)
        ag, rg, bi, ki, bl, kl, vm = (stack(t[sl]) for t in (ag_all, rg_all, bi_all, ki_all,
                                                               bl_all, kl_all, v))
        aa = _bdot_nt(jnp.concatenate([ag, rg], axis=0), jnp.concatenate([bi, ki], axis=0))
        a_ab = jnp.where(strict, aa[:hc, :hc], 0.0)
        a_ak = jnp.where(strict, aa[:hc, hc:], 0.0)
        a_rb = jnp.where(incl, aa[hc:, :hc], 0.0)
        a_rk = jnp.where(incl, aa[hc:, hc:], 0.0)
        t_inv = eye_hc + a_ab
        pw = _bdot(a_ab, a_ab)
        for _ in range(int(math.log2(C)) - 2):
            both = _bdot(pw, jnp.concatenate([t_inv, pw], axis=-1))
            t_inv = t_inv + both[:, :hc]
            pw = both[:, hc:]
        t_inv = t_inv + _bdot(pw, t_inv)
        w12 = _bdot(t_inv, jnp.concatenate([ag.astype(F32), _bdot(a_ak, vm)], axis=-1))
        ry = _bdot(a_rb, w12)
        rq = rg.astype(F32) + ry[:, :MIX]
        y0 = ry[:, MIX:] + _bdot(a_rk, vm)
        mn = _bdot(bl.T, w12)
        m_mat = jnp.where(eye_k, gam_all[ch * C:ch * C + 1, :], 0.0) + mn[:, :MIX]
        n_mat = mn[:, MIX:] + _bdot(kl.T, vm)
        prod = _split_dot(jnp.concatenate([rq, m_mat], axis=0).astype(BF16), state, 2)
        ym = prod[:hc] + y0
        y_chunks.append(ym[0:C] + ym[C:2 * C] + ym[2 * C:3 * C] + ym[3 * C:4 * C])
        state = prod[hc:] + n_mat
    state_ref[...] = state
    y = jnp.concatenate(y_chunks, axis=0)

    mean = _head_sum_mxu(y, ones) * (1.0 / HEAD_DIM)
    yc = y - mean
    var = _head_sum_mxu(yc * yc, ones) * (1.0 / HEAD_DIM)
    yn = yc * lax.rsqrt(var + RWKV_GN_EPS) * ng_ref[...] + nb_ref[...]
    o_ref[...] = (yn + bonus) * g


def _rwkv_call(u, mu, w0, w_up, a0, a_up, g_up, k_k, k_a, r_k, norm_g, norm_b):
    bsz, seq, _ = u.shape
    tb = min(RWKV_TB, seq)
    vec = lambda t: t.reshape(1, -1).astype(F32)
    args = (u, vec(mu), vec(w0), w_up, vec(a0), a_up, g_up, vec(k_k), vec(k_a), vec(r_k),
            vec(norm_g), vec(norm_b), _block_ones())
    in_specs = [_row_spec(tb, 4 * MIX)] + [_const_spec(t.shape) for t in args[1:]]
    return pl.pallas_call(
        _rwkv_kernel,
        grid=(bsz, seq // tb),
        in_specs=in_specs,
        out_specs=_row_spec(tb, MIX),
        out_shape=jax.ShapeDtypeStruct((bsz, seq, MIX), F32),
        scratch_shapes=[pltpu.VMEM((MIX, MIX), F32),
                        pltpu.VMEM((1, 4 * MIX), F32)],
        compiler_params=_params(("parallel", "arbitrary")),
    )(*args)


NSA_TQ = 256
NSA_TK = 512
NSA_VROWS = 80
NSA_UNROLL = 2
NSA_ONEHOT = 16
GROUP = NSA_CMP_STRIDE


def _nsa_compress_kernel(xk_ref, xv_ref, pk_ref, pv_ref, k1_ref, k2_ref, v1_ref, v2_ref, o_ref):
    half = GROUP * HEAD_DIM

    def compress(x, pos, w1_ref, w2_ref):
        n = x.shape[0]
        first = _bdot(x + pos[:, :half], w1_ref[0:half, :])
        second = _bdot(x + pos[:, half:], w1_ref[half:2 * half, :])
        hid = first + pltpu.roll(second, n - 1, axis=0)
        return _bdot(_silu(hid), w2_ref[...])

    kc = compress(xk_ref[...], pk_ref[...], k1_ref, k2_ref)
    vc = compress(xv_ref[...], pv_ref[...], v1_ref, v2_ref)
    o_ref[...] = jnp.concatenate([kc, vc], axis=-1)


def _nsa_compress_call(xk, xv, pos_k, pos_v, k1, k2, v1, v2):
    bsz, ng, width = xk.shape
    args = (xk, xv, pos_k.reshape(1, -1), pos_v.reshape(1, -1), k1.astype(BF16), k2.astype(BF16),
            v1.astype(BF16), v2.astype(BF16))
    blk = pl.BlockSpec((None, ng, width), lambda b: (b, 0, 0))
    const = lambda t: pl.BlockSpec(t.shape, lambda b: (0,) * t.ndim)
    return pl.pallas_call(
        _nsa_compress_kernel,
        grid=(bsz,),
        in_specs=[blk, blk] + [const(t) for t in args[2:]],
        out_specs=pl.BlockSpec((None, ng, 2 * HEAD_DIM), lambda b: (b, 0, 0)),
        out_shape=jax.ShapeDtypeStruct((bsz, ng, 2 * HEAD_DIM), F32),
        compiler_params=_params(("parallel",)),
    )(*args)


REMOVED = -3e38


def _nsa_kernel(q_ref, g_ref, ks_ref, vs_ref, kw_ref, vw_ref, kc_ref, vc_ref, ov_ref, o_ref,
                selbias_ref, *, n_top):
    c = pl.program_id(1)
    tq = q_ref.shape[0]
    tk = ks_ref.shape[1]
    ct = kc_ref.shape[0]
    ns = ov_ref.shape[0]
    t0 = c * tq
    cols = N_HEADS * tq

    q_t = (q_ref[...] * (HEAD_DIM ** -0.5)).T
    qs = jnp.concatenate([q_t[h * HEAD_DIM:(h + 1) * HEAD_DIM] for h in range(N_HEADS)],
                         axis=1).astype(BF16)
    t_q = t0 + lax.broadcasted_iota(jnp.int32, (1, tq), 1)
    t_col = jnp.concatenate([t_q] * N_HEADS, axis=1)

    def online(carry, s, v_aug):
        m, acc = carry
        m_new = jnp.maximum(m, jnp.max(s, axis=0, keepdims=True))
        p = jnp.exp(s - m_new)
        acc = jnp.exp(m - m_new) * acc + jnp.dot(v_aug, p.astype(BF16), preferred_element_type=F32)
        return m_new, acc

    def normalise(acc):
        return acc[0:HEAD_DIM] / acc[HEAD_DIM:HEAD_DIM + 1]

    init_aug = (jnp.full((1, cols), NEG_BIG, F32), jnp.zeros((vs_ref.shape[1], cols), F32))

    n_row = lax.broadcasted_iota(jnp.int32, (ct, 1), 0)
    valid = (n_row * NSA_CMP_STRIDE + (NSA_CMP_BLOCK - 1)) <= t_col
    s = jnp.where(valid, jnp.dot(kc_ref[...], qs, preferred_element_type=F32), NEG_BIG)
    m_c = jnp.max(s, axis=0, keepdims=True)
    p = jnp.exp(s - m_c)
    p_hi = p.astype(BF16)
    p_lo = (p - p_hi.astype(F32)).astype(BF16)
    acc_c = jnp.dot(vc_ref[...], p_hi, preferred_element_type=F32)
    ov = ov_ref[...]
    imp4 = (jnp.dot(ov, p_hi, preferred_element_type=F32)
            + jnp.dot(ov, p_lo, preferred_element_type=F32))
    inv_c = jnp.where(m_c > 0.5 * NEG_BIG, 1.0 / acc_c[HEAD_DIM:HEAD_DIM + 1], 0.0)
    o_cmp = acc_c[0:HEAD_DIM] * inv_c
    imp4 = imp4 * inv_c
    imp = imp4[:, 0:tq]
    for h in range(1, N_HEADS):
        imp = imp + imp4[:, h * tq:(h + 1) * tq]

    blk = lax.broadcasted_iota(jnp.int32, (ns, 1), 0)
    blk_f = blk.astype(F32)
    cur = t_q // NSA_SEL_BLOCK
    forced = (blk == 0) | (blk == cur) | (blk == cur - 1)
    score = jnp.where(forced, POS_BIG, jnp.where(blk <= cur, imp, NEG_BIG))
    chosen = jnp.zeros((ns, tq), jnp.bool_)
    for _ in range(n_top):
        best = jnp.max(score, axis=0, keepdims=True)
        first = jnp.min(jnp.where(score == best, blk_f, float(ns)), axis=0, keepdims=True)
        hit = blk_f == first
        chosen = chosen | hit
        score = jnp.where(hit, REMOVED, score)
    selbias_ref[...] = jnp.where(chosen, 0.0, NEG_BIG)

    per_tile = tk // NSA_SEL_BLOCK
    key_row = lax.broadcasted_iota(jnp.int32, (tk, 1), 0)

    pad_rows = jnp.zeros((ks_ref.shape[2] - HEAD_DIM - NSA_ONEHOT, cols), BF16)
    bias_pad = jnp.zeros((NSA_ONEHOT - per_tile, tq), F32)

    def sel_scores(j, live=None):
        start = pl.multiple_of(j * per_tile, per_tile)
        bias = selbias_ref[pl.ds(start, per_tile), :]
        if live is not None:
            bias = jnp.where(live, bias, NEG_BIG)
        bias = jnp.concatenate([bias, bias_pad], axis=0)
        bias = jnp.concatenate([bias.astype(BF16)] * N_HEADS, axis=1)
        rhs = jnp.concatenate([bias, qs, pad_rows], axis=0)
        return jnp.dot(ks_ref[j], rhs, preferred_element_type=F32)

    j_last = t0 // tk

    def sel_group(i, carry):
        tiles = []
        for u in range(NSA_UNROLL):
            j = i * NSA_UNROLL + u
            tiles.append((jnp.minimum(j, j_last - 1), None if u == 0 else j < j_last))
        scores = [sel_scores(j, live) for j, live in tiles]
        for (j, _), s in zip(tiles, scores):
            carry = online(carry, s, vs_ref[j])
        return carry

    carry = lax.fori_loop(0, (j_last + NSA_UNROLL - 1) // NSA_UNROLL, sel_group, init_aug)
    causal = jnp.where((j_last * tk + key_row) <= t_q, 0.0, NEG_BIG)
    _, acc_s = online(carry, sel_scores(j_last) + jnp.concatenate([causal] * N_HEADS, axis=1),
                      vs_ref[j_last])
    o_sel = normalise(acc_s)

    tw = kw_ref.shape[1]
    n_wt = (NSA_WINDOW + tq) // tw
    jw = (t0 - NSA_WINDOW) // tw
    wkey_row = lax.broadcasted_iota(jnp.int32, (tw, 1), 0)
    s_parts = []
    w_tiles = []
    for i in range(n_wt):
        exists = (jw + i) >= 0
        w_tiles.append(jnp.maximum(jw + i, 0))
        dist = t_q - ((jw + i) * tw + wkey_row)
        s = jnp.dot(kw_ref[w_tiles[i]], qs, preferred_element_type=F32)
        if i == 0:
            bias = jnp.where((dist < NSA_WINDOW) & exists, 0.0, NEG_BIG)
            s = s + jnp.concatenate([bias] * N_HEADS, axis=1)
        elif i == n_wt - 1:
            bias = jnp.where(dist >= 0, 0.0, NEG_BIG)
            s = s + jnp.concatenate([bias] * N_HEADS, axis=1)
        else:
            s = s + jnp.where(exists, 0.0, NEG_BIG)
        s_parts.append(s)
    m_w = s_parts[0].max(axis=0, keepdims=True)
    for sp in s_parts[1:]:
        m_w = jnp.maximum(m_w, sp.max(axis=0, keepdims=True))
    acc_w = jnp.zeros((vw_ref.shape[1], cols), F32)
    for i, sp in enumerate(s_parts):
        p = jnp.exp(sp - m_w)
        acc_w = acc_w + jnp.dot(vw_ref[w_tiles[i]], p.astype(BF16), preferred_element_type=F32)
    o_win = normalise(acc_w)

    gates = _sigmoid(g_ref[...]).T

    def gate_row(branch):
        return jnp.concatenate([gates[branch * N_HEADS + h:branch * N_HEADS + h + 1, :]
                                for h in range(N_HEADS)], axis=1)

    out = gate_row(0) * o_cmp + gate_row(1) * o_sel + gate_row(2) * o_win
    o_ref[...] = jnp.concatenate([out[:, h * tq:(h + 1) * tq].T for h in range(N_HEADS)], axis=-1)


def _nsa_call(u_nsa, pos_k, pos_v, k1, k2, v1, v2):
    bsz, seq, _ = u_nsa.shape
    ng = seq // GROUP
    xk = u_nsa[:, :, MIX:MIX + HEAD_DIM].reshape(bsz, ng, GROUP * HEAD_DIM)
    xv = u_nsa[:, :, MIX + HEAD_DIM:MIX + 2 * HEAD_DIM].reshape(bsz, ng, GROUP * HEAD_DIM)
    kvc = _nsa_compress_call(xk, xv, pos_k, pos_v, k1, k2, v1, v2)

    n_sel = seq // NSA_SEL_BLOCK
    n_top = min(NSA_TOP_N, n_sel)
    cmp_start = jnp.arange(ng) * NSA_CMP_STRIDE
    sel_start = jnp.arange(n_sel) * NSA_SEL_BLOCK
    overlap = ((cmp_start[:, None] < sel_start[None, :] + NSA_SEL_BLOCK)
               & (cmp_start[:, None] + NSA_CMP_BLOCK > sel_start[None, :])).astype(BF16)
    tq = min(NSA_TQ, seq)
    tk = min(NSA_TK, seq)
    tw = tq

    def key_tiles(a, tile):
        return a.astype(BF16).reshape(bsz, a.shape[1] // tile, tile, HEAD_DIM)

    def value_tiles(a, tile):
        return jnp.swapaxes(key_tiles(a, tile), -1, -2)

    def value_tiles_aug(a, tile):
        vt = value_tiles(a, tile)
        extra = jnp.zeros(vt.shape[:2] + (NSA_VROWS - HEAD_DIM, tile), BF16).at[:, :, 0, :].set(1.0)
        return jnp.concatenate([vt, extra], axis=2)

    def key_tiles_onehot(a, tile):
        kt = key_tiles(a, tile)
        local = jnp.arange(tile) // NSA_SEL_BLOCK
        onehot = (local[:, None] == jnp.arange(NSA_ONEHOT)[None, :]).astype(BF16)
        zeros = jnp.zeros(kt.shape[:3] + (128 - HEAD_DIM - NSA_ONEHOT,), BF16)
        return jnp.concatenate([jnp.broadcast_to(onehot, kt.shape[:2] + onehot.shape), kt, zeros],
                               axis=-1)

    col = lambda j: u_nsa[:, :, MIX + j * HEAD_DIM:MIX + (j + 1) * HEAD_DIM]
    operands = (u_nsa, u_nsa,
                key_tiles_onehot(col(2), tk), value_tiles_aug(col(3), tk),
                key_tiles(col(4), tw), value_tiles_aug(col(5), tw),
                key_tiles(kvc[:, :, :HEAD_DIM], ng)[:, 0],
                value_tiles_aug(kvc[:, :, HEAD_DIM:], ng)[:, 0],
                overlap.T)
    per_batch = lambda t: pl.BlockSpec((None,) + t.shape[1:],
                                       lambda b, c: (b,) + (0,) * (t.ndim - 1))
    return pl.pallas_call(
        functools.partial(_nsa_kernel, n_top=n_top),
        grid=(bsz, seq // tq),
        in_specs=[
            pl.BlockSpec((None, tq, MIX), lambda b, c: (b, c, 0)),
            pl.BlockSpec((None, tq, 128), lambda b, c: (b, c, 5)),
        ] + [per_batch(t) for t in operands[2:8]] + [_const_spec(operands[8].shape)],
        out_specs=_row_spec(tq, MIX),
        out_shape=jax.ShapeDtypeStruct((bsz, seq, MIX), F32),
        scratch_shapes=[pltpu.VMEM((n_sel, tq), F32)],
        compiler_params=_params(("parallel", "arbitrary")),
    )(*operands)


HGRN_TB = 256
HGRN_C = 32


def _hgrn_kernel(u_ref, lbl_ref, ng_ref, ones_ref, o_ref, state_ref, *, layer):
    c = pl.program_id(1)

    @pl.when(c == 0)
    def _():
        state_ref[...] = jnp.zeros_like(state_ref)

    logits = lbl_ref[...]
    ex = jnp.exp(logits - jnp.max(logits, axis=0, keepdims=True))
    soft = ex / jnp.sum(ex, axis=0, keepdims=True)
    lb = jnp.sum(soft[0:layer + 1], axis=0, keepdims=True) - soft[0:1]

    u = u_ref[...]
    tb = u.shape[0]
    q = _silu(u[:, 0:MIX])
    f = lb + (1.0 - lb) / (1.0 + jnp.exp(-u[:, MIX:2 * MIX]))
    logf = jnp.log(jnp.maximum(f, GATE_FLOOR))
    k = 1.0 - f
    v = u[:, 2 * MIX:3 * MIX]
    og = u[:, 3 * MIX:4 * MIX]

    C = HGRN_C
    SUB = 8
    ones_bd = ones_ref[...]
    ti = lax.broadcasted_iota(jnp.int32, (tb, tb), 0)
    si = lax.broadcasted_iota(jnp.int32, (tb, tb), 1)
    same_chunk = (ti // C) == (si // C)
    b_all = _split_dot((same_chunk & (ti >= si)).astype(BF16), logf, 3)
    b_last_all = _split_dot(same_chunk.astype(BF16), logf, 3)
    b2_all = b_all * math.log2(math.e)
    qe_all = q * jnp.exp(b_all)
    kd_all = k * jnp.exp(b_last_all - b_all)
    g_last_all = jnp.exp(b_last_all)
    row8 = lax.broadcasted_iota(jnp.int32, (SUB, 1), 0)
    hr = lax.broadcasted_iota(jnp.int32, (MIX, MIX), 0) // HEAD_DIM
    hc_ = lax.broadcasted_iota(jnp.int32, (MIX, MIX), 1) // HEAD_DIM
    same_head = hr == hc_

    state = state_ref[...]
    o_chunks = []
    for ch in range(tb // C):
        sl = slice(ch * C, (ch + 1) * C)
        qc, kc, vc, b2 = q[sl], k[sl], v[sl], b2_all[sl]
        pieces = []
        for s in range(C):
            r0 = (s // SUB) * SUB
            pm = qc[r0:] * (kc[s:s + 1, :] * jnp.exp2(b2[r0:] - b2[s:s + 1, :]))
            top = jnp.where(row8 + r0 >= s, pm[0:SUB], 0.0)
            pieces.append(top if C - r0 == SUB else jnp.concatenate([top, pm[SUB:]], axis=0))
        attn = jnp.dot(jnp.concatenate(pieces, axis=0).astype(BF16), ones_bd,
                       preferred_element_type=F32)
        groups = [jnp.zeros((SUB, MIX), F32) for _ in range(C // SUB)]
        off = 0
        for s in range(C):
            g0 = s // SUB
            for g in range(g0, C // SUB):
                groups[g] = groups[g] + attn[off:off + SUB, :] * vc[s:s + 1, :]
                off += SUB
        o = jnp.concatenate(groups, axis=0)
        o_chunks.append(o + _bdot_nt(qe_all[sl], state))
        outer = jnp.dot(vc.T.astype(BF16), kd_all[sl].astype(BF16), preferred_element_type=F32)
        state = state * g_last_all[ch * C:ch * C + 1, :] + jnp.where(same_head, outer, 0.0)
    state_ref[...] = state
    o = jnp.concatenate(o_chunks, axis=0)
    ms = _head_sum_mxu(o * o, ones_bd) * (1.0 / HEAD_DIM)
    o_ref[...] = o * lax.rsqrt(ms + NORM_EPS) * ng_ref[...] * _silu(og)


def _block_ones():
    hid = jnp.arange(MIX) // HEAD_DIM
    return (hid[:, None] == hid[None, :]).astype(BF16)


def _hgrn_call(u, lb_logits, norm_g, layer):
    bsz, seq, _ = u.shape
    tb = min(HGRN_TB, seq)
    args = (u, lb_logits.astype(F32), norm_g.reshape(1, -1).astype(F32), _block_ones())
    return pl.pallas_call(
        functools.partial(_hgrn_kernel, layer=layer),
        grid=(bsz, seq // tb),
        in_specs=[_row_spec(tb, 4 * MIX)] + [_const_spec(t.shape) for t in args[1:]],
        out_specs=_row_spec(tb, MIX),
        out_shape=jax.ShapeDtypeStruct((bsz, seq, MIX), F32),
        scratch_shapes=[pltpu.VMEM((MIX, MIX), F32)],
        compiler_params=_params(("parallel", "arbitrary")),
    )(*args)


RET_C = 256


def _ret_kernel(u_ref, cos_ref, sin_ref, dm_ref, qd_ref, kd_ref, cd_ref, ng_ref, nb_ref,
                ones_ref, o_ref, state_ref):
    c = pl.program_id(1)

    @pl.when(c == 0)
    def _():
        state_ref[...] = jnp.zeros_like(state_ref)

    u = u_ref[...]
    cosf = cos_ref[...]
    sins = sin_ref[...]
    half = HEAD_DIM // 2

    def rope(a):
        outs = []
        for j in range(MIX // 128):
            blk = a[:, j * 128:(j + 1) * 128]
            lane = lax.broadcasted_iota(jnp.int32, blk.shape, 1)
            swapped = jnp.where((lane % HEAD_DIM) < half, pltpu.roll(blk, 128 - half, axis=1),
                                pltpu.roll(blk, half, axis=1))
            outs.append(swapped)
        return a * cosf + jnp.concatenate(outs, axis=-1) * sins

    q = rope(u[:, 0:MIX])
    k = rope(u[:, MIX:2 * MIX]) * (HEAD_DIM ** -0.5)
    v = u[:, 2 * MIX:3 * MIX]
    g = u[:, 3 * MIX:4 * MIX]
    qd = q * qd_ref[...]
    kd = k * kd_ref[...]
    C = u.shape[0]
    row_head = lax.broadcasted_iota(jnp.int32, (N_HEADS * C, MIX), 0) // C
    lane_head = lax.broadcasted_iota(jnp.int32, (N_HEADS * C, MIX), 1) // HEAD_DIM
    own = row_head == lane_head
    q_stack = jnp.where(own, jnp.concatenate([q] * N_HEADS, axis=0), 0.0)
    s = _bdot_nt(q_stack, k) * dm_ref[...]
    sv = jnp.where(own, _bdot(s, v), 0.0)
    o = sv[0:C]
    for h in range(1, N_HEADS):
        o = o + sv[h * C:(h + 1) * C]
    state = state_ref[...]
    o = o + _bdot(qd, state)
    kr = lax.broadcasted_iota(jnp.int32, (MIX, MIX), 0) // HEAD_DIM
    kc = lax.broadcasted_iota(jnp.int32, (MIX, MIX), 1) // HEAD_DIM
    state_ref[...] = state * cd_ref[...] + jnp.where(kr == kc, _bdot(kd.T, v), 0.0)
    ones = ones_ref[...]
    mean = _head_sum_mxu(o, ones) * (1.0 / HEAD_DIM)
    oc = o - mean
    var = _head_sum_mxu(oc * oc, ones) * (1.0 / HEAD_DIM)
    y = oc * lax.rsqrt(var + RET_GN_EPS) * ng_ref[...] + nb_ref[...]
    o_ref[...] = y * _silu(g)


def _ret_call(u, norm_g, norm_b):
    bsz, seq, _ = u.shape
    C = min(RET_C, seq)
    half = HEAD_DIM // 2
    pos = jnp.arange(seq, dtype=F32)
    inv_freq = RET_ROPE_BASE ** (-jnp.arange(0, HEAD_DIM, 2, dtype=F32) / HEAD_DIM)
    ang = pos[:, None] * inv_freq[None, :]
    cos, sin = jnp.cos(ang), jnp.sin(ang)
    cosf = jnp.tile(jnp.concatenate([cos, cos], axis=-1), (1, N_HEADS))
    sins = jnp.tile(jnp.concatenate([-sin, sin], axis=-1), (1, N_HEADS))
    log_gamma = jnp.log(1.0 - jnp.exp2(-5.0 - jnp.arange(N_HEADS, dtype=F32)))
    i = jnp.arange(C, dtype=F32)
    dpos = i[:, None] - i[None, :]
    dm = jnp.where(dpos >= 0, jnp.exp(jnp.maximum(dpos, 0.0)[None] * log_gamma[:, None, None]), 0.0)
    lanes = lambda t: jnp.repeat(t, HEAD_DIM, axis=-1)
    qd = lanes(jnp.exp((i + 1.0)[:, None] * log_gamma[None, :]))
    kd = lanes(jnp.exp((C - 1.0 - i)[:, None] * log_gamma[None, :]))
    cd = lanes(jnp.exp(C * log_gamma)[None, :])
    args = (u, cosf, sins, dm.reshape(N_HEADS * C, C), qd, kd, cd,
            norm_g.reshape(1, -1).astype(F32), norm_b.reshape(1, -1).astype(F32), _block_ones())
    in_specs = ([_row_spec(C, 4 * MIX),
                 pl.BlockSpec((C, MIX), lambda b, c: (c, 0)),
                 pl.BlockSpec((C, MIX), lambda b, c: (c, 0))]
                + [_const_spec(t.shape) for t in args[3:]])
    return pl.pallas_call(
        _ret_kernel,
        grid=(bsz, seq // C),
        in_specs=in_specs,
        out_specs=_row_spec(C, MIX),
        out_shape=jax.ShapeDtypeStruct((bsz, seq, MIX), F32),
        scratch_shapes=[pltpu.VMEM((MIX, MIX), F32)],
        compiler_params=_params(("parallel", "arbitrary")),
    )(*args)


DENSE_TM = 512
FFN_TF = 256
GROUP_COLS = (NSA_PAD, 4 * MIX, 4 * MIX, 4 * MIX)


def _rmsnorm(x, g):
    return x * lax.rsqrt(jnp.mean(x * x, axis=-1, keepdims=True) + NORM_EPS) * g


def _resident(shape):
    nd = len(shape)
    return pl.BlockSpec(shape, lambda b, c: (0,) * nd, pipeline_mode=pl.Buffered(1))


def _inproj_kernel(h_ref, g_ref, w_ref, nsa_ref, hg_ref, rt_ref, rw_ref):
    xn = _rmsnorm(h_ref[...], g_ref[...]).astype(BF16)
    off = 0
    for ref, width in zip((nsa_ref, hg_ref, rt_ref, rw_ref), GROUP_COLS):
        ref[...] = jnp.dot(xn, w_ref[:, off:off + width], preferred_element_type=F32)
        off += width


def _inproj_call(h, g, w_pad):
    bsz, seq, _ = h.shape
    tm = min(DENSE_TM, seq)
    return pl.pallas_call(
        _inproj_kernel,
        grid=(bsz, seq // tm),
        in_specs=[_row_spec(tm, D_MODEL), _resident((1, D_MODEL)), _resident(w_pad.shape)],
        out_specs=[_row_spec(tm, w) for w in GROUP_COLS],
        out_shape=[jax.ShapeDtypeStruct((bsz, seq, w), F32) for w in GROUP_COLS],
        compiler_params=_params(("parallel", "parallel")),
    )(h, g.reshape(1, -1), w_pad)


def _merge_kernel(h_ref, b0_ref, b1_ref, b2_ref, b3_ref, g_ref, wg_ref, bg_ref, wb_ref, wo_ref,
                  o_ref):
    h = h_ref[...]
    xn = _rmsnorm(h, g_ref[...]).astype(BF16)
    merged = None
    for m, b_ref in enumerate((b0_ref, b1_ref, b2_ref, b3_ref)):
        gate = _sigmoid(jnp.dot(xn, wg_ref[m], preferred_element_type=F32) + bg_ref[m])
        term = gate * _bdot(b_ref[...], wb_ref[m])
        merged = term if merged is None else merged + term
    o_ref[...] = h + _bdot(merged, wo_ref[...])


def _merge_call(h, branches, g, w_gate, b_gate, w_branch, w_out):
    bsz, seq, _ = h.shape
    tm = min(DENSE_TM, seq)
    consts = (g.reshape(1, -1), w_gate, b_gate.reshape(4, 1, D_MODEL), w_branch, w_out)
    return pl.pallas_call(
        _merge_kernel,
        grid=(bsz, seq // tm),
        in_specs=([_row_spec(tm, D_MODEL)] + [_row_spec(tm, MIX)] * 4
                  + [_resident(t.shape) for t in consts]),
        out_specs=_row_spec(tm, D_MODEL),
        out_shape=jax.ShapeDtypeStruct(h.shape, F32),
        compiler_params=_params(("parallel", "parallel")),
    )(h, *branches, *consts)


def _ffn_kernel(h_ref, g_ref, wg_ref, wu_ref, wd_ref, o_ref):
    h = h_ref[...]
    hn = _rmsnorm(h, g_ref[...]).astype(BF16)
    acc = h
    for f in range(0, D_FF, FFN_TF):
        gate = jnp.dot(hn, wg_ref[:, f:f + FFN_TF], preferred_element_type=F32)
        up = jnp.dot(hn, wu_ref[:, f:f + FFN_TF], preferred_element_type=F32)
        acc = acc + _bdot(_silu(gate) * up, wd_ref[f:f + FFN_TF, :])
    o_ref[...] = acc


def _ffn_call(h, g, w_gate, w_up, w_down):
    bsz, seq, _ = h.shape
    tm = min(DENSE_TM, seq)
    consts = (g.reshape(1, -1), w_gate, w_up, w_down)
    return pl.pallas_call(
        _ffn_kernel,
        grid=(bsz, seq // tm),
        in_specs=[_row_spec(tm, D_MODEL)] + [_resident(t.shape) for t in consts],
        out_specs=_row_spec(tm, D_MODEL),
        out_shape=jax.ShapeDtypeStruct(h.shape, F32),
        compiler_params=_params(("parallel", "parallel")),
    )(h, *consts)


def _ple_kernel(h_ref, p_ref, g_ref, wg_ref, wp_ref, gf_ref, o_ref, *, final_norm):
    h = h_ref[...]
    hp = _rmsnorm(h, g_ref[...])
    out = h + _sigmoid(_bdot(hp, wg_ref[...])) * _bdot(p_ref[...], wp_ref[...])
    if final_norm:
        out = _rmsnorm(out, gf_ref[...])
    o_ref[...] = out


def _ple_call(h, p, g, w_gate, w_proj, g_final, final_norm):
    bsz, seq, _ = h.shape
    tm = min(DENSE_TM, seq)
    consts = (g.reshape(1, -1), w_gate, w_proj, g_final.reshape(1, -1))
    return pl.pallas_call(
        functools.partial(_ple_kernel, final_norm=final_norm),
        grid=(bsz, seq // tm),
        in_specs=([_row_spec(tm, D_MODEL), _row_spec(tm, PLE_DIM)]
                  + [_resident(t.shape) for t in consts]),
        out_specs=_row_spec(tm, D_MODEL),
        out_shape=jax.ShapeDtypeStruct(h.shape, F32),
        compiler_params=_params(("parallel", "parallel")),
    )(h, p, *consts)


def kernel(x, p, norm_mix, w_in, nsa_pos_k, nsa_pos_v, nsa_cmp_k1, nsa_cmp_k2, nsa_cmp_v1,
           nsa_cmp_v2, hgrn_lb_logits, hgrn_norm, ret_norm_g, ret_norm_b, rwkv_mu, rwkv_w0,
           rwkv_w_up, rwkv_a0, rwkv_a_up, rwkv_g_up, rwkv_k_k, rwkv_k_a, rwkv_r_k, rwkv_norm_g,
           rwkv_norm_b, w_branch, w_gate, b_gate, w_out, norm_ffn, w_ffn_gate, w_ffn_up,
           w_ffn_down, norm_ple, w_ple_gate, w_ple_proj, norm_final):
    depth = w_in.shape[0]
    w_in_pad = jnp.concatenate(
        [w_in[:, :, :NSA_WIDTH], jnp.zeros((depth, D_MODEL, NSA_PAD - NSA_WIDTH), w_in.dtype),
         w_in[:, :, NSA_WIDTH:]], axis=-1).astype(BF16)
    bf = lambda t: t.astype(BF16)
    h = x
    for i in range(depth):
        u_nsa, u_hgrn, u_ret, u_rwkv = _inproj_call(h, norm_mix[i], w_in_pad[i])
        branches = (
            _nsa_call(u_nsa, nsa_pos_k[i], nsa_pos_v[i], nsa_cmp_k1[i], nsa_cmp_k2[i],
                      nsa_cmp_v1[i], nsa_cmp_v2[i]),
            _hgrn_call(u_hgrn, hgrn_lb_logits, hgrn_norm[i], i),
            _ret_call(u_ret, ret_norm_g[i], ret_norm_b[i]),
            _rwkv_call(u_rwkv, rwkv_mu[i], rwkv_w0[i], rwkv_w_up[i], rwkv_a0[i], rwkv_a_up[i],
                       rwkv_g_up[i], rwkv_k_k[i], rwkv_k_a[i], rwkv_r_k[i], rwkv_norm_g[i],
                       rwkv_norm_b[i]),
        )
        h = _merge_call(h, branches, norm_mix[i], bf(w_gate[i]), b_gate[i], bf(w_branch[i]),
                        bf(w_out[i]))
        h = _ffn_call(h, norm_ffn[i], bf(w_ffn_gate[i]), bf(w_ffn_up[i]), bf(w_ffn_down[i]))
        h = _ple_call(h, p[i], norm_ple[i], bf(w_ple_gate[i]), bf(w_ple_proj[i]), norm_final,
                      final_norm=(i == depth - 1))
    return h
```

```python
import functools
import math

import jax
import jax.numpy as jnp
from jax import lax
from jax.experimental import pallas as pl
from jax.experimental.pallas import tpu as pltpu

F32 = jnp.float32
BF16 = jnp.bfloat16

D_MODEL = 1024
N_HEADS = 4
HEAD_DIM = 64
MIX = N_HEADS * HEAD_DIM
D_FF = 2816
PLE_DIM = 256
NORM_EPS = 1e-6
NEG_BIG = -1e30
POS_BIG = 1e30
GATE_FLOOR = 1e-20

NSA_CMP_BLOCK = 32
NSA_CMP_STRIDE = 16
NSA_SEL_BLOCK = 64
NSA_TOP_N = 16
NSA_WINDOW = 512
NSA_CMP_HIDDEN = 128
NSA_WIDTH = 652
NSA_PAD = 768

RET_ROPE_BASE = 10000.0
RET_GN_EPS = 1e-5
RWKV_GN_EPS = 64e-5

VMEM_LIMIT = 56 * 1024 * 1024


def _bdot(a, b):
    return jnp.dot(a.astype(BF16), b.astype(BF16), preferred_element_type=F32)


def _bdot_nt(a, b):
    return lax.dot_general(a.astype(BF16), b.astype(BF16), (((1,), (1,)), ((), ())),
                           preferred_element_type=F32)


def _sigmoid(x):
    return 0.5 * jnp.tanh(0.5 * x) + 0.5


def _silu(x):
    return x * _sigmoid(x)


def _params(sem):
    return pltpu.CompilerParams(dimension_semantics=sem, vmem_limit_bytes=VMEM_LIMIT)


def _row_spec(tile, width):
    return pl.BlockSpec((None, tile, width), lambda b, c: (b, c, 0))


def _const_spec(shape):
    nd = len(shape)
    return pl.BlockSpec(shape, lambda b, c: (0,) * nd)


RWKV_TB = 256
RWKV_C = 64


def _split_dot(a_bf16, x, parts):
    total = None
    rest = x
    for _ in range(parts):
        piece = rest.astype(BF16)
        rest = rest - piece.astype(F32)
        term = jnp.dot(a_bf16, piece, preferred_element_type=F32)
        total = term if total is None else total + term
    return total


def _head_sum_mxu(x, ones_t):
    hi = x.astype(BF16)
    lo = (x - hi.astype(F32)).astype(BF16)
    return (jnp.dot(hi, ones_t, preferred_element_type=F32)
            + jnp.dot(lo, ones_t, preferred_element_type=F32))


def _drain(steps):
    for _ in steps:
        pass


def _rwkv_steps(u_ref, mu_ref, w0_ref, wup_ref, a0_ref, aup_ref, gup_ref, kk_ref, ka_ref,
                rk_ref, ng_ref, nb_ref, ones_ref, o_ref, state_ref, prev_ref):
    c = pl.program_id(1)
    ones = ones_ref[...]

    @pl.when(c == 0)
    def _():
        state_ref[...] = jnp.zeros_like(state_ref)
        prev_ref[...] = jnp.zeros_like(prev_ref)

    u = u_ref[...]
    tb = u.shape[0]
    row = lax.broadcasted_iota(jnp.int32, u.shape, 0)
    u_prev = jnp.where(row == 0, prev_ref[...], pltpu.roll(u, 1, axis=0))
    prev_ref[...] = u[tb - 1:tb, :]
    xs = u + mu_ref[...] * (u_prev - u)
    r = xs[:, 0:MIX]
    k = xs[:, MIX:2 * MIX]
    v = xs[:, 2 * MIX:3 * MIX]
    w_lo = xs[:, 3 * MIX:3 * MIX + 64]
    a_lo = xs[:, 3 * MIX + 64:3 * MIX + 128]
    g_lo = xs[:, 3 * MIX + 128:3 * MIX + 256]

    logw = -math.exp(-0.5) * _sigmoid(w0_ref[...] + _bdot(jnp.tanh(w_lo), wup_ref[...]))
    a = _sigmoid(a0_ref[...] + _bdot(a_lo, aup_ref[...]))
    g = _bdot(_sigmoid(g_lo), gup_ref[...])
    kk = k * kk_ref[...]
    kk = kk * lax.rsqrt(jnp.maximum(_head_sum_mxu(kk * kk, ones), 1e-24))
    k2 = k * (1.0 + (a - 1.0) * ka_ref[...])
    alpha = -kk
    beta = kk * a
    bonus = _head_sum_mxu(r * k2 * rk_ref[...], ones) * v

    C = RWKV_C
    ti = lax.broadcasted_iota(jnp.int32, (tb, tb), 0)
    si = lax.broadcasted_iota(jnp.int32, (tb, tb), 1)
    same_chunk = (ti // C) == (si // C)
    prefix = (same_chunk & (ti >= si)).astype(BF16)
    cum = _split_dot(prefix, logw, 3)
    cum_last = _split_dot(same_chunk.astype(BF16), logw, 3)
    gam_all = jnp.exp(cum_last)
    e_inv = jnp.exp(-cum)
    e_last = jnp.exp(cum_last - cum)
    ag_all = alpha * jnp.exp(cum - logw)
    rg_all = r * jnp.exp(cum)
    bi_all = beta * e_inv
    ki_all = k2 * e_inv
    bl_all = beta * e_last
    kl_all = k2 * e_last

    hc = N_HEADS * C
    row_head = lax.broadcasted_iota(jnp.int32, (hc, MIX), 0) // C
    lane_head = lax.broadcasted_iota(jnp.int32, (hc, MIX), 1) // HEAD_DIM
    own = row_head == lane_head

    def stack(x):
        return jnp.where(own, jnp.concatenate([x] * N_HEADS, axis=0), 0.0).astype(BF16)

    rr = lax.broadcasted_iota(jnp.int32, (hc, hc), 0)
    cc = lax.broadcasted_iota(jnp.int32, (hc, hc), 1)
    strict = rr > cc
    incl = rr >= cc
    eye_hc = (rr == cc).astype(F32)
    kr = lax.broadcasted_iota(jnp.int32, (MIX, MIX), 0)
    kc_ = lax.broadcasted_iota(jnp.int32, (MIX, MIX), 1)
    eye_k = kr == kc_

    state = state_ref[...]
    y_chunks = []
    yield
    for ch in range(tb // C):
        sl = slice(ch * C, (ch + 1) * C)
        ag, rg, bi, ki, bl, kl, vm = (stack(t[sl]) for t in (ag_all, rg_all, bi_all, ki_all,
                                                               bl_all, kl_all, v))
        aa = _bdot_nt(jnp.concatenate([ag, rg], axis=0), jnp.concatenate([bi, ki], axis=0))
        a_ab = jnp.where(strict, aa[:hc, :hc], 0.0)
        a_ak = jnp.where(strict, aa[:hc, hc:], 0.0)
        a_rb = jnp.where(incl, aa[hc:, :hc], 0.0)
        a_rk = jnp.where(incl, aa[hc:, hc:], 0.0)
        t_inv = eye_hc + a_ab
        pw = _bdot(a_ab, a_ab)
        for _ in range(int(math.log2(C)) - 2):
            both = _bdot(pw, jnp.concatenate([t_inv, pw], axis=-1))
            t_inv = t_inv + both[:, :hc]
            pw = both[:, hc:]
        t_inv = t_inv + _bdot(pw, t_inv)
        w12 = _bdot(t_inv, jnp.concatenate([ag.astype(F32), _bdot(a_ak, vm)], axis=-1))
        ry = _bdot(a_rb, w12)
        rq = rg.astype(F32) + ry[:, :MIX]
        y0 = ry[:, MIX:] + _bdot(a_rk, vm)
        mn = _bdot(bl.T, w12)
        m_mat = jnp.where(eye_k, gam_all[ch * C:ch * C + 1, :], 0.0) + mn[:, :MIX]
        n_mat = mn[:, MIX:] + _bdot(kl.T, vm)
        prod = _split_dot(jnp.concatenate([rq, m_mat], axis=0).astype(BF16), state, 2)
        ym = prod[:hc] + y0
        y_chunks.append(ym[0:C] + ym[C:2 * C] + ym[2 * C:3 * C] + ym[3 * C:4 * C])
        state = prod[hc:] + n_mat
        yield
    state_ref[...] = state
    y = jnp.concatenate(y_chunks, axis=0)

    mean = _head_sum_mxu(y, ones) * (1.0 / HEAD_DIM)
    yc = y - mean
    var = _head_sum_mxu(yc * yc, ones) * (1.0 / HEAD_DIM)
    yn = yc * lax.rsqrt(var + RWKV_GN_EPS) * ng_ref[...] + nb_ref[...]
    o_ref[...] = (yn + bonus) * g


def _rwkv_kernel(*refs):
    _drain(_rwkv_steps(*refs))


def _rwkv_operands(u, mu, w0, w_up, a0, a_up, g_up, k_k, k_a, r_k, norm_g, norm_b):
    vec = lambda t: t.reshape(1, -1).astype(F32)
    return (u, vec(mu), vec(w0), w_up, vec(a0), a_up, g_up, vec(k_k), vec(k_a), vec(r_k),
            vec(norm_g), vec(norm_b), _block_ones())


RWKV_SCRATCH = (pltpu.VMEM((MIX, MIX), F32), pltpu.VMEM((1, 4 * MIX), F32))


def _rwkv_call(*params):
    args = _rwkv_operands(*params)
    bsz, seq, _ = args[0].shape
    tb = min(RWKV_TB, seq)
    in_specs = [_row_spec(tb, 4 * MIX)] + [_const_spec(t.shape) for t in args[1:]]
    return pl.pallas_call(
        _rwkv_kernel,
        grid=(bsz, seq // tb),
        in_specs=in_specs,
        out_specs=_row_spec(tb, MIX),
        out_shape=jax.ShapeDtypeStruct((bsz, seq, MIX), F32),
        scratch_shapes=list(RWKV_SCRATCH),
        compiler_params=_params(("parallel", "arbitrary")),
    )(*args)


NSA_TQ = 256
NSA_TK = 512
NSA_VROWS = 80
NSA_UNROLL = 2
NSA_CMP_PARTS = 4
NSA_ONEHOT = 16
GROUP = NSA_CMP_STRIDE


def _nsa_compress_kernel(xk_ref, xv_ref, pk_ref, pv_ref, k1_ref, k2_ref, v1_ref, v2_ref, o_ref):
    half = GROUP * HEAD_DIM

    def compress(x, pos, w1_ref, w2_ref):
        n = x.shape[0]
        first = _bdot(x + pos[:, :half], w1_ref[0:half, :])
        second = _bdot(x + pos[:, half:], w1_ref[half:2 * half, :])
        hid = first + pltpu.roll(second, n - 1, axis=0)
        return _bdot(_silu(hid), w2_ref[...])

    kc = compress(xk_ref[...], pk_ref[...], k1_ref, k2_ref)
    vc = compress(xv_ref[...], pv_ref[...], v1_ref, v2_ref)
    o_ref[...] = jnp.concatenate([kc, vc], axis=-1)


def _nsa_compress_call(xk, xv, pos_k, pos_v, k1, k2, v1, v2):
    bsz, ng, width = xk.shape
    args = (xk, xv, pos_k.reshape(1, -1), pos_v.reshape(1, -1), k1.astype(BF16), k2.astype(BF16),
            v1.astype(BF16), v2.astype(BF16))
    blk = pl.BlockSpec((None, ng, width), lambda b: (b, 0, 0))
    const = lambda t: pl.BlockSpec(t.shape, lambda b: (0,) * t.ndim)
    return pl.pallas_call(
        _nsa_compress_kernel,
        grid=(bsz,),
        in_specs=[blk, blk] + [const(t) for t in args[2:]],
        out_specs=pl.BlockSpec((None, ng, 2 * HEAD_DIM), lambda b: (b, 0, 0)),
        out_shape=jax.ShapeDtypeStruct((bsz, ng, 2 * HEAD_DIM), F32),
        compiler_params=_params(("parallel",)),
    )(*args)


REMOVED = -3e38


def _nsa_kernel(q_ref, g_ref, ks_ref, vs_ref, kw_ref, vw_ref, kc_ref, vc_ref, o_ref,
                selbias_ref, *, n_top):
    c = pl.program_id(1)
    tq = q_ref.shape[0]
    tk = ks_ref.shape[1]
    ct = kc_ref.shape[0]
    ns = selbias_ref.shape[0]
    t0 = c * tq
    cols = N_HEADS * tq

    q_t = (q_ref[...] * (HEAD_DIM ** -0.5)).T
    qs = jnp.concatenate([q_t[h * HEAD_DIM:(h + 1) * HEAD_DIM] for h in range(N_HEADS)],
                         axis=1).astype(BF16)
    t_q = t0 + lax.broadcasted_iota(jnp.int32, (1, tq), 1)
    t_col = jnp.concatenate([t_q] * N_HEADS, axis=1)

    def online(carry, s, v_aug):
        m, acc = carry
        m_new = jnp.maximum(m, jnp.max(s, axis=0, keepdims=True))
        p = jnp.exp(s - m_new)
        acc = jnp.exp(m - m_new) * acc + jnp.dot(v_aug, p.astype(BF16), preferred_element_type=F32)
        return m_new, acc

    def normalise(acc):
        return acc[0:HEAD_DIM] / acc[HEAD_DIM:HEAD_DIM + 1]

    init_aug = (jnp.full((1, cols), NEG_BIG, F32), jnp.zeros((vs_ref.shape[1], cols), F32))

    vrows = vs_ref.shape[1]

    def cmp_quarters(quarters):
        rows = quarters * (ct // NSA_CMP_PARTS)
        n_row = lax.broadcasted_iota(jnp.int32, (rows, 1), 0)
        valid = (n_row * NSA_CMP_STRIDE + (NSA_CMP_BLOCK - 1)) <= t_col
        s = jnp.where(valid, jnp.dot(kc_ref[0:rows, :], qs, preferred_element_type=F32), NEG_BIG)
        m = jnp.max(s, axis=0, keepdims=True)
        p = jnp.exp(s - m)
        p_hi = p.astype(BF16)
        p_lo = (p - p_hi.astype(F32)).astype(BF16)
        res = jnp.dot(vc_ref[:, 0:rows], p_hi, preferred_element_type=F32)
        imp_lo = jnp.dot(vc_ref[vrows:, 0:rows], p_lo, preferred_element_type=F32)
        return m, jnp.concatenate([res[:vrows], res[vrows:] + imp_lo], axis=0)

    last_valid = (t0 + tq - NSA_CMP_BLOCK) // NSA_CMP_STRIDE
    quarter = jnp.clip(last_valid // (ct // NSA_CMP_PARTS), 0, NSA_CMP_PARTS - 1)
    m_c, res_c = lax.switch(quarter, [functools.partial(cmp_quarters, i + 1)
                                      for i in range(NSA_CMP_PARTS)])
    inv_c = jnp.where(m_c > 0.5 * NEG_BIG, 1.0 / res_c[HEAD_DIM:HEAD_DIM + 1], 0.0)
    o_cmp = res_c[0:HEAD_DIM] * inv_c
    imp4 = res_c[vrows:] * inv_c
    imp = imp4[:, 0:tq]
    for h in range(1, N_HEADS):
        imp = imp + imp4[:, h * tq:(h + 1) * tq]

    blk = lax.broadcasted_iota(jnp.int32, (ns, 1), 0)
    blk_f = blk.astype(F32)
    cur = t_q // NSA_SEL_BLOCK
    forced = (blk == 0) | (blk == cur) | (blk == cur - 1)
    score = jnp.where(forced, POS_BIG, jnp.where(blk <= cur, imp, NEG_BIG))
    chosen = jnp.zeros((ns, tq), jnp.bool_)
    for _ in range(n_top):
        best = jnp.max(score, axis=0, keepdims=True)
        first = jnp.min(jnp.where(score == best, blk_f, float(ns)), axis=0, keepdims=True)
        hit = blk_f == first
        chosen = chosen | hit
        score = jnp.where(hit, REMOVED, score)
    selbias_ref[...] = jnp.where(chosen, 0.0, NEG_BIG)

    per_tile = tk // NSA_SEL_BLOCK
    key_row = lax.broadcasted_iota(jnp.int32, (tk, 1), 0)

    pad_rows = jnp.zeros((ks_ref.shape[2] - HEAD_DIM - NSA_ONEHOT, cols), BF16)
    bias_pad = jnp.zeros((NSA_ONEHOT - per_tile, tq), F32)

    def sel_scores(j, live=None):
        start = pl.multiple_of(j * per_tile, per_tile)
        bias = selbias_ref[pl.ds(start, per_tile), :]
        if live is not None:
            bias = jnp.where(live, bias, NEG_BIG)
        bias = jnp.concatenate([bias, bias_pad], axis=0)
        bias = jnp.concatenate([bias.astype(BF16)] * N_HEADS, axis=1)
        rhs = jnp.concatenate([bias, qs, pad_rows], axis=0)
        return jnp.dot(ks_ref[j], rhs, preferred_element_type=F32)

    j_last = t0 // tk

    def sel_group(i, carry):
        tiles = []
        for u in range(NSA_UNROLL):
            j = i * NSA_UNROLL + u
            tiles.append((jnp.minimum(j, j_last - 1), None if u == 0 else j < j_last))
        scores = [sel_scores(j, live) for j, live in tiles]
        for (j, _), s in zip(tiles, scores):
            carry = online(carry, s, vs_ref[j])
        return carry

    carry = lax.fori_loop(0, (j_last + NSA_UNROLL - 1) // NSA_UNROLL, sel_group, init_aug)
    causal = jnp.where((j_last * tk + key_row) <= t_q, 0.0, NEG_BIG)
    _, acc_s = online(carry, sel_scores(j_last) + jnp.concatenate([causal] * N_HEADS, axis=1),
                      vs_ref[j_last])
    o_sel = normalise(acc_s)

    tw = kw_ref.shape[1]
    n_wt = (NSA_WINDOW + tq) // tw
    jw = (t0 - NSA_WINDOW) // tw
    wkey_row = lax.broadcasted_iota(jnp.int32, (tw, 1), 0)
    s_parts = []
    w_tiles = []
    for i in range(n_wt):
        exists = (jw + i) >= 0
        w_tiles.append(jnp.maximum(jw + i, 0))
        dist = t_q - ((jw + i) * tw + wkey_row)
        s = jnp.dot(kw_ref[w_tiles[i]], qs, preferred_element_type=F32)
        if i == 0:
            bias = jnp.where((dist < NSA_WINDOW) & exists, 0.0, NEG_BIG)
            s = s + jnp.concatenate([bias] * N_HEADS, axis=1)
        elif i == n_wt - 1:
            bias = jnp.where(dist >= 0, 0.0, NEG_BIG)
            s = s + jnp.concatenate([bias] * N_HEADS, axis=1)
        else:
            s = s + jnp.where(exists, 0.0, NEG_BIG)
        s_parts.append(s)
    m_w = s_parts[0].max(axis=0, keepdims=True)
    for sp in s_parts[1:]:
        m_w = jnp.maximum(m_w, sp.max(axis=0, keepdims=True))
    acc_w = jnp.zeros((vw_ref.shape[1], cols), F32)
    for i, sp in enumerate(s_parts):
        p = jnp.exp(sp - m_w)
        acc_w = acc_w + jnp.dot(vw_ref[w_tiles[i]], p.astype(BF16), preferred_element_type=F32)
    o_win = normalise(acc_w)

    gates = _sigmoid(g_ref[...]).T

    def gate_row(branch):
        return jnp.concatenate([gates[branch * N_HEADS + h:branch * N_HEADS + h + 1, :]
                                for h in range(N_HEADS)], axis=1)

    out = gate_row(0) * o_cmp + gate_row(1) * o_sel + gate_row(2) * o_win
    o_ref[...] = jnp.concatenate([out[:, h * tq:(h + 1) * tq].T for h in range(N_HEADS)], axis=-1)


def _nsa_call(u_nsa, pos_k, pos_v, k1, k2, v1, v2):
    bsz, seq, _ = u_nsa.shape
    ng = seq // GROUP
    xk = u_nsa[:, :, MIX:MIX + HEAD_DIM].reshape(bsz, ng, GROUP * HEAD_DIM)
    xv = u_nsa[:, :, MIX + HEAD_DIM:MIX + 2 * HEAD_DIM].reshape(bsz, ng, GROUP * HEAD_DIM)
    kvc = _nsa_compress_call(xk, xv, pos_k, pos_v, k1, k2, v1, v2)

    n_sel = seq // NSA_SEL_BLOCK
    n_top = min(NSA_TOP_N, n_sel)
    cmp_start = jnp.arange(ng) * NSA_CMP_STRIDE
    sel_start = jnp.arange(n_sel) * NSA_SEL_BLOCK
    overlap = ((cmp_start[:, None] < sel_start[None, :] + NSA_SEL_BLOCK)
               & (cmp_start[:, None] + NSA_CMP_BLOCK > sel_start[None, :])).astype(BF16)
    tq = min(NSA_TQ, seq)
    tk = min(NSA_TK, seq)
    tw = tq

    def key_tiles(a, tile):
        return a.astype(BF16).reshape(bsz, a.shape[1] // tile, tile, HEAD_DIM)

    def value_tiles(a, tile):
        return jnp.swapaxes(key_tiles(a, tile), -1, -2)

    def value_tiles_aug(a, tile):
        vt = value_tiles(a, tile)
        extra = jnp.zeros(vt.shape[:2] + (NSA_VROWS - HEAD_DIM, tile), BF16).at[:, :, 0, :].set(1.0)
        return jnp.concatenate([vt, extra], axis=2)

    def key_tiles_onehot(a, tile):
        kt = key_tiles(a, tile)
        local = jnp.arange(tile) // NSA_SEL_BLOCK
        onehot = (local[:, None] == jnp.arange(NSA_ONEHOT)[None, :]).astype(BF16)
        zeros = jnp.zeros(kt.shape[:3] + (128 - HEAD_DIM - NSA_ONEHOT,), BF16)
        return jnp.concatenate([jnp.broadcast_to(onehot, kt.shape[:2] + onehot.shape), kt, zeros],
                               axis=-1)

    col = lambda j: u_nsa[:, :, MIX + j * HEAD_DIM:MIX + (j + 1) * HEAD_DIM]
    operands = (u_nsa, u_nsa,
                key_tiles_onehot(col(2), tk), value_tiles_aug(col(3), tk),
                key_tiles(col(4), tw), value_tiles_aug(col(5), tw),
                key_tiles(kvc[:, :, :HEAD_DIM], ng)[:, 0],
                jnp.concatenate([value_tiles_aug(kvc[:, :, HEAD_DIM:], ng)[:, 0],
                                 jnp.broadcast_to(overlap.T, (bsz, n_sel, ng))], axis=1))
    per_batch = lambda t: pl.BlockSpec((None,) + t.shape[1:],
                                       lambda b, c: (b,) + (0,) * (t.ndim - 1))
    return pl.pallas_call(
        functools.partial(_nsa_kernel, n_top=n_top),
        grid=(bsz, seq // tq),
        in_specs=[
            pl.BlockSpec((None, tq, MIX), lambda b, c: (b, c, 0)),
            pl.BlockSpec((None, tq, 128), lambda b, c: (b, c, 5)),
        ] + [per_batch(t) for t in operands[2:]],
        out_specs=_row_spec(tq, MIX),
        out_shape=jax.ShapeDtypeStruct((bsz, seq, MIX), F32),
        scratch_shapes=[pltpu.VMEM((n_sel, tq), F32)],
        compiler_params=_params(("parallel", "arbitrary")),
    )(*operands)


HGRN_TB = 256
HGRN_C = 32


def _hgrn_steps(u_ref, lbl_ref, ng_ref, ones_ref, o_ref, state_ref, *, layer):
    c = pl.program_id(1)

    @pl.when(c == 0)
    def _():
        state_ref[...] = jnp.zeros_like(state_ref)

    logits = lbl_ref[...]
    ex = jnp.exp(logits - jnp.max(logits, axis=0, keepdims=True))
    soft = ex / jnp.sum(ex, axis=0, keepdims=True)
    lb = jnp.sum(soft[0:layer + 1], axis=0, keepdims=True) - soft[0:1]

    u = u_ref[...]
    tb = u.shape[0]
    q = _silu(u[:, 0:MIX])
    f = lb + (1.0 - lb) / (1.0 + jnp.exp(-u[:, MIX:2 * MIX]))
    logf = jnp.log(jnp.maximum(f, GATE_FLOOR))
    k = 1.0 - f
    v = u[:, 2 * MIX:3 * MIX]
    og = u[:, 3 * MIX:4 * MIX]

    C = HGRN_C
    SUB = 8
    ones_bd = ones_ref[...]
    ti = lax.broadcasted_iota(jnp.int32, (tb, tb), 0)
    si = lax.broadcasted_iota(jnp.int32, (tb, tb), 1)
    same_chunk = (ti // C) == (si // C)
    b_all = _split_dot((same_chunk & (ti >= si)).astype(BF16), logf, 3)
    b_last_all = _split_dot(same_chunk.astype(BF16), logf, 3)
    b2_all = b_all * math.log2(math.e)
    qe_all = q * jnp.exp(b_all)
    kd_all = k * jnp.exp(b_last_all - b_all)
    g_last_all = jnp.exp(b_last_all)
    row8 = lax.broadcasted_iota(jnp.int32, (SUB, 1), 0)
    hr = lax.broadcasted_iota(jnp.int32, (MIX, MIX), 0) // HEAD_DIM
    hc_ = lax.broadcasted_iota(jnp.int32, (MIX, MIX), 1) // HEAD_DIM
    same_head = hr == hc_

    state = state_ref[...]
    o_chunks = []
    yield
    for ch in range(tb // C):
        sl = slice(ch * C, (ch + 1) * C)
        qc, kc, vc, b2 = q[sl], k[sl], v[sl], b2_all[sl]
        pieces = []
        for s in range(C):
            r0 = (s // SUB) * SUB
            pm = qc[r0:] * (kc[s:s + 1, :] * jnp.exp2(b2[r0:] - b2[s:s + 1, :]))
            top = jnp.where(row8 + r0 >= s, pm[0:SUB], 0.0)
            pieces.append(top if C - r0 == SUB else jnp.concatenate([top, pm[SUB:]], axis=0))
        attn = jnp.dot(jnp.concatenate(pieces, axis=0).astype(BF16), ones_bd,
                       preferred_element_type=F32)
        groups = [jnp.zeros((SUB, MIX), F32) for _ in range(C // SUB)]
        off = 0
        for s in range(C):
            g0 = s // SUB
            for g in range(g0, C // SUB):
                groups[g] = groups[g] + attn[off:off + SUB, :] * vc[s:s + 1, :]
                off += SUB
        o = jnp.concatenate(groups, axis=0)
        o_chunks.append(o + _bdot_nt(qe_all[sl], state))
        outer = jnp.dot(vc.T.astype(BF16), kd_all[sl].astype(BF16), preferred_element_type=F32)
        state = state * g_last_all[ch * C:ch * C + 1, :] + jnp.where(same_head, outer, 0.0)
        yield
    state_ref[...] = state
    o = jnp.concatenate(o_chunks, axis=0)
    ms = _head_sum_mxu(o * o, ones_bd) * (1.0 / HEAD_DIM)
    o_ref[...] = o * lax.rsqrt(ms + NORM_EPS) * ng_ref[...] * _silu(og)


def _hgrn_kernel(*refs, layer):
    _drain(_hgrn_steps(*refs, layer=layer))


def _block_ones():
    hid = jnp.arange(MIX) // HEAD_DIM
    return (hid[:, None] == hid[None, :]).astype(BF16)


def _hgrn_operands(u, lb_logits, norm_g):
    return (u, lb_logits.astype(F32), norm_g.reshape(1, -1).astype(F32), _block_ones())


HGRN_SCRATCH = (pltpu.VMEM((MIX, MIX), F32),)


def _hgrn_call(u, lb_logits, norm_g, layer):
    args = _hgrn_operands(u, lb_logits, norm_g)
    bsz, seq, _ = u.shape
    tb = min(HGRN_TB, seq)
    return pl.pallas_call(
        functools.partial(_hgrn_kernel, layer=layer),
        grid=(bsz, seq // tb),
        in_specs=[_row_spec(tb, 4 * MIX)] + [_const_spec(t.shape) for t in args[1:]],
        out_specs=_row_spec(tb, MIX),
        out_shape=jax.ShapeDtypeStruct((bsz, seq, MIX), F32),
        scratch_shapes=list(HGRN_SCRATCH),
        compiler_params=_params(("parallel", "arbitrary")),
    )(*args)


def _hgrn_rwkv_kernel(*refs, layer, n_hgrn_in, n_rwkv_in):
    h_in = refs[:n_hgrn_in]
    r_in = refs[n_hgrn_in:n_hgrn_in + n_rwkv_in]
    o_h, o_r, state_h, state_r, prev_r = refs[n_hgrn_in + n_rwkv_in:]
    hgrn = _hgrn_steps(*h_in, o_h, state_h, layer=layer)
    rwkv = _rwkv_steps(*r_in, o_r, state_r, prev_r)
    per_round = (HGRN_TB // HGRN_C) // (RWKV_TB // RWKV_C)
    live = True
    while live:
        live = next(rwkv, "done") != "done"
        for _ in range(per_round):
            live = (next(hgrn, "done") != "done") or live


def _hgrn_rwkv_call(u_hgrn, lb_logits, norm_g, layer, u_rwkv, *rwkv_params):
    h_args = _hgrn_operands(u_hgrn, lb_logits, norm_g)
    r_args = _rwkv_operands(u_rwkv, *rwkv_params)
    bsz, seq, _ = u_hgrn.shape
    tb = min(HGRN_TB, seq)
    specs = lambda args: ([_row_spec(tb, 4 * MIX)] + [_const_spec(t.shape) for t in args[1:]])
    out = jax.ShapeDtypeStruct((bsz, seq, MIX), F32)
    return pl.pallas_call(
        functools.partial(_hgrn_rwkv_kernel, layer=layer, n_hgrn_in=len(h_args),
                          n_rwkv_in=len(r_args)),
        grid=(bsz, seq // tb),
        in_specs=specs(h_args) + specs(r_args),
        out_specs=[_row_spec(tb, MIX), _row_spec(tb, MIX)],
        out_shape=[out, out],
        scratch_shapes=list(HGRN_SCRATCH + RWKV_SCRATCH),
        compiler_params=_params(("parallel", "arbitrary")),
    )(*h_args, *r_args)


RET_C = 256


def _ret_kernel(u_ref, cos_ref, sin_ref, dm_ref, qd_ref, kd_ref, cd_ref, ng_ref, nb_ref,
                ones_ref, o_ref, state_ref):
    c = pl.program_id(1)

    @pl.when(c == 0)
    def _():
        state_ref[...] = jnp.zeros_like(state_ref)

    u = u_ref[...]
    cosf = cos_ref[...]
    sins = sin_ref[...]
    half = HEAD_DIM // 2

    def rope(a):
        outs = []
        for j in range(MIX // 128):
            blk = a[:, j * 128:(j + 1) * 128]
            lane = lax.broadcasted_iota(jnp.int32, blk.shape, 1)
            swapped = jnp.where((lane % HEAD_DIM) < half, pltpu.roll(blk, 128 - half, axis=1),
                                pltpu.roll(blk, half, axis=1))
            outs.append(swapped)
        return a * cosf + jnp.concatenate(outs, axis=-1) * sins

    q = rope(u[:, 0:MIX])
    k = rope(u[:, MIX:2 * MIX]) * (HEAD_DIM ** -0.5)
    v = u[:, 2 * MIX:3 * MIX]
    g = u[:, 3 * MIX:4 * MIX]
    qd = q * qd_ref[...]
    kd = k * kd_ref[...]
    C = u.shape[0]
    row_head = lax.broadcasted_iota(jnp.int32, (N_HEADS * C, MIX), 0) // C
    lane_head = lax.broadcasted_iota(jnp.int32, (N_HEADS * C, MIX), 1) // HEAD_DIM
    own = row_head == lane_head
    q_stack = jnp.where(own, jnp.concatenate([q] * N_HEADS, axis=0), 0.0)
    s = _bdot_nt(q_stack, k) * dm_ref[...]
    sv = jnp.where(own, _bdot(s, v), 0.0)
    o = sv[0:C]
    for h in range(1, N_HEADS):
        o = o + sv[h * C:(h + 1) * C]
    state = state_ref[...]
    o = o + _bdot(qd, state)
    kr = lax.broadcasted_iota(jnp.int32, (MIX, MIX), 0) // HEAD_DIM
    kc = lax.broadcasted_iota(jnp.int32, (MIX, MIX), 1) // HEAD_DIM
    state_ref[...] = state * cd_ref[...] + jnp.where(kr == kc, _bdot(kd.T, v), 0.0)
    ones = ones_ref[...]
    mean = _head_sum_mxu(o, ones) * (1.0 / HEAD_DIM)
    oc = o - mean
    var = _head_sum_mxu(oc * oc, ones) * (1.0 / HEAD_DIM)
    y = oc * lax.rsqrt(var + RET_GN_EPS) * ng_ref[...] + nb_ref[...]
    o_ref[...] = y * _silu(g)


def _ret_call(u, norm_g, norm_b):
    bsz, seq, _ = u.shape
    C = min(RET_C, seq)
    half = HEAD_DIM // 2
    pos = jnp.arange(seq, dtype=F32)
    inv_freq = RET_ROPE_BASE ** (-jnp.arange(0, HEAD_DIM, 2, dtype=F32) / HEAD_DIM)
    ang = pos[:, None] * inv_freq[None, :]
    cos, sin = jnp.cos(ang), jnp.sin(ang)
    cosf = jnp.tile(jnp.concatenate([cos, cos], axis=-1), (1, N_HEADS))
    sins = jnp.tile(jnp.concatenate([-sin, sin], axis=-1), (1, N_HEADS))
    log_gamma = jnp.log(1.0 - jnp.exp2(-5.0 - jnp.arange(N_HEADS, dtype=F32)))
    i = jnp.arange(C, dtype=F32)
    dpos = i[:, None] - i[None, :]
    dm = jnp.where(dpos >= 0, jnp.exp(jnp.maximum(dpos, 0.0)[None] * log_gamma[:, None, None]), 0.0)
    lanes = lambda t: jnp.repeat(t, HEAD_DIM, axis=-1)
    qd = lanes(jnp.exp((i + 1.0)[:, None] * log_gamma[None, :]))
    kd = lanes(jnp.exp((C - 1.0 - i)[:, None] * log_gamma[None, :]))
    cd = lanes(jnp.exp(C * log_gamma)[None, :])
    args = (u, cosf, sins, dm.reshape(N_HEADS * C, C), qd, kd, cd,
            norm_g.reshape(1, -1).astype(F32), norm_b.reshape(1, -1).astype(F32), _block_ones())
    in_specs = ([_row_spec(C, 4 * MIX),
                 pl.BlockSpec((C, MIX), lambda b, c: (c, 0)),
                 pl.BlockSpec((C, MIX), lambda b, c: (c, 0))]
                + [_const_spec(t.shape) for t in args[3:]])
    return pl.pallas_call(
        _ret_kernel,
        grid=(bsz, seq // C),
        in_specs=in_specs,
        out_specs=_row_spec(C, MIX),
        out_shape=jax.ShapeDtypeStruct((bsz, seq, MIX), F32),
        scratch_shapes=[pltpu.VMEM((MIX, MIX), F32)],
        compiler_params=_params(("parallel", "arbitrary")),
    )(*args)


DENSE_TM = 512
FFN_TF = 256
GROUP_COLS = (NSA_PAD, 4 * MIX, 4 * MIX, 4 * MIX)


def _rmsnorm(x, g):
    return x * lax.rsqrt(jnp.mean(x * x, axis=-1, keepdims=True) + NORM_EPS) * g


def _resident(shape):
    nd = len(shape)
    return pl.BlockSpec(shape, lambda b, c: (0,) * nd, pipeline_mode=pl.Buffered(1))


def _inproj_kernel(h_ref, g_ref, w_ref, nsa_ref, hg_ref, rt_ref, rw_ref):
    xn = _rmsnorm(h_ref[...], g_ref[...]).astype(BF16)
    off = 0
    for ref, width in zip((nsa_ref, hg_ref, rt_ref, rw_ref), GROUP_COLS):
        ref[...] = jnp.dot(xn, w_ref[:, off:off + width], preferred_element_type=F32)
        off += width


def _inproj_call(h, g, w_pad):
    bsz, seq, _ = h.shape
    tm = min(DENSE_TM, seq)
    return pl.pallas_call(
        _inproj_kernel,
        grid=(bsz, seq // tm),
        in_specs=[_row_spec(tm, D_MODEL), _resident((1, D_MODEL)), _resident(w_pad.shape)],
        out_specs=[_row_spec(tm, w) for w in GROUP_COLS],
        out_shape=[jax.ShapeDtypeStruct((bsz, seq, w), F32) for w in GROUP_COLS],
        compiler_params=_params(("parallel", "parallel")),
    )(h, g.reshape(1, -1), w_pad)


def _merge_kernel(h_ref, b0_ref, b1_ref, b2_ref, b3_ref, g_ref, wg_ref, bg_ref, wb_ref, wo_ref,
                  o_ref):
    h = h_ref[...]
    xn = _rmsnorm(h, g_ref[...]).astype(BF16)
    merged = None
    for m, b_ref in enumerate((b0_ref, b1_ref, b2_ref, b3_ref)):
        gate = _sigmoid(jnp.dot(xn, wg_ref[m], preferred_element_type=F32) + bg_ref[m])
        term = gate * _bdot(b_ref[...], wb_ref[m])
        merged = term if merged is None else merged + term
    o_ref[...] = h + _bdot(merged, wo_ref[...])


def _merge_call(h, branches, g, w_gate, b_gate, w_branch, w_out):
    bsz, seq, _ = h.shape
    tm = min(DENSE_TM, seq)
    consts = (g.reshape(1, -1), w_gate, b_gate.reshape(4, 1, D_MODEL), w_branch, w_out)
    return pl.pallas_call(
        _merge_kernel,
        grid=(bsz, seq // tm),
        in_specs=([_row_spec(tm, D_MODEL)] + [_row_spec(tm, MIX)] * 4
                  + [_resident(t.shape) for t in consts]),
        out_specs=_row_spec(tm, D_MODEL),
        out_shape=jax.ShapeDtypeStruct(h.shape, F32),
        compiler_params=_params(("parallel", "parallel")),
    )(h, *branches, *consts)


def _ffn_kernel(h_ref, g_ref, wg_ref, wu_ref, wd_ref, o_ref):
    h = h_ref[...]
    hn = _rmsnorm(h, g_ref[...]).astype(BF16)
    acc = h
    for f in range(0, D_FF, FFN_TF):
        gate = jnp.dot(hn, wg_ref[:, f:f + FFN_TF], preferred_element_type=F32)
        up = jnp.dot(hn, wu_ref[:, f:f + FFN_TF], preferred_element_type=F32)
        acc = acc + _bdot(_silu(gate) * up, wd_ref[f:f + FFN_TF, :])
    o_ref[...] = acc


def _ffn_call(h, g, w_gate, w_up, w_down):
    bsz, seq, _ = h.shape
    tm = min(DENSE_TM, seq)
    consts = (g.reshape(1, -1), w_gate, w_up, w_down)
    return pl.pallas_call(
        _ffn_kernel,
        grid=(bsz, seq // tm),
        in_specs=[_row_spec(tm, D_MODEL)] + [_resident(t.shape) for t in consts],
        out_specs=_row_spec(tm, D_MODEL),
        out_shape=jax.ShapeDtypeStruct(h.shape, F32),
        compiler_params=_params(("parallel", "parallel")),
    )(h, *consts)


def _ple_kernel(h_ref, p_ref, g_ref, wg_ref, wp_ref, gf_ref, o_ref, *, final_norm):
    h = h_ref[...]
    hp = _rmsnorm(h, g_ref[...])
    out = h + _sigmoid(_bdot(hp, wg_ref[...])) * _bdot(p_ref[...], wp_ref[...])
    if final_norm:
        out = _rmsnorm(out, gf_ref[...])
    o_ref[...] = out


def _ple_call(h, p, g, w_gate, w_proj, g_final, final_norm):
    bsz, seq, _ = h.shape
    tm = min(DENSE_TM, seq)
    consts = (g.reshape(1, -1), w_gate, w_proj, g_final.reshape(1, -1))
    return pl.pallas_call(
        functools.partial(_ple_kernel, final_norm=final_norm),
        grid=(bsz, seq // tm),
        in_specs=([_row_spec(tm, D_MODEL), _row_spec(tm, PLE_DIM)]
                  + [_resident(t.shape) for t in consts]),
        out_specs=_row_spec(tm, D_MODEL),
        out_shape=jax.ShapeDtypeStruct(h.shape, F32),
        compiler_params=_params(("parallel", "parallel")),
    )(h, p, *consts)


def kernel(x, p, norm_mix, w_in, nsa_pos_k, nsa_pos_v, nsa_cmp_k1, nsa_cmp_k2, nsa_cmp_v1,
           nsa_cmp_v2, hgrn_lb_logits, hgrn_norm, ret_norm_g, ret_norm_b, rwkv_mu, rwkv_w0,
           rwkv_w_up, rwkv_a0, rwkv_a_up, rwkv_g_up, rwkv_k_k, rwkv_k_a, rwkv_r_k, rwkv_norm_g,
           rwkv_norm_b, w_branch, w_gate, b_gate, w_out, norm_ffn, w_ffn_gate, w_ffn_up,
           w_ffn_down, norm_ple, w_ple_gate, w_ple_proj, norm_final):
    depth = w_in.shape[0]
    w_in_pad = jnp.concatenate(
        [w_in[:, :, :NSA_WIDTH], jnp.zeros((depth, D_MODEL, NSA_PAD - NSA_WIDTH), w_in.dtype),
         w_in[:, :, NSA_WIDTH:]], axis=-1).astype(BF16)
    bf = lambda t: t.astype(BF16)
    h = x
    for i in range(depth):
        u_nsa, u_hgrn, u_ret, u_rwkv = _inproj_call(h, norm_mix[i], w_in_pad[i])
        o_hgrn, o_rwkv = _hgrn_rwkv_call(
            u_hgrn, hgrn_lb_logits, hgrn_norm[i], i,
            u_rwkv, rwkv_mu[i], rwkv_w0[i], rwkv_w_up[i], rwkv_a0[i], rwkv_a_up[i], rwkv_g_up[i],
            rwkv_k_k[i], rwkv_k_a[i], rwkv_r_k[i], rwkv_norm_g[i], rwkv_norm_b[i])
        branches = (
            _nsa_call(u_nsa, nsa_pos_k[i], nsa_pos_v[i], nsa_cmp_k1[i], nsa_cmp_k2[i],
                      nsa_cmp_v1[i], nsa_cmp_v2[i]),
            o_hgrn,
            _ret_call(u_ret, ret_norm_g[i], ret_norm_b[i]),
            o_rwkv,
        )
        h = _merge_call(h, branches, norm_mix[i], bf(w_gate[i]), b_gate[i], bf(w_branch[i]),
                        bf(w_out[i]))
        h = _ffn_call(h, norm_ffn[i], bf(w_ffn_gate[i]), bf(w_ffn_up[i]), bf(w_ffn_down[i]))
        h = _ple_call(h, p[i], norm_ple[i], bf(w_ple_gate[i]), bf(w_ple_proj[i]), norm_final,
                      final_norm=(i == depth - 1))
    return h
```

```python
import functools
import math

import jax
import jax.numpy as jnp
from jax import lax
from jax.experimental import pallas as pl
from jax.experimental.pallas import tpu as pltpu

F32 = jnp.float32
BF16 = jnp.bfloat16

D_MODEL = 1024
N_HEADS = 4
HEAD_DIM = 64
MIX = N_HEADS * HEAD_DIM
D_FF = 2816
PLE_DIM = 256
NORM_EPS = 1e-6
NEG_BIG = -1e30
POS_BIG = 1e30
GATE_FLOOR = 1e-20

NSA_CMP_BLOCK = 32
NSA_CMP_STRIDE = 16
NSA_SEL_BLOCK = 64
NSA_TOP_N = 16
NSA_WINDOW = 512
NSA_CMP_HIDDEN = 128
NSA_WIDTH = 652
NSA_PAD = 768

RET_ROPE_BASE = 10000.0
RET_GN_EPS = 1e-5
RWKV_GN_EPS = 64e-5

VMEM_LIMIT = 56 * 1024 * 1024


def _bdot(a, b):
    return jnp.dot(a.astype(BF16), b.astype(BF16), preferred_element_type=F32)


def _bdot_nt(a, b):
    return lax.dot_general(a.astype(BF16), b.astype(BF16), (((1,), (1,)), ((), ())),
                           preferred_element_type=F32)


def _sigmoid(x):
    return 0.5 * jnp.tanh(0.5 * x) + 0.5


def _silu(x):
    return x * _sigmoid(x)


def _params(sem):
    return pltpu.CompilerParams(dimension_semantics=sem, vmem_limit_bytes=VMEM_LIMIT)


def _row_spec(tile, width):
    return pl.BlockSpec((None, tile, width), lambda b, c: (b, c, 0))


def _const_spec(shape):
    nd = len(shape)
    return pl.BlockSpec(shape, lambda b, c: (0,) * nd)


REC_TB = 256
RWKV_C = 64


def _split_dot(a_bf16, x, parts):
    total = None
    rest = x
    for _ in range(parts):
        piece = rest.astype(BF16)
        rest = rest - piece.astype(F32)
        term = jnp.dot(a_bf16, piece, preferred_element_type=F32)
        total = term if total is None else total + term
    return total


def _head_sum_mxu(x, ones_t):
    hi = x.astype(BF16)
    lo = (x - hi.astype(F32)).astype(BF16)
    return (jnp.dot(hi, ones_t, preferred_element_type=F32)
            + jnp.dot(lo, ones_t, preferred_element_type=F32))


def _rwkv_steps(u_ref, mu_ref, w0_ref, wup_ref, a0_ref, aup_ref, gup_ref, kk_ref, ka_ref,
                rk_ref, ng_ref, nb_ref, ones_ref, o_ref, state_ref, prev_ref):
    c = pl.program_id(1)
    ones = ones_ref[...]

    @pl.when(c == 0)
    def _():
        state_ref[...] = jnp.zeros_like(state_ref)
        prev_ref[...] = jnp.zeros_like(prev_ref)

    u = u_ref[...]
    tb = u.shape[0]
    row = lax.broadcasted_iota(jnp.int32, u.shape, 0)
    u_prev = jnp.where(row == 0, prev_ref[...], pltpu.roll(u, 1, axis=0))
    prev_ref[...] = u[tb - 1:tb, :]
    xs = u + mu_ref[...] * (u_prev - u)
    r = xs[:, 0:MIX]
    k = xs[:, MIX:2 * MIX]
    v = xs[:, 2 * MIX:3 * MIX]
    w_lo = xs[:, 3 * MIX:3 * MIX + 64]
    a_lo = xs[:, 3 * MIX + 64:3 * MIX + 128]
    g_lo = xs[:, 3 * MIX + 128:3 * MIX + 256]

    logw = -math.exp(-0.5) * _sigmoid(w0_ref[...] + _bdot(jnp.tanh(w_lo), wup_ref[...]))
    a = _sigmoid(a0_ref[...] + _bdot(a_lo, aup_ref[...]))
    g = _bdot(_sigmoid(g_lo), gup_ref[...])
    kk = k * kk_ref[...]
    kk = kk * lax.rsqrt(jnp.maximum(_head_sum_mxu(kk * kk, ones), 1e-24))
    k2 = k * (1.0 + (a - 1.0) * ka_ref[...])
    alpha = -kk
    beta = kk * a
    bonus = _head_sum_mxu(r * k2 * rk_ref[...], ones) * v

    C = RWKV_C
    ti = lax.broadcasted_iota(jnp.int32, (tb, tb), 0)
    si = lax.broadcasted_iota(jnp.int32, (tb, tb), 1)
    same_chunk = (ti // C) == (si // C)
    prefix = (same_chunk & (ti >= si)).astype(BF16)
    cum = _split_dot(prefix, logw, 3)
    cum_last = _split_dot(same_chunk.astype(BF16), logw, 3)
    gam_all = jnp.exp(cum_last)
    e_inv = jnp.exp(-cum)
    e_last = jnp.exp(cum_last - cum)
    ag_all = alpha * jnp.exp(cum - logw)
    rg_all = r * jnp.exp(cum)
    bi_all = beta * e_inv
    ki_all = k2 * e_inv
    bl_all = beta * e_last
    kl_all = k2 * e_last

    hc = N_HEADS * C
    row_head = lax.broadcasted_iota(jnp.int32, (hc, MIX), 0) // C
    lane_head = lax.broadcasted_iota(jnp.int32, (hc, MIX), 1) // HEAD_DIM
    own = row_head == lane_head

    def stack(x):
        return jnp.where(own, jnp.concatenate([x] * N_HEADS, axis=0), 0.0).astype(BF16)

    rr = lax.broadcasted_iota(jnp.int32, (hc, hc), 0)
    cc = lax.broadcasted_iota(jnp.int32, (hc, hc), 1)
    strict = rr > cc
    incl = rr >= cc
    eye_hc = (rr == cc).astype(F32)
    kr = lax.broadcasted_iota(jnp.int32, (MIX, MIX), 0)
    kc_ = lax.broadcasted_iota(jnp.int32, (MIX, MIX), 1)
    eye_k = kr == kc_

    state = state_ref[...]
    y_chunks = []
    yield
    for ch in range(tb // C):
        sl = slice(ch * C, (ch + 1) * C)
        ag, rg, bi, ki, bl, kl, vm = (stack(t[sl]) for t in (ag_all, rg_all, bi_all, ki_all,
                                                               bl_all, kl_all, v))
        aa = _bdot_nt(jnp.concatenate([ag, rg], axis=0), jnp.concatenate([bi, ki], axis=0))
        a_ab = jnp.where(strict, aa[:hc, :hc], 0.0)
        a_ak = jnp.where(strict, aa[:hc, hc:], 0.0)
        a_rb = jnp.where(incl, aa[hc:, :hc], 0.0)
        a_rk = jnp.where(incl, aa[hc:, hc:], 0.0)
        t_inv = eye_hc + a_ab
        pw = _bdot(a_ab, a_ab)
        for _ in range(int(math.log2(C)) - 2):
            both = _bdot(pw, jnp.concatenate([t_inv, pw], axis=-1))
            t_inv = t_inv + both[:, :hc]
            pw = both[:, hc:]
        t_inv = t_inv + _bdot(pw, t_inv)
        w12 = _bdot(t_inv, jnp.concatenate([ag.astype(F32), _bdot(a_ak, vm)], axis=-1))
        ry = _bdot(a_rb, w12)
        rq = rg.astype(F32) + ry[:, :MIX]
        y0 = ry[:, MIX:] + _bdot(a_rk, vm)
        mn = _bdot(bl.T, w12)
        m_mat = jnp.where(eye_k, gam_all[ch * C:ch * C + 1, :], 0.0) + mn[:, :MIX]
        n_mat = mn[:, MIX:] + _bdot(kl.T, vm)
        prod = _bdot(jnp.concatenate([rq, m_mat], axis=0), state)
        ym = prod[:hc] + y0
        y_chunks.append(ym[0:C] + ym[C:2 * C] + ym[2 * C:3 * C] + ym[3 * C:4 * C])
        state = prod[hc:] + n_mat
        yield
    state_ref[...] = state
    y = jnp.concatenate(y_chunks, axis=0)

    mean = _head_sum_mxu(y, ones) * (1.0 / HEAD_DIM)
    yc = y - mean
    var = _head_sum_mxu(yc * yc, ones) * (1.0 / HEAD_DIM)
    yn = yc * lax.rsqrt(var + RWKV_GN_EPS) * ng_ref[...] + nb_ref[...]
    o_ref[...] = (yn + bonus) * g


NSA_TQ = 256
NSA_TK = 512
NSA_VROWS = 80
NSA_UNROLL = 2
NSA_CMP_PARTS = 4
NSA_ONEHOT = 16
GROUP = NSA_CMP_STRIDE


def _ones_row_block(width):
    row = lax.broadcasted_iota(jnp.int32, (NSA_VROWS - HEAD_DIM, width), 0)
    return (row == 0).astype(F32)


def _nsa_compress_kernel(xk_ref, xv_ref, pk_ref, pv_ref, k1_ref, k2_ref, v1_ref, v2_ref, ov_ref,
                         kc_ref, vc_ref):
    half = GROUP * HEAD_DIM

    def compress(x, pos, w1_ref, w2_ref):
        n = x.shape[0]
        first = _bdot(x + pos[:, :half], w1_ref[0:half, :])
        second = _bdot(x + pos[:, half:], w1_ref[half:2 * half, :])
        hid = first + pltpu.roll(second, n - 1, axis=0)
        return _bdot(_silu(hid), w2_ref[...])

    kc = compress(xk_ref[...], pk_ref[...], k1_ref, k2_ref)
    vc = compress(xv_ref[...], pv_ref[...], v1_ref, v2_ref)
    kc_ref[...] = kc.astype(BF16)
    ng = vc.shape[0]
    vc_ref[0:NSA_VROWS, :] = jnp.concatenate([vc.T, _ones_row_block(ng)], axis=0).astype(BF16)
    vc_ref[NSA_VROWS:, :] = ov_ref[...]


def _nsa_compress_call(xk, xv, pos_k, pos_v, k1, k2, v1, v2, overlap_t):
    bsz, ng, width = xk.shape
    n_sel = overlap_t.shape[0]
    args = (xk, xv, pos_k.reshape(1, -1), pos_v.reshape(1, -1), k1.astype(BF16), k2.astype(BF16),
            v1.astype(BF16), v2.astype(BF16), overlap_t)
    blk = pl.BlockSpec((None, ng, width), lambda b: (b, 0, 0))
    const = lambda t: pl.BlockSpec(t.shape, lambda b: (0,) * t.ndim)
    return pl.pallas_call(
        _nsa_compress_kernel,
        grid=(bsz,),
        in_specs=[blk, blk] + [const(t) for t in args[2:]],
        out_specs=[pl.BlockSpec((None, ng, HEAD_DIM), lambda b: (b, 0, 0)),
                   pl.BlockSpec((None, NSA_VROWS + n_sel, ng), lambda b: (b, 0, 0))],
        out_shape=[jax.ShapeDtypeStruct((bsz, ng, HEAD_DIM), BF16),
                   jax.ShapeDtypeStruct((bsz, NSA_VROWS + n_sel, ng), BF16)],
        compiler_params=_params(("parallel",)),
    )(*args)


REMOVED = -3e38


def _nsa_kernel(q_ref, g_ref, ks_ref, vs_ref, kw_ref, vw_ref, kc_ref, vc_ref, o_ref,
                selbias_ref, *, n_top):
    c = pl.program_id(1)
    tq = q_ref.shape[0]
    tk = ks_ref.shape[1]
    ct = kc_ref.shape[0]
    ns = selbias_ref.shape[0]
    t0 = c * tq
    cols = N_HEADS * tq

    q_t = (q_ref[...] * (HEAD_DIM ** -0.5)).T
    qs = jnp.concatenate([q_t[h * HEAD_DIM:(h + 1) * HEAD_DIM] for h in range(N_HEADS)],
                         axis=1).astype(BF16)
    t_q = t0 + lax.broadcasted_iota(jnp.int32, (1, tq), 1)
    t_col = jnp.concatenate([t_q] * N_HEADS, axis=1)

    def online(carry, s, v_aug):
        m, acc = carry
        m_new = jnp.maximum(m, jnp.max(s, axis=0, keepdims=True))
        p = jnp.exp(s - m_new)
        acc = jnp.exp(m - m_new) * acc + jnp.dot(v_aug, p.astype(BF16), preferred_element_type=F32)
        return m_new, acc

    def normalise(acc):
        return acc[0:HEAD_DIM] / acc[HEAD_DIM:HEAD_DIM + 1]

    init_aug = (jnp.full((1, cols), NEG_BIG, F32), jnp.zeros((vs_ref.shape[1], cols), F32))

    vrows = vs_ref.shape[1]

    def cmp_quarters(quarters):
        rows = quarters * (ct // NSA_CMP_PARTS)
        n_row = lax.broadcasted_iota(jnp.int32, (rows, 1), 0)
        valid = (n_row * NSA_CMP_STRIDE + (NSA_CMP_BLOCK - 1)) <= t_col
        s = jnp.where(valid, jnp.dot(kc_ref[0:rows, :], qs, preferred_element_type=F32), NEG_BIG)
        m = jnp.max(s, axis=0, keepdims=True)
        p = jnp.exp(s - m)
        p_hi = p.astype(BF16)
        p_lo = (p - p_hi.astype(F32)).astype(BF16)
        res = jnp.dot(vc_ref[:, 0:rows], p_hi, preferred_element_type=F32)
        imp_lo = jnp.dot(vc_ref[vrows:, 0:rows], p_lo, preferred_element_type=F32)
        return m, jnp.concatenate([res[:vrows], res[vrows:] + imp_lo], axis=0)

    last_valid = (t0 + tq - NSA_CMP_BLOCK) // NSA_CMP_STRIDE
    quarter = jnp.clip(last_valid // (ct // NSA_CMP_PARTS), 0, NSA_CMP_PARTS - 1)
    m_c, res_c = lax.switch(quarter, [functools.partial(cmp_quarters, i + 1)
                                      for i in range(NSA_CMP_PARTS)])
    inv_c = jnp.where(m_c > 0.5 * NEG_BIG, 1.0 / res_c[HEAD_DIM:HEAD_DIM + 1], 0.0)
    o_cmp = res_c[0:HEAD_DIM] * inv_c
    imp4 = res_c[vrows:] * inv_c
    imp = imp4[:, 0:tq]
    for h in range(1, N_HEADS):
        imp = imp + imp4[:, h * tq:(h + 1) * tq]

    blk = lax.broadcasted_iota(jnp.int32, (ns, 1), 0)
    blk_f = blk.astype(F32)
    cur = t_q // NSA_SEL_BLOCK
    forced = (blk == 0) | (blk == cur) | (blk == cur - 1)
    score = jnp.where(forced, POS_BIG, jnp.where(blk <= cur, imp, NEG_BIG))
    chosen = jnp.zeros((ns, tq), jnp.bool_)
    for _ in range(n_top):
        best = jnp.max(score, axis=0, keepdims=True)
        first = jnp.min(jnp.where(score == best, blk_f, float(ns)), axis=0, keepdims=True)
        hit = blk_f == first
        chosen = chosen | hit
        score = jnp.where(hit, REMOVED, score)
    selbias_ref[...] = jnp.where(chosen, 0.0, NEG_BIG)

    per_tile = tk // NSA_SEL_BLOCK
    key_row = lax.broadcasted_iota(jnp.int32, (tk, 1), 0)

    pad_rows = jnp.zeros((ks_ref.shape[2] - HEAD_DIM - NSA_ONEHOT, cols), BF16)
    bias_pad = jnp.zeros((NSA_ONEHOT - per_tile, tq), F32)

    def sel_scores(j, live=None):
        start = pl.multiple_of(j * per_tile, per_tile)
        bias = selbias_ref[pl.ds(start, per_tile), :]
        if live is not None:
            bias = jnp.where(live, bias, NEG_BIG)
        bias = jnp.concatenate([bias, bias_pad], axis=0)
        bias = jnp.concatenate([bias.astype(BF16)] * N_HEADS, axis=1)
        rhs = jnp.concatenate([bias, qs, pad_rows], axis=0)
        return jnp.dot(ks_ref[j], rhs, preferred_element_type=F32)

    j_last = t0 // tk

    def sel_group(i, carry):
        tiles = []
        for u in range(NSA_UNROLL):
            j = i * NSA_UNROLL + u
            tiles.append((jnp.minimum(j, j_last - 1), None if u == 0 else j < j_last))
        scores = [sel_scores(j, live) for j, live in tiles]
        for (j, _), s in zip(tiles, scores):
            carry = online(carry, s, vs_ref[j])
        return carry

    carry = lax.fori_loop(0, (j_last + NSA_UNROLL - 1) // NSA_UNROLL, sel_group, init_aug)
    causal = jnp.where((j_last * tk + key_row) <= t_q, 0.0, NEG_BIG)
    _, acc_s = online(carry, sel_scores(j_last) + jnp.concatenate([causal] * N_HEADS, axis=1),
                      vs_ref[j_last])
    o_sel = normalise(acc_s)

    tw = kw_ref.shape[1]
    n_wt = (NSA_WINDOW + tq) // tw
    jw = (t0 - NSA_WINDOW) // tw
    wkey_row = lax.broadcasted_iota(jnp.int32, (tw, 1), 0)
    s_parts = []
    w_tiles = []
    for i in range(n_wt):
        exists = (jw + i) >= 0
        w_tiles.append(jnp.maximum(jw + i, 0))
        dist = t_q - ((jw + i) * tw + wkey_row)
        s = jnp.dot(kw_ref[w_tiles[i]], qs, preferred_element_type=F32)
        if i == 0:
            bias = jnp.where((dist < NSA_WINDOW) & exists, 0.0, NEG_BIG)
            s = s + jnp.concatenate([bias] * N_HEADS, axis=1)
        elif i == n_wt - 1:
            bias = jnp.where(dist >= 0, 0.0, NEG_BIG)
            s = s + jnp.concatenate([bias] * N_HEADS, axis=1)
        else:
            s = s + jnp.where(exists, 0.0, NEG_BIG)
        s_parts.append(s)
    m_w = s_parts[0].max(axis=0, keepdims=True)
    for sp in s_parts[1:]:
        m_w = jnp.maximum(m_w, sp.max(axis=0, keepdims=True))
    acc_w = jnp.zeros((vw_ref.shape[1], cols), F32)
    for i, sp in enumerate(s_parts):
        p = jnp.exp(sp - m_w)
        acc_w = acc_w + jnp.dot(vw_ref[w_tiles[i]], p.astype(BF16), preferred_element_type=F32)
    o_win = normalise(acc_w)

    gates = _sigmoid(g_ref[...]).T

    def gate_row(branch):
        return jnp.concatenate([gates[branch * N_HEADS + h:branch * N_HEADS + h + 1, :]
                                for h in range(N_HEADS)], axis=1)

    out = gate_row(0) * o_cmp + gate_row(1) * o_sel + gate_row(2) * o_win
    o_ref[...] = jnp.concatenate([out[:, h * tq:(h + 1) * tq].T for h in range(N_HEADS)], axis=-1)


def _nsa_call(q, gates, k_cmp, v_cmp, k_sel, v_sel, k_win, v_win, pos_k, pos_v, k1, k2, v1, v2):
    bsz, seq, _ = q.shape
    ng = seq // GROUP
    n_sel = seq // NSA_SEL_BLOCK
    n_top = min(NSA_TOP_N, n_sel)
    cmp_start = jnp.arange(ng) * NSA_CMP_STRIDE
    sel_start = jnp.arange(n_sel) * NSA_SEL_BLOCK
    overlap = ((cmp_start[:, None] < sel_start[None, :] + NSA_SEL_BLOCK)
               & (cmp_start[:, None] + NSA_CMP_BLOCK > sel_start[None, :])).astype(BF16)
    kc, vc_stack = _nsa_compress_call(
        k_cmp.reshape(bsz, ng, GROUP * HEAD_DIM), v_cmp.reshape(bsz, ng, GROUP * HEAD_DIM),
        pos_k, pos_v, k1, k2, v1, v2, overlap.T)
    tq = min(NSA_TQ, seq)
    tk, tw = v_sel.shape[-1], v_win.shape[-1]
    operands = (q, gates,
                k_sel.reshape(bsz, seq // tk, tk, k_sel.shape[-1]), v_sel,
                k_win.reshape(bsz, seq // tw, tw, HEAD_DIM), v_win,
                kc, vc_stack)
    per_batch = lambda t: pl.BlockSpec((None,) + t.shape[1:],
                                       lambda b, c: (b,) + (0,) * (t.ndim - 1))
    return pl.pallas_call(
        functools.partial(_nsa_kernel, n_top=n_top),
        grid=(bsz, seq // tq),
        in_specs=[_row_spec(tq, MIX), _row_spec(tq, 128)] + [per_batch(t) for t in operands[2:]],
        out_specs=_row_spec(tq, MIX),
        out_shape=jax.ShapeDtypeStruct((bsz, seq, MIX), F32),
        scratch_shapes=[pltpu.VMEM((n_sel, tq), F32)],
        compiler_params=_params(("parallel", "arbitrary")),
    )(*operands)


HGRN_C = 32


def _hgrn_steps(u_ref, lbl_ref, ng_ref, ones_ref, o_ref, state_ref, *, layer):
    c = pl.program_id(1)

    @pl.when(c == 0)
    def _():
        state_ref[...] = jnp.zeros_like(state_ref)

    logits = lbl_ref[...]
    ex = jnp.exp(logits - jnp.max(logits, axis=0, keepdims=True))
    soft = ex / jnp.sum(ex, axis=0, keepdims=True)
    lb = jnp.sum(soft[0:layer + 1], axis=0, keepdims=True) - soft[0:1]

    u = u_ref[...]
    tb = u.shape[0]
    q = _silu(u[:, 0:MIX])
    f = lb + (1.0 - lb) / (1.0 + jnp.exp(-u[:, MIX:2 * MIX]))
    logf = jnp.log(jnp.maximum(f, GATE_FLOOR))
    k = 1.0 - f
    v = u[:, 2 * MIX:3 * MIX]
    og = u[:, 3 * MIX:4 * MIX]

    C = HGRN_C
    SUB = 8
    ones_bd = ones_ref[...]
    ti = lax.broadcasted_iota(jnp.int32, (tb, tb), 0)
    si = lax.broadcasted_iota(jnp.int32, (tb, tb), 1)
    same_chunk = (ti // C) == (si // C)
    b_all = _split_dot((same_chunk & (ti >= si)).astype(BF16), logf, 3)
    b_last_all = _split_dot(same_chunk.astype(BF16), logf, 3)
    b2_all = b_all * math.log2(math.e)
    qe_all = q * jnp.exp(b_all)
    kd_all = k * jnp.exp(b_last_all - b_all)
    g_last_all = jnp.exp(b_last_all)
    row8 = lax.broadcasted_iota(jnp.int32, (SUB, 1), 0)
    hr = lax.broadcasted_iota(jnp.int32, (MIX, MIX), 0) // HEAD_DIM
    hc_ = lax.broadcasted_iota(jnp.int32, (MIX, MIX), 1) // HEAD_DIM
    same_head = hr == hc_

    state = state_ref[...]
    o_chunks = []
    yield
    for ch in range(tb // C):
        sl = slice(ch * C, (ch + 1) * C)
        qc, kc, vc, b2 = q[sl], k[sl], v[sl], b2_all[sl]
        pieces = []
        for s in range(C):
            r0 = (s // SUB) * SUB
            pm = qc[r0:] * (kc[s:s + 1, :] * jnp.exp2(b2[r0:] - b2[s:s + 1, :]))
            top = jnp.where(row8 + r0 >= s, pm[0:SUB], 0.0)
            pieces.append(top if C - r0 == SUB else jnp.concatenate([top, pm[SUB:]], axis=0))
        attn = jnp.dot(jnp.concatenate(pieces, axis=0).astype(BF16), ones_bd,
                       preferred_element_type=F32)
        groups = [jnp.zeros((SUB, MIX), F32) for _ in range(C // SUB)]
        off = 0
        for s in range(C):
            g0 = s // SUB
            for g in range(g0, C // SUB):
                groups[g] = groups[g] + attn[off:off + SUB, :] * vc[s:s + 1, :]
                off += SUB
        o = jnp.concatenate(groups, axis=0)
        o_chunks.append(o + _bdot_nt(qe_all[sl], state))
        outer = jnp.dot(vc.T.astype(BF16), kd_all[sl].astype(BF16), preferred_element_type=F32)
        state = state * g_last_all[ch * C:ch * C + 1, :] + jnp.where(same_head, outer, 0.0)
        yield
    state_ref[...] = state
    o = jnp.concatenate(o_chunks, axis=0)
    ms = _head_sum_mxu(o * o, ones_bd) * (1.0 / HEAD_DIM)
    o_ref[...] = o * lax.rsqrt(ms + NORM_EPS) * ng_ref[...] * _silu(og)


def _block_ones():
    hid = jnp.arange(MIX) // HEAD_DIM
    return (hid[:, None] == hid[None, :]).astype(BF16)


def _ret_steps(u_ref, cos_ref, sin_ref, dm_ref, qd_ref, kd_ref, cd_ref, ng_ref, nb_ref,
               ones_ref, o_ref, state_ref):
    c = pl.program_id(1)

    @pl.when(c == 0)
    def _():
        state_ref[...] = jnp.zeros_like(state_ref)

    u = u_ref[...]
    cosf = cos_ref[...]
    sins = sin_ref[...]
    half = HEAD_DIM // 2

    def rope(a):
        outs = []
        for j in range(MIX // 128):
            blk = a[:, j * 128:(j + 1) * 128]
            lane = lax.broadcasted_iota(jnp.int32, blk.shape, 1)
            swapped = jnp.where((lane % HEAD_DIM) < half, pltpu.roll(blk, 128 - half, axis=1),
                                pltpu.roll(blk, half, axis=1))
            outs.append(swapped)
        return a * cosf + jnp.concatenate(outs, axis=-1) * sins

    q = rope(u[:, 0:MIX])
    k = rope(u[:, MIX:2 * MIX]) * (HEAD_DIM ** -0.5)
    v = u[:, 2 * MIX:3 * MIX]
    g = u[:, 3 * MIX:4 * MIX]
    qd = q * qd_ref[...]
    kd = k * kd_ref[...]
    C = u.shape[0]
    yield
    row_head = lax.broadcasted_iota(jnp.int32, (N_HEADS * C, MIX), 0) // C
    lane_head = lax.broadcasted_iota(jnp.int32, (N_HEADS * C, MIX), 1) // HEAD_DIM
    own = row_head == lane_head
    q_stack = jnp.where(own, jnp.concatenate([q] * N_HEADS, axis=0), 0.0)
    s = _bdot_nt(q_stack, k) * dm_ref[...]
    sv = jnp.where(own, _bdot(s, v), 0.0)
    o = sv[0:C]
    for h in range(1, N_HEADS):
        o = o + sv[h * C:(h + 1) * C]
    yield
    state = state_ref[...]
    o = o + _bdot(qd, state)
    kr = lax.broadcasted_iota(jnp.int32, (MIX, MIX), 0) // HEAD_DIM
    kc = lax.broadcasted_iota(jnp.int32, (MIX, MIX), 1) // HEAD_DIM
    state_ref[...] = state * cd_ref[...] + jnp.where(kr == kc, _bdot(kd.T, v), 0.0)
    ones = ones_ref[...]
    mean = _head_sum_mxu(o, ones) * (1.0 / HEAD_DIM)
    oc = o - mean
    var = _head_sum_mxu(oc * oc, ones) * (1.0 / HEAD_DIM)
    y = oc * lax.rsqrt(var + RET_GN_EPS) * ng_ref[...] + nb_ref[...]
    o_ref[...] = y * _silu(g)


def _ret_operands(u, norm_g, norm_b):
    bsz, seq, _ = u.shape
    C = min(REC_TB, seq)
    pos = jnp.arange(seq, dtype=F32)
    inv_freq = RET_ROPE_BASE ** (-jnp.arange(0, HEAD_DIM, 2, dtype=F32) / HEAD_DIM)
    ang = pos[:, None] * inv_freq[None, :]
    cos, sin = jnp.cos(ang), jnp.sin(ang)
    cosf = jnp.tile(jnp.concatenate([cos, cos], axis=-1), (1, N_HEADS))
    sins = jnp.tile(jnp.concatenate([-sin, sin], axis=-1), (1, N_HEADS))
    log_gamma = jnp.log(1.0 - jnp.exp2(-5.0 - jnp.arange(N_HEADS, dtype=F32)))
    i = jnp.arange(C, dtype=F32)
    dpos = i[:, None] - i[None, :]
    dm = jnp.where(dpos >= 0, jnp.exp(jnp.maximum(dpos, 0.0)[None] * log_gamma[:, None, None]), 0.0)
    lanes = lambda t: jnp.repeat(t, HEAD_DIM, axis=-1)
    qd = lanes(jnp.exp((i + 1.0)[:, None] * log_gamma[None, :]))
    kd = lanes(jnp.exp((C - 1.0 - i)[:, None] * log_gamma[None, :]))
    cd = lanes(jnp.exp(C * log_gamma)[None, :])
    args = (u, cosf, sins, dm.reshape(N_HEADS * C, C), qd, kd, cd,
            norm_g.reshape(1, -1).astype(F32), norm_b.reshape(1, -1).astype(F32), _block_ones())
    in_specs = ([_row_spec(C, 4 * MIX),
                 pl.BlockSpec((C, MIX), lambda b, c: (c, 0)),
                 pl.BlockSpec((C, MIX), lambda b, c: (c, 0))]
                + [_const_spec(t.shape) for t in args[3:]])
    return args, in_specs


def _recurrent_kernel(*refs, layer, n_in):
    n_h, n_t, n_r = n_in
    h_in, t_in, r_in = refs[:n_h], refs[n_h:n_h + n_t], refs[n_h + n_t:n_h + n_t + n_r]
    o_h, o_t, o_r, state_h, state_t, state_r, prev_r = refs[n_h + n_t + n_r:]
    streams = [(_rwkv_steps(*r_in, o_r, state_r, prev_r), 1),
               (_hgrn_steps(*h_in, o_h, state_h, layer=layer), RWKV_C // HGRN_C),
               (_ret_steps(*t_in, o_t, state_t), 1)]
    live = True
    while live:
        live = False
        for steps, per_round in streams:
            for _ in range(per_round):
                live = (next(steps, "done") != "done") or live


def _recurrent_call(u_hgrn, lb_logits, hgrn_norm, layer, u_ret, ret_norm_g, ret_norm_b,
                    u_rwkv, *rwkv_params):
    bsz, seq, _ = u_hgrn.shape
    tb = min(REC_TB, seq)
    h_args = (u_hgrn, lb_logits.astype(F32), hgrn_norm.reshape(1, -1).astype(F32), _block_ones())
    t_args, t_specs = _ret_operands(u_ret, ret_norm_g, ret_norm_b)
    vec = lambda t: t.reshape(1, -1).astype(F32)
    mu, w0, w_up, a0, a_up, g_up, k_k, k_a, r_k, norm_g, norm_b = rwkv_params
    r_args = (u_rwkv, vec(mu), vec(w0), w_up, vec(a0), a_up, g_up, vec(k_k), vec(k_a), vec(r_k),
              vec(norm_g), vec(norm_b), _block_ones())
    specs = lambda args: ([_row_spec(tb, 4 * MIX)] + [_const_spec(t.shape) for t in args[1:]])
    out = jax.ShapeDtypeStruct((bsz, seq, MIX), F32)
    state = pltpu.VMEM((MIX, MIX), F32)
    return pl.pallas_call(
        functools.partial(_recurrent_kernel, layer=layer,
                          n_in=(len(h_args), len(t_args), len(r_args))),
        grid=(bsz, seq // tb),
        in_specs=specs(h_args) + t_specs + specs(r_args),
        out_specs=[_row_spec(tb, MIX)] * 3,
        out_shape=[out] * 3,
        scratch_shapes=[state, state, state, pltpu.VMEM((1, 4 * MIX), F32)],
        compiler_params=_params(("parallel", "arbitrary")),
    )(*h_args, *t_args, *r_args)


DENSE_TM = 512
FFN_TF = 256


def _rmsnorm(x, g):
    return x * lax.rsqrt(jnp.mean(x * x, axis=-1, keepdims=True) + NORM_EPS) * g


def _resident(shape):
    nd = len(shape)
    return pl.BlockSpec(shape, lambda b, c: (0,) * nd, pipeline_mode=pl.Buffered(1))


def _inproj_kernel(h_ref, g_ref, w_ref, onehot_ref, q_ref, gate_ref, kc_ref, vc_ref, ks_ref,
                   vs_ref, kw_ref, vw_ref, hg_ref, rt_ref, rw_ref):
    xn = _rmsnorm(h_ref[...], g_ref[...]).astype(BF16)
    nsa = jnp.dot(xn, w_ref[:, 0:NSA_PAD], preferred_element_type=F32)
    tm = nsa.shape[0]
    col = lambda j: nsa[:, MIX + j * HEAD_DIM:MIX + (j + 1) * HEAD_DIM]
    q_ref[...] = nsa[:, 0:MIX]
    kc_ref[...] = col(0)
    vc_ref[...] = col(1)
    gate_ref[...] = nsa[:, MIX + 6 * HEAD_DIM:NSA_PAD]
    ks_ref[...] = jnp.concatenate(
        [onehot_ref[...], col(2).astype(BF16),
         jnp.zeros((tm, ks_ref.shape[1] - NSA_ONEHOT - HEAD_DIM), BF16)], axis=-1)
    kw_ref[...] = col(4).astype(BF16)
    vs_ref[...] = jnp.concatenate([col(3).T, _ones_row_block(tm)], axis=0).astype(BF16)
    vw_t = jnp.concatenate([col(5).T, _ones_row_block(tm)], axis=0).astype(BF16)
    tw = vw_ref.shape[-1]
    for t in range(tm // tw):
        vw_ref[t] = vw_t[:, t * tw:(t + 1) * tw]
    off = NSA_PAD
    for ref in (hg_ref, rt_ref, rw_ref):
        ref[...] = jnp.dot(xn, w_ref[:, off:off + 4 * MIX], preferred_element_type=F32)
        off += 4 * MIX


def _inproj_call(h, g, w_pad):
    bsz, seq, _ = h.shape
    tm = min(NSA_TK, seq)
    tw = min(NSA_TQ, seq)
    local = jnp.arange(tm) // NSA_SEL_BLOCK
    onehot = (local[:, None] == jnp.arange(NSA_ONEHOT)[None, :]).astype(BF16)
    struct = jax.ShapeDtypeStruct
    outs = [
        (struct((bsz, seq, MIX), F32), _row_spec(tm, MIX)),
        (struct((bsz, seq, 128), F32), _row_spec(tm, 128)),
        (struct((bsz, seq, HEAD_DIM), F32), _row_spec(tm, HEAD_DIM)),
        (struct((bsz, seq, HEAD_DIM), F32), _row_spec(tm, HEAD_DIM)),
        (struct((bsz, seq, 128), BF16), _row_spec(tm, 128)),
        (struct((bsz, seq // tm, NSA_VROWS, tm), BF16),
         pl.BlockSpec((None, None, NSA_VROWS, tm), lambda b, c: (b, c, 0, 0))),
        (struct((bsz, seq, HEAD_DIM), BF16), _row_spec(tm, HEAD_DIM)),
        (struct((bsz, seq // tw, NSA_VROWS, tw), BF16),
         pl.BlockSpec((None, tm // tw, NSA_VROWS, tw), lambda b, c: (b, c, 0, 0))),
    ] + [(struct((bsz, seq, 4 * MIX), F32), _row_spec(tm, 4 * MIX))] * 3
    res = pl.pallas_call(
        _inproj_kernel,
        grid=(bsz, seq // tm),
        in_specs=[_row_spec(tm, D_MODEL), _resident((1, D_MODEL)), _resident(w_pad.shape),
                  _resident(onehot.shape)],
        out_specs=[spec for _, spec in outs],
        out_shape=[shape for shape, _ in outs],
        compiler_params=_params(("parallel", "parallel")),
    )(h, g.reshape(1, -1), w_pad, onehot)
    return tuple(res[:8]), res[8], res[9], res[10]


def _merge_kernel(h_ref, b0_ref, b1_ref, b2_ref, b3_ref, g_ref, wg_ref, bg_ref, wb_ref, wo_ref,
                  o_ref):
    h = h_ref[...]
    xn = _rmsnorm(h, g_ref[...]).astype(BF16)
    merged = None
    for m, b_ref in enumerate((b0_ref, b1_ref, b2_ref, b3_ref)):
        gate = _sigmoid(jnp.dot(xn, wg_ref[m], preferred_element_type=F32) + bg_ref[m])
        term = gate * _bdot(b_ref[...], wb_ref[m])
        merged = term if merged is None else merged + term
    o_ref[...] = h + _bdot(merged, wo_ref[...])


def _merge_call(h, branches, g, w_gate, b_gate, w_branch, w_out):
    bsz, seq, _ = h.shape
    tm = min(DENSE_TM, seq)
    consts = (g.reshape(1, -1), w_gate, b_gate.reshape(4, 1, D_MODEL), w_branch, w_out)
    return pl.pallas_call(
        _merge_kernel,
        grid=(bsz, seq // tm),
        in_specs=([_row_spec(tm, D_MODEL)] + [_row_spec(tm, MIX)] * 4
                  + [_resident(t.shape) for t in consts]),
        out_specs=_row_spec(tm, D_MODEL),
        out_shape=jax.ShapeDtypeStruct(h.shape, F32),
        compiler_params=_params(("parallel", "parallel")),
    )(h, *branches, *consts)


def _ffn_kernel(h_ref, g_ref, wg_ref, wu_ref, wd_ref, o_ref):
    h = h_ref[...]
    hn = _rmsnorm(h, g_ref[...]).astype(BF16)
    acc = h
    for f in range(0, D_FF, FFN_TF):
        gate = jnp.dot(hn, wg_ref[:, f:f + FFN_TF], preferred_element_type=F32)
        up = jnp.dot(hn, wu_ref[:, f:f + FFN_TF], preferred_element_type=F32)
        acc = acc + _bdot(_silu(gate) * up, wd_ref[f:f + FFN_TF, :])
    o_ref[...] = acc


def _ffn_call(h, g, w_gate, w_up, w_down):
    bsz, seq, _ = h.shape
    tm = min(DENSE_TM, seq)
    consts = (g.reshape(1, -1), w_gate, w_up, w_down)
    return pl.pallas_call(
        _ffn_kernel,
        grid=(bsz, seq // tm),
        in_specs=[_row_spec(tm, D_MODEL)] + [_resident(t.shape) for t in consts],
        out_specs=_row_spec(tm, D_MODEL),
        out_shape=jax.ShapeDtypeStruct(h.shape, F32),
        compiler_params=_params(("parallel", "parallel")),
    )(h, *consts)


def _ple_kernel(h_ref, p_ref, g_ref, wg_ref, wp_ref, gf_ref, o_ref, *, final_norm):
    h = h_ref[...]
    hp = _rmsnorm(h, g_ref[...])
    out = h + _sigmoid(_bdot(hp, wg_ref[...])) * _bdot(p_ref[...], wp_ref[...])
    if final_norm:
        out = _rmsnorm(out, gf_ref[...])
    o_ref[...] = out


def _ple_call(h, p, g, w_gate, w_proj, g_final, final_norm):
    bsz, seq, _ = h.shape
    tm = min(DENSE_TM, seq)
    consts = (g.reshape(1, -1), w_gate, w_proj, g_final.reshape(1, -1))
    return pl.pallas_call(
        functools.partial(_ple_kernel, final_norm=final_norm),
        grid=(bsz, seq // tm),
        in_specs=([_row_spec(tm, D_MODEL), _row_spec(tm, PLE_DIM)]
                  + [_resident(t.shape) for t in consts]),
        out_specs=_row_spec(tm, D_MODEL),
        out_shape=jax.ShapeDtypeStruct(h.shape, F32),
        compiler_params=_params(("parallel", "parallel")),
    )(h, p, *consts)


def kernel(x, p, norm_mix, w_in, nsa_pos_k, nsa_pos_v, nsa_cmp_k1, nsa_cmp_k2, nsa_cmp_v1,
           nsa_cmp_v2, hgrn_lb_logits, hgrn_norm, ret_norm_g, ret_norm_b, rwkv_mu, rwkv_w0,
           rwkv_w_up, rwkv_a0, rwkv_a_up, rwkv_g_up, rwkv_k_k, rwkv_k_a, rwkv_r_k, rwkv_norm_g,
           rwkv_norm_b, w_branch, w_gate, b_gate, w_out, norm_ffn, w_ffn_gate, w_ffn_up,
           w_ffn_down, norm_ple, w_ple_gate, w_ple_proj, norm_final):
    depth = w_in.shape[0]
    w_in_pad = jnp.concatenate(
        [w_in[:, :, :NSA_WIDTH], jnp.zeros((depth, D_MODEL, NSA_PAD - NSA_WIDTH), w_in.dtype),
         w_in[:, :, NSA_WIDTH:]], axis=-1).astype(BF16)
    bf = lambda t: t.astype(BF16)
    h = x
    for i in range(depth):
        nsa_in, u_hgrn, u_ret, u_rwkv = _inproj_call(h, norm_mix[i], w_in_pad[i])
        o_hgrn, o_ret, o_rwkv = _recurrent_call(
            u_hgrn, hgrn_lb_logits, hgrn_norm[i], i, u_ret, ret_norm_g[i], ret_norm_b[i],
            u_rwkv, rwkv_mu[i], rwkv_w0[i], rwkv_w_up[i], rwkv_a0[i], rwkv_a_up[i], rwkv_g_up[i],
            rwkv_k_k[i], rwkv_k_a[i], rwkv_r_k[i], rwkv_norm_g[i], rwkv_norm_b[i])
        branches = (
            _nsa_call(*nsa_in, nsa_pos_k[i], nsa_pos_v[i], nsa_cmp_k1[i], nsa_cmp_k2[i],
                      nsa_cmp_v1[i], nsa_cmp_v2[i]),
            o_hgrn, o_ret, o_rwkv,
        )
        h = _merge_call(h, branches, norm_mix[i], bf(w_gate[i]), b_gate[i], bf(w_branch[i]),
                        bf(w_out[i]))
        h = _ffn_call(h, norm_ffn[i], bf(w_ffn_gate[i]), bf(w_ffn_up[i]), bf(w_ffn_down[i]))
        h = _ple_call(h, p[i], norm_ple[i], bf(w_ple_gate[i]), bf(w_ple_proj[i]), norm_final,
                      final_norm=(i == depth - 1))
    return h
```

```python
import functools
import math

import jax
import jax.numpy as jnp
from jax import lax
from jax.experimental import pallas as pl
from jax.experimental.pallas import tpu as pltpu

F32 = jnp.float32
BF16 = jnp.bfloat16

D_MODEL = 1024
N_HEADS = 4
HEAD_DIM = 64
MIX = N_HEADS * HEAD_DIM
D_FF = 2816
PLE_DIM = 256
NORM_EPS = 1e-6
NEG_BIG = -1e30
POS_BIG = 1e30
GATE_FLOOR = 1e-20

NSA_CMP_BLOCK = 32
NSA_CMP_STRIDE = 16
NSA_SEL_BLOCK = 64
NSA_TOP_N = 16
NSA_WINDOW = 512
NSA_CMP_HIDDEN = 128
NSA_WIDTH = 652
NSA_PAD = 768

RET_ROPE_BASE = 10000.0
RET_GN_EPS = 1e-5
RWKV_GN_EPS = 64e-5

VMEM_LIMIT = 56 * 1024 * 1024


def _bdot(a, b):
    return jnp.dot(a.astype(BF16), b.astype(BF16), preferred_element_type=F32)


def _bdot_nt(a, b):
    return lax.dot_general(a.astype(BF16), b.astype(BF16), (((1,), (1,)), ((), ())),
                           preferred_element_type=F32)


def _sigmoid(x):
    return 0.5 * jnp.tanh(0.5 * x) + 0.5


def _silu(x):
    return x * _sigmoid(x)


def _params(sem):
    return pltpu.CompilerParams(dimension_semantics=sem, vmem_limit_bytes=VMEM_LIMIT)


def _row_spec(tile, width):
    return pl.BlockSpec((None, tile, width), lambda b, c: (b, c, 0))


def _const_spec(shape):
    nd = len(shape)
    return pl.BlockSpec(shape, lambda b, c: (0,) * nd)


REC_TB = 256
RWKV_C = 64


def _split_dot(a_bf16, x, parts):
    total = None
    rest = x
    for _ in range(parts):
        piece = rest.astype(BF16)
        rest = rest - piece.astype(F32)
        term = jnp.dot(a_bf16, piece, preferred_element_type=F32)
        total = term if total is None else total + term
    return total


def _head_sum_mxu(x, ones_t):
    hi = x.astype(BF16)
    lo = (x - hi.astype(F32)).astype(BF16)
    return (jnp.dot(hi, ones_t, preferred_element_type=F32)
            + jnp.dot(lo, ones_t, preferred_element_type=F32))


def _rwkv_steps(u_ref, mu_ref, w0_ref, wup_ref, a0_ref, aup_ref, gup_ref, kk_ref, ka_ref,
                rk_ref, ng_ref, nb_ref, ones_ref, o_ref, state_ref, prev_ref):
    c = pl.program_id(1)
    ones = ones_ref[...]

    @pl.when(c == 0)
    def _():
        state_ref[...] = jnp.zeros_like(state_ref)
        prev_ref[...] = jnp.zeros_like(prev_ref)

    u = u_ref[...]
    tb = u.shape[0]
    row = lax.broadcasted_iota(jnp.int32, u.shape, 0)
    u_prev = jnp.where(row == 0, prev_ref[...], pltpu.roll(u, 1, axis=0))
    prev_ref[...] = u[tb - 1:tb, :]
    xs = u + mu_ref[...] * (u_prev - u)
    r = xs[:, 0:MIX]
    k = xs[:, MIX:2 * MIX]
    v = xs[:, 2 * MIX:3 * MIX]
    w_lo = xs[:, 3 * MIX:3 * MIX + 64]
    a_lo = xs[:, 3 * MIX + 64:3 * MIX + 128]
    g_lo = xs[:, 3 * MIX + 128:3 * MIX + 256]

    logw = -math.exp(-0.5) * _sigmoid(w0_ref[...] + _bdot(jnp.tanh(w_lo), wup_ref[...]))
    a = _sigmoid(a0_ref[...] + _bdot(a_lo, aup_ref[...]))
    g = _bdot(_sigmoid(g_lo), gup_ref[...])
    kk = k * kk_ref[...]
    kk = kk * lax.rsqrt(jnp.maximum(_head_sum_mxu(kk * kk, ones), 1e-24))
    k2 = k * (1.0 + (a - 1.0) * ka_ref[...])
    alpha = -kk
    beta = kk * a
    bonus = _head_sum_mxu(r * k2 * rk_ref[...], ones) * v

    C = RWKV_C
    ti = lax.broadcasted_iota(jnp.int32, (tb, tb), 0)
    si = lax.broadcasted_iota(jnp.int32, (tb, tb), 1)
    same_chunk = (ti // C) == (si // C)
    prefix = (same_chunk & (ti >= si)).astype(BF16)
    cum = _split_dot(prefix, logw, 3)
    cum_last = _split_dot(same_chunk.astype(BF16), logw, 3)
    gam_all = jnp.exp(cum_last)
    e_inv = jnp.exp(-cum)
    e_last = jnp.exp(cum_last - cum)
    ag_all = alpha * jnp.exp(cum - logw)
    rg_all = r * jnp.exp(cum)
    bi_all = beta * e_inv
    ki_all = k2 * e_inv
    bl_all = beta * e_last
    kl_all = k2 * e_last

    hc = N_HEADS * C
    row_head = lax.broadcasted_iota(jnp.int32, (hc, MIX), 0) // C
    lane_head = lax.broadcasted_iota(jnp.int32, (hc, MIX), 1) // HEAD_DIM
    own = row_head == lane_head

    def stack(x):
        return jnp.where(own, jnp.concatenate([x] * N_HEADS, axis=0), 0.0).astype(BF16)

    rr = lax.broadcasted_iota(jnp.int32, (hc, hc), 0)
    cc = lax.broadcasted_iota(jnp.int32, (hc, hc), 1)
    strict = rr > cc
    incl = rr >= cc
    eye_hc = (rr == cc).astype(F32)
    kr = lax.broadcasted_iota(jnp.int32, (MIX, MIX), 0)
    kc_ = lax.broadcasted_iota(jnp.int32, (MIX, MIX), 1)
    eye_k = kr == kc_

    state = state_ref[...]
    y_chunks = []
    yield
    for ch in range(tb // C):
        sl = slice(ch * C, (ch + 1) * C)
        ag, rg, bi, ki, bl, kl, vm = (stack(t[sl]) for t in (ag_all, rg_all, bi_all, ki_all,
                                                               bl_all, kl_all, v))
        aa = _bdot_nt(jnp.concatenate([ag, rg], axis=0), jnp.concatenate([bi, ki], axis=0))
        a_ab = jnp.where(strict, aa[:hc, :hc], 0.0)
        a_ak = jnp.where(strict, aa[:hc, hc:], 0.0)
        a_rb = jnp.where(incl, aa[hc:, :hc], 0.0)
        a_rk = jnp.where(incl, aa[hc:, hc:], 0.0)
        t_inv = eye_hc + a_ab
        pw = _bdot(a_ab, a_ab)
        for _ in range(int(math.log2(C)) - 2):
            both = _bdot(pw, jnp.concatenate([t_inv, pw], axis=-1))
            t_inv = t_inv + both[:, :hc]
            pw = both[:, hc:]
        t_inv = t_inv + _bdot(pw, t_inv)
        w12 = _bdot(t_inv, jnp.concatenate([ag.astype(F32), _bdot(a_ak, vm)], axis=-1))
        ry = _bdot(a_rb, w12)
        rq = rg.astype(F32) + ry[:, :MIX]
        y0 = ry[:, MIX:] + _bdot(a_rk, vm)
        mn = _bdot(bl.T, w12)
        m_mat = jnp.where(eye_k, gam_all[ch * C:ch * C + 1, :], 0.0) + mn[:, :MIX]
        n_mat = mn[:, MIX:] + _bdot(kl.T, vm)
        prod = _bdot(jnp.concatenate([rq, m_mat], axis=0), state)
        ym = prod[:hc] + y0
        y_chunks.append(ym[0:C] + ym[C:2 * C] + ym[2 * C:3 * C] + ym[3 * C:4 * C])
        state = prod[hc:] + n_mat
        yield
    state_ref[...] = state
    y = jnp.concatenate(y_chunks, axis=0)

    mean = _head_sum_mxu(y, ones) * (1.0 / HEAD_DIM)
    yc = y - mean
    var = _head_sum_mxu(yc * yc, ones) * (1.0 / HEAD_DIM)
    yn = yc * lax.rsqrt(var + RWKV_GN_EPS) * ng_ref[...] + nb_ref[...]
    o_ref[...] = (yn + bonus) * g


NSA_TQ = 256
NSA_TK = 512
NSA_VROWS = 80
NSA_UNROLL = 2
NSA_CMP_PARTS = 4
NSA_ONEHOT = 16
GROUP = NSA_CMP_STRIDE


def _ones_row_block(width):
    row = lax.broadcasted_iota(jnp.int32, (NSA_VROWS - HEAD_DIM, width), 0)
    return (row == 0).astype(F32)


def _nsa_compress_kernel(xk_ref, xv_ref, pk_ref, pv_ref, k1_ref, k2_ref, v1_ref, v2_ref, ov_ref,
                         kc_ref, vc_ref):
    ng = kc_ref.shape[0]

    def compress(x_ref, pos_ref, w1_ref, w2_ref):
        first = jnp.zeros((ng, k2_ref.shape[0]), F32)
        second = jnp.zeros((ng, k2_ref.shape[0]), F32)
        for j in range(GROUP):
            xj = x_ref[pl.ds(j, ng, stride=GROUP), :]
            lo, hi = j * HEAD_DIM, (GROUP + j) * HEAD_DIM
            first = first + _bdot(xj + pos_ref[j:j + 1, :], w1_ref[lo:lo + HEAD_DIM, :])
            second = second + _bdot(xj + pos_ref[GROUP + j:GROUP + j + 1, :],
                                    w1_ref[hi:hi + HEAD_DIM, :])
        hid = first + pltpu.roll(second, ng - 1, axis=0)
        return _bdot(_silu(hid), w2_ref[...])

    kc = compress(xk_ref, pk_ref, k1_ref, k2_ref)
    vc = compress(xv_ref, pv_ref, v1_ref, v2_ref)
    kc_ref[...] = kc.astype(BF16)
    ng = vc.shape[0]
    vc_ref[0:NSA_VROWS, :] = jnp.concatenate([vc.T, _ones_row_block(ng)], axis=0).astype(BF16)
    vc_ref[NSA_VROWS:, :] = ov_ref[...]


def _nsa_compress_call(xk, xv, pos_k, pos_v, k1, k2, v1, v2, overlap_t):
    bsz, seq, width = xk.shape
    ng = seq // GROUP
    n_sel = overlap_t.shape[0]
    args = (xk, xv, pos_k, pos_v, k1.astype(BF16), k2.astype(BF16),
            v1.astype(BF16), v2.astype(BF16), overlap_t)
    blk = pl.BlockSpec((None, seq, width), lambda b: (b, 0, 0))
    const = lambda t: pl.BlockSpec(t.shape, lambda b: (0,) * t.ndim)
    return pl.pallas_call(
        _nsa_compress_kernel,
        grid=(bsz,),
        in_specs=[blk, blk] + [const(t) for t in args[2:]],
        out_specs=[pl.BlockSpec((None, ng, HEAD_DIM), lambda b: (b, 0, 0)),
                   pl.BlockSpec((None, NSA_VROWS + n_sel, ng), lambda b: (b, 0, 0))],
        out_shape=[jax.ShapeDtypeStruct((bsz, ng, HEAD_DIM), BF16),
                   jax.ShapeDtypeStruct((bsz, NSA_VROWS + n_sel, ng), BF16)],
        compiler_params=_params(("parallel",)),
    )(*args)


REMOVED = -3e38


def _nsa_kernel(q_ref, g_ref, ks_ref, vs_ref, kw_ref, vw_ref, kc_ref, vc_ref, o_ref,
                selbias_ref, *, n_top):
    c = pl.program_id(1)
    tq = q_ref.shape[0]
    tk = ks_ref.shape[1]
    ct = kc_ref.shape[0]
    ns = selbias_ref.shape[0]
    t0 = c * tq
    cols = N_HEADS * tq

    q_t = (q_ref[...] * (HEAD_DIM ** -0.5)).T
    qs = jnp.concatenate([q_t[h * HEAD_DIM:(h + 1) * HEAD_DIM] for h in range(N_HEADS)],
                         axis=1).astype(BF16)
    t_q = t0 + lax.broadcasted_iota(jnp.int32, (1, tq), 1)
    t_col = jnp.concatenate([t_q] * N_HEADS, axis=1)

    def online(carry, s, v_aug):
        m, acc = carry
        m_new = jnp.maximum(m, jnp.max(s, axis=0, keepdims=True))
        p = jnp.exp(s - m_new)
        acc = jnp.exp(m - m_new) * acc + jnp.dot(v_aug, p.astype(BF16), preferred_element_type=F32)
        return m_new, acc

    def normalise(acc):
        return acc[0:HEAD_DIM] / acc[HEAD_DIM:HEAD_DIM + 1]

    init_aug = (jnp.full((1, cols), NEG_BIG, F32), jnp.zeros((vs_ref.shape[1], cols), F32))

    vrows = vs_ref.shape[1]

    def cmp_quarters(quarters):
        rows = quarters * (ct // NSA_CMP_PARTS)
        n_row = lax.broadcasted_iota(jnp.int32, (rows, 1), 0)
        valid = (n_row * NSA_CMP_STRIDE + (NSA_CMP_BLOCK - 1)) <= t_col
        s = jnp.where(valid, jnp.dot(kc_ref[0:rows, :], qs, preferred_element_type=F32), NEG_BIG)
        m = jnp.max(s, axis=0, keepdims=True)
        p = jnp.exp(s - m)
        p_hi = p.astype(BF16)
        p_lo = (p - p_hi.astype(F32)).astype(BF16)
        res = jnp.dot(vc_ref[:, 0:rows], p_hi, preferred_element_type=F32)
        imp_lo = jnp.dot(vc_ref[vrows:, 0:rows], p_lo, preferred_element_type=F32)
        return m, jnp.concatenate([res[:vrows], res[vrows:] + imp_lo], axis=0)

    last_valid = (t0 + tq - NSA_CMP_BLOCK) // NSA_CMP_STRIDE
    quarter = jnp.clip(last_valid // (ct // NSA_CMP_PARTS), 0, NSA_CMP_PARTS - 1)
    m_c, res_c = lax.switch(quarter, [functools.partial(cmp_quarters, i + 1)
                                      for i in range(NSA_CMP_PARTS)])
    inv_c = jnp.where(m_c > 0.5 * NEG_BIG, 1.0 / res_c[HEAD_DIM:HEAD_DIM + 1], 0.0)
    o_cmp = res_c[0:HEAD_DIM] * inv_c
    imp4 = res_c[vrows:] * inv_c
    imp = imp4[:, 0:tq]
    for h in range(1, N_HEADS):
        imp = imp + imp4[:, h * tq:(h + 1) * tq]

    blk = lax.broadcasted_iota(jnp.int32, (ns, 1), 0)
    blk_f = blk.astype(F32)
    cur = t_q // NSA_SEL_BLOCK
    forced = (blk == 0) | (blk == cur) | (blk == cur - 1)
    score = jnp.where(forced, POS_BIG, jnp.where(blk <= cur, imp, NEG_BIG))
    chosen = jnp.zeros((ns, tq), jnp.bool_)
    for _ in range(n_top):
        best = jnp.max(score, axis=0, keepdims=True)
        first = jnp.min(jnp.where(score == best, blk_f, float(ns)), axis=0, keepdims=True)
        hit = blk_f == first
        chosen = chosen | hit
        score = jnp.where(hit, REMOVED, score)
    selbias_ref[...] = jnp.where(chosen, 0.0, NEG_BIG)

    per_tile = tk // NSA_SEL_BLOCK
    key_row = lax.broadcasted_iota(jnp.int32, (tk, 1), 0)

    pad_rows = jnp.zeros((ks_ref.shape[2] - HEAD_DIM - NSA_ONEHOT, cols), BF16)
    bias_pad = jnp.zeros((NSA_ONEHOT - per_tile, tq), F32)

    def sel_scores(j, live=None):
        start = pl.multiple_of(j * per_tile, per_tile)
        bias = selbias_ref[pl.ds(start, per_tile), :]
        if live is not None:
            bias = jnp.where(live, bias, NEG_BIG)
        bias = jnp.concatenate([bias, bias_pad], axis=0)
        bias = jnp.concatenate([bias.astype(BF16)] * N_HEADS, axis=1)
        rhs = jnp.concatenate([bias, qs, pad_rows], axis=0)
        return jnp.dot(ks_ref[j], rhs, preferred_element_type=F32)

    j_last = t0 // tk

    def sel_group(i, carry):
        tiles = []
        for u in range(NSA_UNROLL):
            j = i * NSA_UNROLL + u
            tiles.append((jnp.minimum(j, j_last - 1), None if u == 0 else j < j_last))
        scores = [sel_scores(j, live) for j, live in tiles]
        for (j, _), s in zip(tiles, scores):
            carry = online(carry, s, vs_ref[j])
        return carry

    carry = lax.fori_loop(0, (j_last + NSA_UNROLL - 1) // NSA_UNROLL, sel_group, init_aug)
    causal = jnp.where((j_last * tk + key_row) <= t_q, 0.0, NEG_BIG)
    _, acc_s = online(carry, sel_scores(j_last) + jnp.concatenate([causal] * N_HEADS, axis=1),
                      vs_ref[j_last])
    o_sel = normalise(acc_s)

    tw = kw_ref.shape[1]
    n_wt = (NSA_WINDOW + tq) // tw
    jw = (t0 - NSA_WINDOW) // tw
    wkey_row = lax.broadcasted_iota(jnp.int32, (tw, 1), 0)
    s_parts = []
    w_tiles = []
    for i in range(n_wt):
        exists = (jw + i) >= 0
        w_tiles.append(jnp.maximum(jw + i, 0))
        dist = t_q - ((jw + i) * tw + wkey_row)
        s = jnp.dot(kw_ref[w_tiles[i]], qs, preferred_element_type=F32)
        if i == 0:
            bias = jnp.where((dist < NSA_WINDOW) & exists, 0.0, NEG_BIG)
            s = s + jnp.concatenate([bias] * N_HEADS, axis=1)
        elif i == n_wt - 1:
            bias = jnp.where(dist >= 0, 0.0, NEG_BIG)
            s = s + jnp.concatenate([bias] * N_HEADS, axis=1)
        else:
            s = s + jnp.where(exists, 0.0, NEG_BIG)
        s_parts.append(s)
    m_w = s_parts[0].max(axis=0, keepdims=True)
    for sp in s_parts[1:]:
        m_w = jnp.maximum(m_w, sp.max(axis=0, keepdims=True))
    acc_w = jnp.zeros((vw_ref.shape[1], cols), F32)
    for i, sp in enumerate(s_parts):
        p = jnp.exp(sp - m_w)
        acc_w = acc_w + jnp.dot(vw_ref[w_tiles[i]], p.astype(BF16), preferred_element_type=F32)
    o_win = normalise(acc_w)

    gates = _sigmoid(g_ref[...]).T

    def gate_row(branch):
        return jnp.concatenate([gates[branch * N_HEADS + h:branch * N_HEADS + h + 1, :]
                                for h in range(N_HEADS)], axis=1)

    out = gate_row(0) * o_cmp + gate_row(1) * o_sel + gate_row(2) * o_win
    o_ref[...] = jnp.concatenate([out[:, h * tq:(h + 1) * tq].T for h in range(N_HEADS)], axis=-1)


def _nsa_call(q, gates, k_cmp, v_cmp, k_sel, v_sel, k_win, v_win, pos_k, pos_v, k1, k2, v1, v2):
    bsz, seq, _ = q.shape
    ng = seq // GROUP
    n_sel = seq // NSA_SEL_BLOCK
    n_top = min(NSA_TOP_N, n_sel)
    cmp_start = jnp.arange(ng) * NSA_CMP_STRIDE
    sel_start = jnp.arange(n_sel) * NSA_SEL_BLOCK
    overlap = ((cmp_start[:, None] < sel_start[None, :] + NSA_SEL_BLOCK)
               & (cmp_start[:, None] + NSA_CMP_BLOCK > sel_start[None, :])).astype(BF16)
    kc, vc_stack = _nsa_compress_call(k_cmp, v_cmp, pos_k, pos_v, k1, k2, v1, v2, overlap.T)
    tq = min(NSA_TQ, seq)
    tk, tw = v_sel.shape[-1], v_win.shape[-1]
    operands = (q, gates,
                k_sel.reshape(bsz, seq // tk, tk, k_sel.shape[-1]), v_sel,
                k_win.reshape(bsz, seq // tw, tw, HEAD_DIM), v_win,
                kc, vc_stack)
    per_batch = lambda t: pl.BlockSpec((None,) + t.shape[1:],
                                       lambda b, c: (b,) + (0,) * (t.ndim - 1))
    return pl.pallas_call(
        functools.partial(_nsa_kernel, n_top=n_top),
        grid=(bsz, seq // tq),
        in_specs=[_row_spec(tq, MIX), _row_spec(tq, 128)] + [per_batch(t) for t in operands[2:]],
        out_specs=_row_spec(tq, MIX),
        out_shape=jax.ShapeDtypeStruct((bsz, seq, MIX), F32),
        scratch_shapes=[pltpu.VMEM((n_sel, tq), F32)],
        compiler_params=_params(("parallel", "arbitrary")),
    )(*operands)


HGRN_C = 32


def _hgrn_steps(u_ref, lbl_ref, ng_ref, ones_ref, o_ref, state_ref, *, layer):
    c = pl.program_id(1)

    @pl.when(c == 0)
    def _():
        state_ref[...] = jnp.zeros_like(state_ref)

    logits = lbl_ref[...]
    ex = jnp.exp(logits - jnp.max(logits, axis=0, keepdims=True))
    soft = ex / jnp.sum(ex, axis=0, keepdims=True)
    lb = jnp.sum(soft[0:layer + 1], axis=0, keepdims=True) - soft[0:1]

    u = u_ref[...]
    tb = u.shape[0]
    q = _silu(u[:, 0:MIX])
    f = lb + (1.0 - lb) / (1.0 + jnp.exp(-u[:, MIX:2 * MIX]))
    logf = jnp.log(jnp.maximum(f, GATE_FLOOR))
    k = 1.0 - f
    v = u[:, 2 * MIX:3 * MIX]
    og = u[:, 3 * MIX:4 * MIX]

    C = HGRN_C
    SUB = 8
    ones_bd = ones_ref[...]
    ti = lax.broadcasted_iota(jnp.int32, (tb, tb), 0)
    si = lax.broadcasted_iota(jnp.int32, (tb, tb), 1)
    same_chunk = (ti // C) == (si // C)
    b_all = _split_dot((same_chunk & (ti >= si)).astype(BF16), logf, 3)
    b_last_all = _split_dot(same_chunk.astype(BF16), logf, 3)
    b2_all = b_all * math.log2(math.e)
    qe_all = q * jnp.exp(b_all)
    kd_all = k * jnp.exp(b_last_all - b_all)
    g_last_all = jnp.exp(b_last_all)
    row8 = lax.broadcasted_iota(jnp.int32, (SUB, 1), 0)
    hr = lax.broadcasted_iota(jnp.int32, (MIX, MIX), 0) // HEAD_DIM
    hc_ = lax.broadcasted_iota(jnp.int32, (MIX, MIX), 1) // HEAD_DIM
    same_head = hr == hc_

    state = state_ref[...]
    o_chunks = []
    yield
    for ch in range(tb // C):
        sl = slice(ch * C, (ch + 1) * C)
        qc, kc, vc, b2 = q[sl], k[sl], v[sl], b2_all[sl]
        pieces = []
        for s in range(C):
            r0 = (s // SUB) * SUB
            pm = qc[r0:] * (kc[s:s + 1, :] * jnp.exp2(b2[r0:] - b2[s:s + 1, :]))
            top = jnp.where(row8 + r0 >= s, pm[0:SUB], 0.0)
            pieces.append(top if C - r0 == SUB else jnp.concatenate([top, pm[SUB:]], axis=0))
        attn = jnp.dot(jnp.concatenate(pieces, axis=0).astype(BF16), ones_bd,
                       preferred_element_type=F32)
        groups = [jnp.zeros((SUB, MIX), F32) for _ in range(C // SUB)]
        off = 0
        for s in range(C):
            g0 = s // SUB
            for g in range(g0, C // SUB):
                groups[g] = groups[g] + attn[off:off + SUB, :] * vc[s:s + 1, :]
                off += SUB
        o = jnp.concatenate(groups, axis=0)
        o_chunks.append(o + _bdot_nt(qe_all[sl], state))
        outer = jnp.dot(vc.T.astype(BF16), kd_all[sl].astype(BF16), preferred_element_type=F32)
        state = state * g_last_all[ch * C:ch * C + 1, :] + jnp.where(same_head, outer, 0.0)
        yield
    state_ref[...] = state
    o = jnp.concatenate(o_chunks, axis=0)
    ms = _head_sum_mxu(o * o, ones_bd) * (1.0 / HEAD_DIM)
    o_ref[...] = o * lax.rsqrt(ms + NORM_EPS) * ng_ref[...] * _silu(og)


def _block_ones():
    hid = jnp.arange(MIX) // HEAD_DIM
    return (hid[:, None] == hid[None, :]).astype(BF16)


def _ret_steps(u_ref, cos_ref, sin_ref, dm_ref, qd_ref, kd_ref, cd_ref, ng_ref, nb_ref,
               ones_ref, o_ref, state_ref):
    c = pl.program_id(1)

    @pl.when(c == 0)
    def _():
        state_ref[...] = jnp.zeros_like(state_ref)

    u = u_ref[...]
    cosf = cos_ref[...]
    sins = sin_ref[...]
    half = HEAD_DIM // 2

    def rope(a):
        outs = []
        for j in range(MIX // 128):
            blk = a[:, j * 128:(j + 1) * 128]
            lane = lax.broadcasted_iota(jnp.int32, blk.shape, 1)
            swapped = jnp.where((lane % HEAD_DIM) < half, pltpu.roll(blk, 128 - half, axis=1),
                                pltpu.roll(blk, half, axis=1))
            outs.append(swapped)
        return a * cosf + jnp.concatenate(outs, axis=-1) * sins

    q = rope(u[:, 0:MIX])
    k = rope(u[:, MIX:2 * MIX]) * (HEAD_DIM ** -0.5)
    v = u[:, 2 * MIX:3 * MIX]
    g = u[:, 3 * MIX:4 * MIX]
    qd = q * qd_ref[...]
    kd = k * kd_ref[...]
    C = u.shape[0]
    yield
    row_head = lax.broadcasted_iota(jnp.int32, (N_HEADS * C, MIX), 0) // C
    lane_head = lax.broadcasted_iota(jnp.int32, (N_HEADS * C, MIX), 1) // HEAD_DIM
    own = row_head == lane_head
    q_stack = jnp.where(own, jnp.concatenate([q] * N_HEADS, axis=0), 0.0)
    s = _bdot_nt(q_stack, k) * dm_ref[...]
    sv = jnp.where(own, _bdot(s, v), 0.0)
    o = sv[0:C]
    for h in range(1, N_HEADS):
        o = o + sv[h * C:(h + 1) * C]
    yield
    state = state_ref[...]
    o = o + _bdot(qd, state)
    kr = lax.broadcasted_iota(jnp.int32, (MIX, MIX), 0) // HEAD_DIM
    kc = lax.broadcasted_iota(jnp.int32, (MIX, MIX), 1) // HEAD_DIM
    state_ref[...] = state * cd_ref[...] + jnp.where(kr == kc, _bdot(kd.T, v), 0.0)
    ones = ones_ref[...]
    mean = _head_sum_mxu(o, ones) * (1.0 / HEAD_DIM)
    oc = o - mean
    var = _head_sum_mxu(oc * oc, ones) * (1.0 / HEAD_DIM)
    y = oc * lax.rsqrt(var + RET_GN_EPS) * ng_ref[...] + nb_ref[...]
    o_ref[...] = y * _silu(g)


def _ret_operands(u, norm_g, norm_b):
    bsz, seq, _ = u.shape
    C = min(REC_TB, seq)
    pos = jnp.arange(seq, dtype=F32)
    inv_freq = RET_ROPE_BASE ** (-jnp.arange(0, HEAD_DIM, 2, dtype=F32) / HEAD_DIM)
    ang = pos[:, None] * inv_freq[None, :]
    cos, sin = jnp.cos(ang), jnp.sin(ang)
    cosf = jnp.tile(jnp.concatenate([cos, cos], axis=-1), (1, N_HEADS))
    sins = jnp.tile(jnp.concatenate([-sin, sin], axis=-1), (1, N_HEADS))
    log_gamma = jnp.log(1.0 - jnp.exp2(-5.0 - jnp.arange(N_HEADS, dtype=F32)))
    i = jnp.arange(C, dtype=F32)
    dpos = i[:, None] - i[None, :]
    dm = jnp.where(dpos >= 0, jnp.exp(jnp.maximum(dpos, 0.0)[None] * log_gamma[:, None, None]), 0.0)
    lanes = lambda t: jnp.repeat(t, HEAD_DIM, axis=-1)
    qd = lanes(jnp.exp((i + 1.0)[:, None] * log_gamma[None, :]))
    kd = lanes(jnp.exp((C - 1.0 - i)[:, None] * log_gamma[None, :]))
    cd = lanes(jnp.exp(C * log_gamma)[None, :])
    args = (u, cosf, sins, dm.reshape(N_HEADS * C, C), qd, kd, cd,
            norm_g.reshape(1, -1).astype(F32), norm_b.reshape(1, -1).astype(F32), _block_ones())
    in_specs = ([_row_spec(C, 4 * MIX),
                 pl.BlockSpec((C, MIX), lambda b, c: (c, 0)),
                 pl.BlockSpec((C, MIX), lambda b, c: (c, 0))]
                + [_const_spec(t.shape) for t in args[3:]])
    return args, in_specs


def _recurrent_kernel(*refs, layer, n_in):
    n_h, n_t, n_r = n_in
    h_in, t_in, r_in = refs[:n_h], refs[n_h:n_h + n_t], refs[n_h + n_t:n_h + n_t + n_r]
    o_h, o_t, o_r, state_h, state_t, state_r, prev_r = refs[n_h + n_t + n_r:]
    streams = [(_rwkv_steps(*r_in, o_r, state_r, prev_r), 1),
               (_hgrn_steps(*h_in, o_h, state_h, layer=layer), RWKV_C // HGRN_C),
               (_ret_steps(*t_in, o_t, state_t), 1)]
    live = True
    while live:
        live = False
        for steps, per_round in streams:
            for _ in range(per_round):
                live = (next(steps, "done") != "done") or live


def _recurrent_call(u_hgrn, lb_logits, hgrn_norm, layer, u_ret, ret_norm_g, ret_norm_b,
                    u_rwkv, *rwkv_params):
    bsz, seq, _ = u_hgrn.shape
    tb = min(REC_TB, seq)
    h_args = (u_hgrn, lb_logits.astype(F32), hgrn_norm.reshape(1, -1).astype(F32), _block_ones())
    t_args, t_specs = _ret_operands(u_ret, ret_norm_g, ret_norm_b)
    vec = lambda t: t.reshape(1, -1).astype(F32)
    mu, w0, w_up, a0, a_up, g_up, k_k, k_a, r_k, norm_g, norm_b = rwkv_params
    r_args = (u_rwkv, vec(mu), vec(w0), w_up, vec(a0), a_up, g_up, vec(k_k), vec(k_a), vec(r_k),
              vec(norm_g), vec(norm_b), _block_ones())
    specs = lambda args: ([_row_spec(tb, 4 * MIX)] + [_const_spec(t.shape) for t in args[1:]])
    out = jax.ShapeDtypeStruct((bsz, seq, MIX), F32)
    state = pltpu.VMEM((MIX, MIX), F32)
    return pl.pallas_call(
        functools.partial(_recurrent_kernel, layer=layer,
                          n_in=(len(h_args), len(t_args), len(r_args))),
        grid=(bsz, seq // tb),
        in_specs=specs(h_args) + t_specs + specs(r_args),
        out_specs=[_row_spec(tb, MIX)] * 3,
        out_shape=[out] * 3,
        scratch_shapes=[state, state, state, pltpu.VMEM((1, 4 * MIX), F32)],
        compiler_params=_params(("parallel", "arbitrary")),
    )(*h_args, *t_args, *r_args)


DENSE_TM = 512
FFN_TF = 256


def _rmsnorm(x, g):
    return x * lax.rsqrt(jnp.mean(x * x, axis=-1, keepdims=True) + NORM_EPS) * g


def _resident(shape):
    nd = len(shape)
    return pl.BlockSpec(shape, lambda b, c: (0,) * nd, pipeline_mode=pl.Buffered(1))


def _inproj_kernel(h_ref, g_ref, w_ref, onehot_ref, q_ref, gate_ref, kc_ref, vc_ref, ks_ref,
                   vs_ref, kw_ref, vw_ref, hg_ref, rt_ref, rw_ref):
    xn = _rmsnorm(h_ref[...], g_ref[...]).astype(BF16)
    nsa = jnp.dot(xn, w_ref[:, 0:NSA_PAD], preferred_element_type=F32)
    tm = nsa.shape[0]
    col = lambda j: nsa[:, MIX + j * HEAD_DIM:MIX + (j + 1) * HEAD_DIM]
    q_ref[...] = nsa[:, 0:MIX]
    kc_ref[...] = col(0)
    vc_ref[...] = col(1)
    gate_ref[...] = nsa[:, MIX + 6 * HEAD_DIM:NSA_PAD]
    ks_ref[...] = jnp.concatenate(
        [onehot_ref[...], col(2).astype(BF16),
         jnp.zeros((tm, ks_ref.shape[1] - NSA_ONEHOT - HEAD_DIM), BF16)], axis=-1)
    kw_ref[...] = col(4).astype(BF16)
    vs_ref[...] = jnp.concatenate([col(3).T, _ones_row_block(tm)], axis=0).astype(BF16)
    vw_t = jnp.concatenate([col(5).T, _ones_row_block(tm)], axis=0).astype(BF16)
    tw = vw_ref.shape[-1]
    for t in range(tm // tw):
        vw_ref[t] = vw_t[:, t * tw:(t + 1) * tw]
    off = NSA_PAD
    for ref in (hg_ref, rt_ref, rw_ref):
        ref[...] = jnp.dot(xn, w_ref[:, off:off + 4 * MIX], preferred_element_type=F32)
        off += 4 * MIX


def _inproj_call(h, g, w_pad):
    bsz, seq, _ = h.shape
    tm = min(NSA_TK, seq)
    tw = min(NSA_TQ, seq)
    local = jnp.arange(tm) // NSA_SEL_BLOCK
    onehot = (local[:, None] == jnp.arange(NSA_ONEHOT)[None, :]).astype(BF16)
    struct = jax.ShapeDtypeStruct
    outs = [
        (struct((bsz, seq, MIX), F32), _row_spec(tm, MIX)),
        (struct((bsz, seq, 128), F32), _row_spec(tm, 128)),
        (struct((bsz, seq, HEAD_DIM), F32), _row_spec(tm, HEAD_DIM)),
        (struct((bsz, seq, HEAD_DIM), F32), _row_spec(tm, HEAD_DIM)),
        (struct((bsz, seq, 128), BF16), _row_spec(tm, 128)),
        (struct((bsz, seq // tm, NSA_VROWS, tm), BF16),
         pl.BlockSpec((None, None, NSA_VROWS, tm), lambda b, c: (b, c, 0, 0))),
        (struct((bsz, seq, HEAD_DIM), BF16), _row_spec(tm, HEAD_DIM)),
        (struct((bsz, seq // tw, NSA_VROWS, tw), BF16),
         pl.BlockSpec((None, tm // tw, NSA_VROWS, tw), lambda b, c: (b, c, 0, 0))),
    ] + [(struct((bsz, seq, 4 * MIX), F32), _row_spec(tm, 4 * MIX))] * 3
    res = pl.pallas_call(
        _inproj_kernel,
        grid=(bsz, seq // tm),
        in_specs=[_row_spec(tm, D_MODEL), _resident((1, D_MODEL)), _resident(w_pad.shape),
                  _resident(onehot.shape)],
        out_specs=[spec for _, spec in outs],
        out_shape=[shape for shape, _ in outs],
        compiler_params=_params(("parallel", "parallel")),
    )(h, g.reshape(1, -1), w_pad, onehot)
    return tuple(res[:8]), res[8], res[9], res[10]


def _merge_kernel(h_ref, b0_ref, b1_ref, b2_ref, b3_ref, g_ref, wg_ref, bg_ref, wb_ref, wo_ref,
                  o_ref):
    h = h_ref[...]
    xn = _rmsnorm(h, g_ref[...]).astype(BF16)
    merged = None
    for m, b_ref in enumerate((b0_ref, b1_ref, b2_ref, b3_ref)):
        gate = _sigmoid(jnp.dot(xn, wg_ref[m], preferred_element_type=F32) + bg_ref[m])
        term = gate * _bdot(b_ref[...], wb_ref[m])
        merged = term if merged is None else merged + term
    o_ref[...] = h + _bdot(merged, wo_ref[...])


def _merge_call(h, branches, g, w_gate, b_gate, w_branch, w_out):
    bsz, seq, _ = h.shape
    tm = min(DENSE_TM, seq)
    consts = (g.reshape(1, -1), w_gate, b_gate.reshape(4, 1, D_MODEL), w_branch, w_out)
    return pl.pallas_call(
        _merge_kernel,
        grid=(bsz, seq // tm),
        in_specs=([_row_spec(tm, D_MODEL)] + [_row_spec(tm, MIX)] * 4
                  + [_resident(t.shape) for t in consts]),
        out_specs=_row_spec(tm, D_MODEL),
        out_shape=jax.ShapeDtypeStruct(h.shape, F32),
        compiler_params=_params(("parallel", "parallel")),
    )(h, *branches, *consts)


def _ffn_kernel(h_ref, g_ref, wg_ref, wu_ref, wd_ref, o_ref):
    h = h_ref[...]
    hn = _rmsnorm(h, g_ref[...]).astype(BF16)
    acc = h
    for f in range(0, D_FF, FFN_TF):
        gate = jnp.dot(hn, wg_ref[:, f:f + FFN_TF], preferred_element_type=F32)
        up = jnp.dot(hn, wu_ref[:, f:f + FFN_TF], preferred_element_type=F32)
        acc = acc + _bdot(_silu(gate) * up, wd_ref[f:f + FFN_TF, :])
    o_ref[...] = acc


def _ffn_call(h, g, w_gate, w_up, w_down):
    bsz, seq, _ = h.shape
    tm = min(DENSE_TM, seq)
    consts = (g.reshape(1, -1), w_gate, w_up, w_down)
    return pl.pallas_call(
        _ffn_kernel,
        grid=(bsz, seq // tm),
        in_specs=[_row_spec(tm, D_MODEL)] + [_resident(t.shape) for t in consts],
        out_specs=_row_spec(tm, D_MODEL),
        out_shape=jax.ShapeDtypeStruct(h.shape, F32),
        compiler_params=_params(("parallel", "parallel")),
    )(h, *consts)


def _ple_kernel(h_ref, p_ref, g_ref, wg_ref, wp_ref, gf_ref, o_ref, *, final_norm):
    h = h_ref[...]
    hp = _rmsnorm(h, g_ref[...])
    out = h + _sigmoid(_bdot(hp, wg_ref[...])) * _bdot(p_ref[...], wp_ref[...])
    if final_norm:
        out = _rmsnorm(out, gf_ref[...])
    o_ref[...] = out


def _ple_call(h, p, layer, g, w_gate, w_proj, g_final, final_norm):
    bsz, seq, _ = h.shape
    tm = min(DENSE_TM, seq)
    consts = (g.reshape(1, -1), w_gate, w_proj, g_final.reshape(1, -1))
    return pl.pallas_call(
        functools.partial(_ple_kernel, final_norm=final_norm),
        grid=(bsz, seq // tm),
        in_specs=([_row_spec(tm, D_MODEL),
                   pl.BlockSpec((None, None, tm, PLE_DIM), lambda b, c: (layer, b, c, 0))]
                  + [_resident(t.shape) for t in consts]),
        out_specs=_row_spec(tm, D_MODEL),
        out_shape=jax.ShapeDtypeStruct(h.shape, F32),
        compiler_params=_params(("parallel", "parallel")),
    )(h, p, *consts)


def kernel(x, p, norm_mix, w_in, nsa_pos_k, nsa_pos_v, nsa_cmp_k1, nsa_cmp_k2, nsa_cmp_v1,
           nsa_cmp_v2, hgrn_lb_logits, hgrn_norm, ret_norm_g, ret_norm_b, rwkv_mu, rwkv_w0,
           rwkv_w_up, rwkv_a0, rwkv_a_up, rwkv_g_up, rwkv_k_k, rwkv_k_a, rwkv_r_k, rwkv_norm_g,
           rwkv_norm_b, w_branch, w_gate, b_gate, w_out, norm_ffn, w_ffn_gate, w_ffn_up,
           w_ffn_down, norm_ple, w_ple_gate, w_ple_proj, norm_final):
    depth = w_in.shape[0]
    w_in_pad = jnp.concatenate(
        [w_in[:, :, :NSA_WIDTH], jnp.zeros((depth, D_MODEL, NSA_PAD - NSA_WIDTH), w_in.dtype),
         w_in[:, :, NSA_WIDTH:]], axis=-1).astype(BF16)
    bf = lambda t: t.astype(BF16)
    h = x
    for i in range(depth):
        nsa_in, u_hgrn, u_ret, u_rwkv = _inproj_call(h, norm_mix[i], w_in_pad[i])
        o_hgrn, o_ret, o_rwkv = _recurrent_call(
            u_hgrn, hgrn_lb_logits, hgrn_norm[i], i, u_ret, ret_norm_g[i], ret_norm_b[i],
            u_rwkv, rwkv_mu[i], rwkv_w0[i], rwkv_w_up[i], rwkv_a0[i], rwkv_a_up[i], rwkv_g_up[i],
            rwkv_k_k[i], rwkv_k_a[i], rwkv_r_k[i], rwkv_norm_g[i], rwkv_norm_b[i])
        branches = (
            _nsa_call(*nsa_in, nsa_pos_k[i], nsa_pos_v[i], nsa_cmp_k1[i], nsa_cmp_k2[i],
                      nsa_cmp_v1[i], nsa_cmp_v2[i]),
            o_hgrn, o_ret, o_rwkv,
        )
        h = _merge_call(h, branches, norm_mix[i], bf(w_gate[i]), b_gate[i], bf(w_branch[i]),
                        bf(w_out[i]))
        h = _ffn_call(h, norm_ffn[i], bf(w_ffn_gate[i]), bf(w_ffn_up[i]), bf(w_ffn_down[i]))
        h = _ple_call(h, p, i, norm_ple[i], bf(w_ple_gate[i]), bf(w_ple_proj[i]), norm_final,
                      final_norm=(i == depth - 1))
    return h
```

```python
import functools
import math

import jax
import jax.numpy as jnp
from jax import lax
from jax.experimental import pallas as pl
from jax.experimental.pallas import tpu as pltpu

F32 = jnp.float32
BF16 = jnp.bfloat16

D_MODEL = 1024
N_HEADS = 4
HEAD_DIM = 64
MIX = N_HEADS * HEAD_DIM
D_FF = 2816
PLE_DIM = 256
NORM_EPS = 1e-6
NEG_BIG = -1e30
POS_BIG = 1e30
GATE_FLOOR = 1e-20

NSA_CMP_BLOCK = 32
NSA_CMP_STRIDE = 16
NSA_SEL_BLOCK = 64
NSA_TOP_N = 16
NSA_WINDOW = 512
NSA_CMP_HIDDEN = 128
NSA_WIDTH = 652
NSA_PAD = 768

RET_ROPE_BASE = 10000.0
RET_GN_EPS = 1e-5
RWKV_GN_EPS = 64e-5

VMEM_LIMIT = 56 * 1024 * 1024


def _bdot(a, b):
    return jnp.dot(a.astype(BF16), b.astype(BF16), preferred_element_type=F32)


def _bdot_nt(a, b):
    return lax.dot_general(a.astype(BF16), b.astype(BF16), (((1,), (1,)), ((), ())),
                           preferred_element_type=F32)


def _sigmoid(x):
    return 0.5 * jnp.tanh(0.5 * x) + 0.5


def _silu(x):
    return x * _sigmoid(x)


def _params(sem):
    return pltpu.CompilerParams(dimension_semantics=sem, vmem_limit_bytes=VMEM_LIMIT)


def _row_spec(tile, width):
    return pl.BlockSpec((None, tile, width), lambda b, c: (b, c, 0))


def _const_spec(shape):
    nd = len(shape)
    return pl.BlockSpec(shape, lambda b, c: (0,) * nd)


REC_TB = 256
RWKV_C = 64


def _split_dot(a_bf16, x, parts):
    total = None
    rest = x
    for _ in range(parts):
        piece = rest.astype(BF16)
        rest = rest - piece.astype(F32)
        term = jnp.dot(a_bf16, piece, preferred_element_type=F32)
        total = term if total is None else total + term
    return total


def _head_sum_mxu(x, ones_t):
    hi = x.astype(BF16)
    lo = (x - hi.astype(F32)).astype(BF16)
    return (jnp.dot(hi, ones_t, preferred_element_type=F32)
            + jnp.dot(lo, ones_t, preferred_element_type=F32))


def _rwkv_steps(u_ref, mu_ref, w0_ref, wup_ref, a0_ref, aup_ref, gup_ref, kk_ref, ka_ref,
                rk_ref, ng_ref, nb_ref, ones_ref, o_ref, state_ref, prev_ref):
    c = pl.program_id(1)
    ones = ones_ref[...]

    @pl.when(c == 0)
    def _():
        state_ref[...] = jnp.zeros_like(state_ref)
        prev_ref[...] = jnp.zeros_like(prev_ref)

    u = u_ref[...]
    tb = u.shape[0]
    row = lax.broadcasted_iota(jnp.int32, u.shape, 0)
    u_prev = jnp.where(row == 0, prev_ref[...], pltpu.roll(u, 1, axis=0))
    prev_ref[...] = u[tb - 1:tb, :]
    xs = u + mu_ref[...] * (u_prev - u)
    r = xs[:, 0:MIX]
    k = xs[:, MIX:2 * MIX]
    v = xs[:, 2 * MIX:3 * MIX]
    w_lo = xs[:, 3 * MIX:3 * MIX + 64]
    a_lo = xs[:, 3 * MIX + 64:3 * MIX + 128]
    g_lo = xs[:, 3 * MIX + 128:3 * MIX + 256]

    logw = -math.exp(-0.5) * _sigmoid(w0_ref[...] + _bdot(jnp.tanh(w_lo), wup_ref[...]))
    a = _sigmoid(a0_ref[...] + _bdot(a_lo, aup_ref[...]))
    g = _bdot(_sigmoid(g_lo), gup_ref[...])
    kk = k * kk_ref[...]
    kk = kk * lax.rsqrt(jnp.maximum(_head_sum_mxu(kk * kk, ones), 1e-24))
    k2 = k * (1.0 + (a - 1.0) * ka_ref[...])
    alpha = -kk
    beta = kk * a
    bonus = _head_sum_mxu(r * k2 * rk_ref[...], ones) * v

    C = RWKV_C
    ti = lax.broadcasted_iota(jnp.int32, (tb, tb), 0)
    si = lax.broadcasted_iota(jnp.int32, (tb, tb), 1)
    same_chunk = (ti // C) == (si // C)
    prefix = (same_chunk & (ti >= si)).astype(BF16)
    cum = _split_dot(prefix, logw, 3)
    cum_last = _split_dot(same_chunk.astype(BF16), logw, 3)
    gam_all = jnp.exp(cum_last)
    e_inv = jnp.exp(-cum)
    e_last = jnp.exp(cum_last - cum)
    ag_all = alpha * jnp.exp(cum - logw)
    rg_all = r * jnp.exp(cum)
    bi_all = beta * e_inv
    ki_all = k2 * e_inv
    bl_all = beta * e_last
    kl_all = k2 * e_last

    hc = N_HEADS * C
    row_head = lax.broadcasted_iota(jnp.int32, (hc, MIX), 0) // C
    lane_head = lax.broadcasted_iota(jnp.int32, (hc, MIX), 1) // HEAD_DIM
    own = row_head == lane_head

    def stack(x):
        return jnp.where(own, jnp.concatenate([x] * N_HEADS, axis=0), 0.0).astype(BF16)

    rr = lax.broadcasted_iota(jnp.int32, (hc, hc), 0)
    cc = lax.broadcasted_iota(jnp.int32, (hc, hc), 1)
    strict = rr > cc
    incl = rr >= cc
    eye_hc = (rr == cc).astype(F32)
    kr = lax.broadcasted_iota(jnp.int32, (MIX, MIX), 0)
    kc_ = lax.broadcasted_iota(jnp.int32, (MIX, MIX), 1)
    eye_k = kr == kc_

    def chunk_affine(ch, out):
        sl = slice(ch * C, (ch + 1) * C)
        ag, rg, bi, ki, bl, kl, vm = (stack(t[sl]) for t in (ag_all, rg_all, bi_all, ki_all,
                                                               bl_all, kl_all, v))
        aa = _bdot_nt(jnp.concatenate([ag, rg], axis=0), jnp.concatenate([bi, ki], axis=0))
        yield
        a_ab = jnp.where(strict, aa[:hc, :hc], 0.0)
        a_ak = jnp.where(strict, aa[:hc, hc:], 0.0)
        a_rb = jnp.where(incl, aa[hc:, :hc], 0.0)
        a_rk = jnp.where(incl, aa[hc:, hc:], 0.0)
        t_inv = eye_hc + a_ab
        pw = _bdot(a_ab, a_ab)
        av = _bdot(a_ak, vm)
        yield
        for _ in range(int(math.log2(C)) - 2):
            both = _bdot(pw, jnp.concatenate([t_inv, pw], axis=-1))
            t_inv = t_inv + both[:, :hc]
            pw = both[:, hc:]
            yield
        t_inv = t_inv + _bdot(pw, t_inv)
        yield
        w12 = _bdot(t_inv, jnp.concatenate([ag.astype(F32), av], axis=-1))
        yield
        ry = _bdot(a_rb, w12)
        rq = rg.astype(F32) + ry[:, :MIX]
        y0 = ry[:, MIX:] + _bdot(a_rk, vm)
        mn = _bdot(bl.T, w12)
        m_mat = jnp.where(eye_k, gam_all[ch * C:ch * C + 1, :], 0.0) + mn[:, :MIX]
        n_mat = mn[:, MIX:] + _bdot(kl.T, vm)
        out.append((jnp.concatenate([rq, m_mat], axis=0).astype(BF16), y0, n_mat))

    yield
    affine = [[] for _ in range(tb // C)]
    chains = [chunk_affine(ch, affine[ch]) for ch in range(tb // C)]
    live = True
    while live:
        live = False
        for chain in chains:
            live = (next(chain, "done") != "done") or live
        yield

    state = state_ref[...]
    y_chunks = []
    for ((lhs, y0, n_mat),) in affine:
        prod = jnp.dot(lhs, state.astype(BF16), preferred_element_type=F32)
        ym = prod[:hc] + y0
        y_chunks.append(ym[0:C] + ym[C:2 * C] + ym[2 * C:3 * C] + ym[3 * C:4 * C])
        state = prod[hc:] + n_mat
        yield
    state_ref[...] = state
    y = jnp.concatenate(y_chunks, axis=0)

    mean = _head_sum_mxu(y, ones) * (1.0 / HEAD_DIM)
    yc = y - mean
    var = _head_sum_mxu(yc * yc, ones) * (1.0 / HEAD_DIM)
    yn = yc * lax.rsqrt(var + RWKV_GN_EPS) * ng_ref[...] + nb_ref[...]
    o_ref[...] = (yn + bonus) * g


NSA_TQ = 256
NSA_TK = 512
NSA_VROWS = 80
NSA_UNROLL = 2
NSA_CMP_PARTS = 4
NSA_ONEHOT = 16
GROUP = NSA_CMP_STRIDE


def _ones_row_block(width):
    row = lax.broadcasted_iota(jnp.int32, (NSA_VROWS - HEAD_DIM, width), 0)
    return (row == 0).astype(F32)


def _nsa_compress_kernel(xk_ref, xv_ref, pk_ref, pv_ref, k1_ref, k2_ref, v1_ref, v2_ref, ov_ref,
                         kc_ref, vc_ref):
    ng = kc_ref.shape[0]

    def compress(x_ref, pos_ref, w1_ref, w2_ref):
        first = jnp.zeros((ng, k2_ref.shape[0]), F32)
        second = jnp.zeros((ng, k2_ref.shape[0]), F32)
        for j in range(GROUP):
            xj = x_ref[pl.ds(j, ng, stride=GROUP), :]
            lo, hi = j * HEAD_DIM, (GROUP + j) * HEAD_DIM
            first = first + _bdot(xj + pos_ref[j:j + 1, :], w1_ref[lo:lo + HEAD_DIM, :])
            second = second + _bdot(xj + pos_ref[GROUP + j:GROUP + j + 1, :],
                                    w1_ref[hi:hi + HEAD_DIM, :])
        hid = first + pltpu.roll(second, ng - 1, axis=0)
        return _bdot(_silu(hid), w2_ref[...])

    kc = compress(xk_ref, pk_ref, k1_ref, k2_ref)
    vc = compress(xv_ref, pv_ref, v1_ref, v2_ref)
    kc_ref[...] = kc.astype(BF16)
    ng = vc.shape[0]
    vc_ref[0:NSA_VROWS, :] = jnp.concatenate([vc.T, _ones_row_block(ng)], axis=0).astype(BF16)
    vc_ref[NSA_VROWS:, :] = ov_ref[...]


def _nsa_compress_call(xk, xv, pos_k, pos_v, k1, k2, v1, v2, overlap_t):
    bsz, seq, width = xk.shape
    ng = seq // GROUP
    n_sel = overlap_t.shape[0]
    args = (xk, xv, pos_k, pos_v, k1.astype(BF16), k2.astype(BF16),
            v1.astype(BF16), v2.astype(BF16), overlap_t)
    blk = pl.BlockSpec((None, seq, width), lambda b: (b, 0, 0))
    const = lambda t: pl.BlockSpec(t.shape, lambda b: (0,) * t.ndim)
    return pl.pallas_call(
        _nsa_compress_kernel,
        grid=(bsz,),
        in_specs=[blk, blk] + [const(t) for t in args[2:]],
        out_specs=[pl.BlockSpec((None, ng, HEAD_DIM), lambda b: (b, 0, 0)),
                   pl.BlockSpec((None, NSA_VROWS + n_sel, ng), lambda b: (b, 0, 0))],
        out_shape=[jax.ShapeDtypeStruct((bsz, ng, HEAD_DIM), BF16),
                   jax.ShapeDtypeStruct((bsz, NSA_VROWS + n_sel, ng), BF16)],
        compiler_params=_params(("parallel",)),
    )(*args)


REMOVED = -3e38


def _nsa_kernel(q_ref, g_ref, ks_ref, vs_ref, kw_ref, vw_ref, kc_ref, vc_ref, o_ref,
                selbias_ref, *, n_top):
    c = pl.program_id(1)
    tq = q_ref.shape[0]
    tk = ks_ref.shape[1]
    ct = kc_ref.shape[0]
    ns = selbias_ref.shape[0]
    t0 = c * tq
    cols = N_HEADS * tq

    q_t = (q_ref[...] * (HEAD_DIM ** -0.5)).T
    qs = jnp.concatenate([q_t[h * HEAD_DIM:(h + 1) * HEAD_DIM] for h in range(N_HEADS)],
                         axis=1).astype(BF16)
    t_q = t0 + lax.broadcasted_iota(jnp.int32, (1, tq), 1)
    t_col = jnp.concatenate([t_q] * N_HEADS, axis=1)

    def online(carry, s, v_aug):
        m, acc = carry
        m_new = jnp.maximum(m, jnp.max(s, axis=0, keepdims=True))
        p = jnp.exp(s - m_new)
        acc = jnp.exp(m - m_new) * acc + jnp.dot(v_aug, p.astype(BF16), preferred_element_type=F32)
        return m_new, acc

    def normalise(acc):
        return acc[0:HEAD_DIM] / acc[HEAD_DIM:HEAD_DIM + 1]

    init_aug = (jnp.full((1, cols), NEG_BIG, F32), jnp.zeros((vs_ref.shape[1], cols), F32))

    vrows = vs_ref.shape[1]

    def cmp_quarters(quarters):
        rows = quarters * (ct // NSA_CMP_PARTS)
        n_row = lax.broadcasted_iota(jnp.int32, (rows, 1), 0)
        valid = (n_row * NSA_CMP_STRIDE + (NSA_CMP_BLOCK - 1)) <= t_col
        s = jnp.where(valid, jnp.dot(kc_ref[0:rows, :], qs, preferred_element_type=F32), NEG_BIG)
        m = jnp.max(s, axis=0, keepdims=True)
        p = jnp.exp(s - m)
        p_hi = p.astype(BF16)
        p_lo = (p - p_hi.astype(F32)).astype(BF16)
        res = jnp.dot(vc_ref[:, 0:rows], p_hi, preferred_element_type=F32)
        imp_lo = jnp.dot(vc_ref[vrows:, 0:rows], p_lo, preferred_element_type=F32)
        return m, jnp.concatenate([res[:vrows], res[vrows:] + imp_lo], axis=0)

    last_valid = (t0 + tq - NSA_CMP_BLOCK) // NSA_CMP_STRIDE
    quarter = jnp.clip(last_valid // (ct // NSA_CMP_PARTS), 0, NSA_CMP_PARTS - 1)
    m_c, res_c = lax.switch(quarter, [functools.partial(cmp_quarters, i + 1)
                                      for i in range(NSA_CMP_PARTS)])
    inv_c = jnp.where(m_c > 0.5 * NEG_BIG, 1.0 / res_c[HEAD_DIM:HEAD_DIM + 1], 0.0)
    o_cmp = res_c[0:HEAD_DIM] * inv_c
    imp4 = res_c[vrows:] * inv_c
    imp = imp4[:, 0:tq]
    for h in range(1, N_HEADS):
        imp = imp + imp4[:, h * tq:(h + 1) * tq]

    def window_steps(out):
        tw = kw_ref.shape[1]
        n_wt = (NSA_WINDOW + tq) // tw
        jw = (t0 - NSA_WINDOW) // tw
        wkey_row = lax.broadcasted_iota(jnp.int32, (tw, 1), 0)
        s_parts = []
        w_tiles = []
        for i in range(n_wt):
            exists = (jw + i) >= 0
            w_tiles.append(jnp.maximum(jw + i, 0))
            dist = t_q - ((jw + i) * tw + wkey_row)
            s = jnp.dot(kw_ref[w_tiles[i]], qs, preferred_element_type=F32)
            if i == 0:
                bias = jnp.where((dist < NSA_WINDOW) & exists, 0.0, NEG_BIG)
                s = s + jnp.concatenate([bias] * N_HEADS, axis=1)
            elif i == n_wt - 1:
                bias = jnp.where(dist >= 0, 0.0, NEG_BIG)
                s = s + jnp.concatenate([bias] * N_HEADS, axis=1)
            else:
                s = s + jnp.where(exists, 0.0, NEG_BIG)
            s_parts.append(s)
            yield
        m_w = s_parts[0].max(axis=0, keepdims=True)
        for sp in s_parts[1:]:
            m_w = jnp.maximum(m_w, sp.max(axis=0, keepdims=True))
        yield
        acc_w = jnp.zeros((vw_ref.shape[1], cols), F32)
        for i, sp in enumerate(s_parts):
            p = jnp.exp(sp - m_w)
            acc_w = acc_w + jnp.dot(vw_ref[w_tiles[i]], p.astype(BF16),
                                    preferred_element_type=F32)
            yield
        out.append(normalise(acc_w))

    blk = lax.broadcasted_iota(jnp.int32, (ns, 1), 0)
    blk_f = blk.astype(F32)
    cur = t_q // NSA_SEL_BLOCK
    forced = (blk == 0) | (blk == cur) | (blk == cur - 1)
    score = jnp.where(forced, POS_BIG, jnp.where(blk <= cur, imp, NEG_BIG))
    chosen = jnp.zeros((ns, tq), jnp.bool_)
    window_out = []
    window = window_steps(window_out)
    for _ in range(n_top):
        best = jnp.max(score, axis=0, keepdims=True)
        first = jnp.min(jnp.where(score == best, blk_f, float(ns)), axis=0, keepdims=True)
        hit = blk_f == first
        chosen = chosen | hit
        score = jnp.where(hit, REMOVED, score)
        next(window, None)
    for _ in window:
        pass
    (o_win,) = window_out
    selbias_ref[...] = jnp.where(chosen, 0.0, NEG_BIG)

    per_tile = tk // NSA_SEL_BLOCK
    key_row = lax.broadcasted_iota(jnp.int32, (tk, 1), 0)

    pad_rows = jnp.zeros((ks_ref.shape[2] - HEAD_DIM - NSA_ONEHOT, cols), BF16)
    bias_pad = jnp.zeros((NSA_ONEHOT - per_tile, tq), F32)

    def sel_scores(j, live=None):
        start = pl.multiple_of(j * per_tile, per_tile)
        bias = selbias_ref[pl.ds(start, per_tile), :]
        if live is not None:
            bias = jnp.where(live, bias, NEG_BIG)
        bias = jnp.concatenate([bias, bias_pad], axis=0)
        bias = jnp.concatenate([bias.astype(BF16)] * N_HEADS, axis=1)
        rhs = jnp.concatenate([bias, qs, pad_rows], axis=0)
        return jnp.dot(ks_ref[j], rhs, preferred_element_type=F32)

    j_last = t0 // tk

    def sel_group(i, carry):
        tiles = []
        for u in range(NSA_UNROLL):
            j = i * NSA_UNROLL + u
            tiles.append((jnp.minimum(j, j_last - 1), None if u == 0 else j < j_last))
        scores = [sel_scores(j, live) for j, live in tiles]
        for (j, _), s in zip(tiles, scores):
            carry = online(carry, s, vs_ref[j])
        return carry

    carry = lax.fori_loop(0, (j_last + NSA_UNROLL - 1) // NSA_UNROLL, sel_group, init_aug)
    causal = jnp.where((j_last * tk + key_row) <= t_q, 0.0, NEG_BIG)
    _, acc_s = online(carry, sel_scores(j_last) + jnp.concatenate([causal] * N_HEADS, axis=1),
                      vs_ref[j_last])
    o_sel = normalise(acc_s)

    gates = _sigmoid(g_ref[...]).T

    def gate_row(branch):
        return jnp.concatenate([gates[branch * N_HEADS + h:branch * N_HEADS + h + 1, :]
                                for h in range(N_HEADS)], axis=1)

    out = gate_row(0) * o_cmp + gate_row(1) * o_sel + gate_row(2) * o_win
    o_ref[...] = jnp.concatenate([out[:, h * tq:(h + 1) * tq].T for h in range(N_HEADS)], axis=-1)


def _nsa_call(q, gates, k_cmp, v_cmp, k_sel, v_sel, k_win, v_win, pos_k, pos_v, k1, k2, v1, v2):
    bsz, seq, _ = q.shape
    ng = seq // GROUP
    n_sel = seq // NSA_SEL_BLOCK
    n_top = min(NSA_TOP_N, n_sel)
    cmp_start = jnp.arange(ng) * NSA_CMP_STRIDE
    sel_start = jnp.arange(n_sel) * NSA_SEL_BLOCK
    overlap = ((cmp_start[:, None] < sel_start[None, :] + NSA_SEL_BLOCK)
               & (cmp_start[:, None] + NSA_CMP_BLOCK > sel_start[None, :])).astype(BF16)
    kc, vc_stack = _nsa_compress_call(k_cmp, v_cmp, pos_k, pos_v, k1, k2, v1, v2, overlap.T)
    tq = min(NSA_TQ, seq)
    tk, tw = v_sel.shape[-1], v_win.shape[-1]
    operands = (q, gates,
                k_sel.reshape(bsz, seq // tk, tk, k_sel.shape[-1]), v_sel,
                k_win.reshape(bsz, seq // tw, tw, HEAD_DIM), v_win,
                kc, vc_stack)
    per_batch = lambda t: pl.BlockSpec((None,) + t.shape[1:],
                                       lambda b, c: (b,) + (0,) * (t.ndim - 1))
    return pl.pallas_call(
        functools.partial(_nsa_kernel, n_top=n_top),
        grid=(bsz, seq // tq),
        in_specs=[_row_spec(tq, MIX), _row_spec(tq, 128)] + [per_batch(t) for t in operands[2:]],
        out_specs=_row_spec(tq, MIX),
        out_shape=jax.ShapeDtypeStruct((bsz, seq, MIX), F32),
        scratch_shapes=[pltpu.VMEM((n_sel, tq), F32)],
        compiler_params=_params(("parallel", "arbitrary")),
    )(*operands)


HGRN_C = 32


def _hgrn_steps(u_ref, lbl_ref, ng_ref, ones_ref, o_ref, state_ref, *, layer):
    c = pl.program_id(1)

    @pl.when(c == 0)
    def _():
        state_ref[...] = jnp.zeros_like(state_ref)

    logits = lbl_ref[...]
    ex = jnp.exp(logits - jnp.max(logits, axis=0, keepdims=True))
    soft = ex / jnp.sum(ex, axis=0, keepdims=True)
    lb = jnp.sum(soft[0:layer + 1], axis=0, keepdims=True) - soft[0:1]

    u = u_ref[...]
    tb = u.shape[0]
    q = _silu(u[:, 0:MIX])
    f = lb + (1.0 - lb) / (1.0 + jnp.exp(-u[:, MIX:2 * MIX]))
    logf = jnp.log(jnp.maximum(f, GATE_FLOOR))
    k = 1.0 - f
    v = u[:, 2 * MIX:3 * MIX]
    og = u[:, 3 * MIX:4 * MIX]

    C = HGRN_C
    SUB = 8
    ones_bd = ones_ref[...]
    ti = lax.broadcasted_iota(jnp.int32, (tb, tb), 0)
    si = lax.broadcasted_iota(jnp.int32, (tb, tb), 1)
    same_chunk = (ti // C) == (si // C)
    b_all = _split_dot((same_chunk & (ti >= si)).astype(BF16), logf, 3)
    b_last_all = _split_dot(same_chunk.astype(BF16), logf, 3)
    b2_all = b_all * math.log2(math.e)
    qe_all = q * jnp.exp(b_all)
    kd_all = k * jnp.exp(b_last_all - b_all)
    g_last_all = jnp.exp(b_last_all)
    row8 = lax.broadcasted_iota(jnp.int32, (SUB, 1), 0)
    hr = lax.broadcasted_iota(jnp.int32, (MIX, MIX), 0) // HEAD_DIM
    hc_ = lax.broadcasted_iota(jnp.int32, (MIX, MIX), 1) // HEAD_DIM
    same_head = hr == hc_

    def chunk_intra(ch, out):
        sl = slice(ch * C, (ch + 1) * C)
        qc, kc, vc, b2 = q[sl], k[sl], v[sl], b2_all[sl]
        pieces = []
        for s in range(C):
            r0 = (s // SUB) * SUB
            pm = qc[r0:] * (kc[s:s + 1, :] * jnp.exp2(b2[r0:] - b2[s:s + 1, :]))
            top = jnp.where(row8 + r0 >= s, pm[0:SUB], 0.0)
            pieces.append(top if C - r0 == SUB else jnp.concatenate([top, pm[SUB:]], axis=0))
        attn = jnp.dot(jnp.concatenate(pieces, axis=0).astype(BF16), ones_bd,
                       preferred_element_type=F32)
        outer = jnp.dot(vc.T.astype(BF16), kd_all[sl].astype(BF16), preferred_element_type=F32)
        yield
        groups = [jnp.zeros((SUB, MIX), F32) for _ in range(C // SUB)]
        off = 0
        for s in range(C):
            g0 = s // SUB
            for g in range(g0, C // SUB):
                groups[g] = groups[g] + attn[off:off + SUB, :] * vc[s:s + 1, :]
                off += SUB
        out.append((jnp.concatenate(groups, axis=0), jnp.where(same_head, outer, 0.0)))

    yield
    n_chunks = tb // C
    intra = [[] for _ in range(n_chunks)]
    chains = [chunk_intra(ch, intra[ch]) for ch in range(n_chunks)]
    next(chains[0])
    for ch in range(n_chunks):
        if ch + 1 < n_chunks:
            next(chains[ch + 1])
        next(chains[ch], None)
        yield

    state = state_ref[...]
    o_chunks = []
    for ch in range(n_chunks):
        sl = slice(ch * C, (ch + 1) * C)
        ((o_intra, outer),) = intra[ch]
        o_chunks.append(o_intra + _bdot_nt(qe_all[sl], state))
        state = state * g_last_all[ch * C:ch * C + 1, :] + outer
        if ch % 2 == 1:
            yield
    state_ref[...] = state
    o = jnp.concatenate(o_chunks, axis=0)
    ms = _head_sum_mxu(o * o, ones_bd) * (1.0 / HEAD_DIM)
    o_ref[...] = o * lax.rsqrt(ms + NORM_EPS) * ng_ref[...] * _silu(og)


def _block_ones():
    hid = jnp.arange(MIX) // HEAD_DIM
    return (hid[:, None] == hid[None, :]).astype(BF16)


def _ret_steps(u_ref, cos_ref, sin_ref, dm_ref, qd_ref, kd_ref, cd_ref, ng_ref, nb_ref,
               ones_ref, o_ref, state_ref):
    c = pl.program_id(1)

    @pl.when(c == 0)
    def _():
        state_ref[...] = jnp.zeros_like(state_ref)

    u = u_ref[...]
    cosf = cos_ref[...]
    sins = sin_ref[...]
    half = HEAD_DIM // 2

    def rope(a):
        outs = []
        for j in range(MIX // 128):
            blk = a[:, j * 128:(j + 1) * 128]
            lane = lax.broadcasted_iota(jnp.int32, blk.shape, 1)
            swapped = jnp.where((lane % HEAD_DIM) < half, pltpu.roll(blk, 128 - half, axis=1),
                                pltpu.roll(blk, half, axis=1))
            outs.append(swapped)
        return a * cosf + jnp.concatenate(outs, axis=-1) * sins

    q = rope(u[:, 0:MIX])
    k = rope(u[:, MIX:2 * MIX]) * (HEAD_DIM ** -0.5)
    v = u[:, 2 * MIX:3 * MIX]
    g = u[:, 3 * MIX:4 * MIX]
    qd = q * qd_ref[...]
    kd = k * kd_ref[...]
    C = u.shape[0]
    yield
    row_head = lax.broadcasted_iota(jnp.int32, (N_HEADS * C, MIX), 0) // C
    lane_head = lax.broadcasted_iota(jnp.int32, (N_HEADS * C, MIX), 1) // HEAD_DIM
    own = row_head == lane_head
    q_stack = jnp.where(own, jnp.concatenate([q] * N_HEADS, axis=0), 0.0)
    s = _bdot_nt(q_stack, k) * dm_ref[...]
    sv = jnp.where(own, _bdot(s, v), 0.0)
    o = sv[0:C]
    for h in range(1, N_HEADS):
        o = o + sv[h * C:(h + 1) * C]
    yield
    state = state_ref[...]
    o = o + _bdot(qd, state)
    kr = lax.broadcasted_iota(jnp.int32, (MIX, MIX), 0) // HEAD_DIM
    kc = lax.broadcasted_iota(jnp.int32, (MIX, MIX), 1) // HEAD_DIM
    state_ref[...] = state * cd_ref[...] + jnp.where(kr == kc, _bdot(kd.T, v), 0.0)
    ones = ones_ref[...]
    mean = _head_sum_mxu(o, ones) * (1.0 / HEAD_DIM)
    oc = o - mean
    var = _head_sum_mxu(oc * oc, ones) * (1.0 / HEAD_DIM)
    y = oc * lax.rsqrt(var + RET_GN_EPS) * ng_ref[...] + nb_ref[...]
    o_ref[...] = y * _silu(g)


def _ret_operands(u, norm_g, norm_b):
    bsz, seq, _ = u.shape
    C = min(REC_TB, seq)
    pos = jnp.arange(seq, dtype=F32)
    inv_freq = RET_ROPE_BASE ** (-jnp.arange(0, HEAD_DIM, 2, dtype=F32) / HEAD_DIM)
    ang = pos[:, None] * inv_freq[None, :]
    cos, sin = jnp.cos(ang), jnp.sin(ang)
    cosf = jnp.tile(jnp.concatenate([cos, cos], axis=-1), (1, N_HEADS))
    sins = jnp.tile(jnp.concatenate([-sin, sin], axis=-1), (1, N_HEADS))
    log_gamma = jnp.log(1.0 - jnp.exp2(-5.0 - jnp.arange(N_HEADS, dtype=F32)))
    i = jnp.arange(C, dtype=F32)
    dpos = i[:, None] - i[None, :]
    dm = jnp.where(dpos >= 0, jnp.exp(jnp.maximum(dpos, 0.0)[None] * log_gamma[:, None, None]), 0.0)
    lanes = lambda t: jnp.repeat(t, HEAD_DIM, axis=-1)
    qd = lanes(jnp.exp((i + 1.0)[:, None] * log_gamma[None, :]))
    kd = lanes(jnp.exp((C - 1.0 - i)[:, None] * log_gamma[None, :]))
    cd = lanes(jnp.exp(C * log_gamma)[None, :])
    args = (u, cosf, sins, dm.reshape(N_HEADS * C, C), qd, kd, cd,
            norm_g.reshape(1, -1).astype(F32), norm_b.reshape(1, -1).astype(F32), _block_ones())
    in_specs = ([_row_spec(C, 4 * MIX),
                 pl.BlockSpec((C, MIX), lambda b, c: (c, 0)),
                 pl.BlockSpec((C, MIX), lambda b, c: (c, 0))]
                + [_const_spec(t.shape) for t in args[3:]])
    return args, in_specs


def _recurrent_kernel(*refs, layer, n_in):
    n_h, n_t, n_r = n_in
    h_in, t_in, r_in = refs[:n_h], refs[n_h:n_h + n_t], refs[n_h + n_t:n_h + n_t + n_r]
    o_h, o_t, o_r, state_h, state_t, state_r, prev_r = refs[n_h + n_t + n_r:]
    streams = [(_rwkv_steps(*r_in, o_r, state_r, prev_r), 1),
               (_hgrn_steps(*h_in, o_h, state_h, layer=layer), RWKV_C // HGRN_C),
               (_ret_steps(*t_in, o_t, state_t), 1)]
    live = True
    while live:
        live = False
        for steps, per_round in streams:
            for _ in range(per_round):
                live = (next(steps, "done") != "done") or live


def _recurrent_call(u_hgrn, lb_logits, hgrn_norm, layer, u_ret, ret_norm_g, ret_norm_b,
                    u_rwkv, *rwkv_params):
    bsz, seq, _ = u_hgrn.shape
    tb = min(REC_TB, seq)
    h_args = (u_hgrn, lb_logits.astype(F32), hgrn_norm.reshape(1, -1).astype(F32), _block_ones())
    t_args, t_specs = _ret_operands(u_ret, ret_norm_g, ret_norm_b)
    vec = lambda t: t.reshape(1, -1).astype(F32)
    mu, w0, w_up, a0, a_up, g_up, k_k, k_a, r_k, norm_g, norm_b = rwkv_params
    r_args = (u_rwkv, vec(mu), vec(w0), w_up, vec(a0), a_up, g_up, vec(k_k), vec(k_a), vec(r_k),
              vec(norm_g), vec(norm_b), _block_ones())
    specs = lambda args: ([_row_spec(tb, 4 * MIX)] + [_const_spec(t.shape) for t in args[1:]])
    out = jax.ShapeDtypeStruct((bsz, seq, MIX), F32)
    state = pltpu.VMEM((MIX, MIX), F32)
    return pl.pallas_call(
        functools.partial(_recurrent_kernel, layer=layer,
                          n_in=(len(h_args), len(t_args), len(r_args))),
        grid=(bsz, seq // tb),
        in_specs=specs(h_args) + t_specs + specs(r_args),
        out_specs=[_row_spec(tb, MIX)] * 3,
        out_shape=[out] * 3,
        scratch_shapes=[state, state, state, pltpu.VMEM((1, 4 * MIX), F32)],
        compiler_params=_params(("parallel", "arbitrary")),
    )(*h_args, *t_args, *r_args)


DENSE_TM = 512
FFN_TF = 256


def _rmsnorm(x, g):
    return x * lax.rsqrt(jnp.mean(x * x, axis=-1, keepdims=True) + NORM_EPS) * g


def _resident(shape):
    nd = len(shape)
    return pl.BlockSpec(shape, lambda b, c: (0,) * nd, pipeline_mode=pl.Buffered(1))


def _inproj_kernel(h_ref, g_ref, w_ref, onehot_ref, q_ref, gate_ref, kc_ref, vc_ref, ks_ref,
                   vs_ref, kw_ref, vw_ref, hg_ref, rt_ref, rw_ref):
    xn = _rmsnorm(h_ref[...], g_ref[...]).astype(BF16)
    nsa = jnp.dot(xn, w_ref[:, 0:NSA_PAD], preferred_element_type=F32)
    tm = nsa.shape[0]
    col = lambda j: nsa[:, MIX + j * HEAD_DIM:MIX + (j + 1) * HEAD_DIM]
    q_ref[...] = nsa[:, 0:MIX]
    kc_ref[...] = col(0)
    vc_ref[...] = col(1)
    gate_ref[...] = nsa[:, MIX + 6 * HEAD_DIM:NSA_PAD]
    ks_ref[...] = jnp.concatenate(
        [onehot_ref[...], col(2).astype(BF16),
         jnp.zeros((tm, ks_ref.shape[1] - NSA_ONEHOT - HEAD_DIM), BF16)], axis=-1)
    kw_ref[...] = col(4).astype(BF16)
    vs_ref[...] = jnp.concatenate([col(3).T, _ones_row_block(tm)], axis=0).astype(BF16)
    vw_t = jnp.concatenate([col(5).T, _ones_row_block(tm)], axis=0).astype(BF16)
    tw = vw_ref.shape[-1]
    for t in range(tm // tw):
        vw_ref[t] = vw_t[:, t * tw:(t + 1) * tw]
    off = NSA_PAD
    for ref in (hg_ref, rt_ref, rw_ref):
        ref[...] = jnp.dot(xn, w_ref[:, off:off + 4 * MIX], preferred_element_type=F32)
        off += 4 * MIX


def _inproj_call(h, g, w_pad):
    bsz, seq, _ = h.shape
    tm = min(NSA_TK, seq)
    tw = min(NSA_TQ, seq)
    local = jnp.arange(tm) // NSA_SEL_BLOCK
    onehot = (local[:, None] == jnp.arange(NSA_ONEHOT)[None, :]).astype(BF16)
    struct = jax.ShapeDtypeStruct
    outs = [
        (struct((bsz, seq, MIX), F32), _row_spec(tm, MIX)),
        (struct((bsz, seq, 128), F32), _row_spec(tm, 128)),
        (struct((bsz, seq, HEAD_DIM), F32), _row_spec(tm, HEAD_DIM)),
        (struct((bsz, seq, HEAD_DIM), F32), _row_spec(tm, HEAD_DIM)),
        (struct((bsz, seq, 128), BF16), _row_spec(tm, 128)),
        (struct((bsz, seq // tm, NSA_VROWS, tm), BF16),
         pl.BlockSpec((None, None, NSA_VROWS, tm), lambda b, c: (b, c, 0, 0))),
        (struct((bsz, seq, HEAD_DIM), BF16), _row_spec(tm, HEAD_DIM)),
        (struct((bsz, seq // tw, NSA_VROWS, tw), BF16),
         pl.BlockSpec((None, tm // tw, NSA_VROWS, tw), lambda b, c: (b, c, 0, 0))),
    ] + [(struct((bsz, seq, 4 * MIX), F32), _row_spec(tm, 4 * MIX))] * 3
    res = pl.pallas_call(
        _inproj_kernel,
        grid=(bsz, seq // tm),
        in_specs=[_row_spec(tm, D_MODEL), _resident((1, D_MODEL)), _resident(w_pad.shape),
                  _resident(onehot.shape)],
        out_specs=[spec for _, spec in outs],
        out_shape=[shape for shape, _ in outs],
        compiler_params=_params(("parallel", "parallel")),
    )(h, g.reshape(1, -1), w_pad, onehot)
    return tuple(res[:8]), res[8], res[9], res[10]


def _merge_kernel(h_ref, b0_ref, b1_ref, b2_ref, b3_ref, g_ref, wg_ref, bg_ref, wb_ref, wo_ref,
                  o_ref):
    h = h_ref[...]
    xn = _rmsnorm(h, g_ref[...]).astype(BF16)
    merged = None
    for m, b_ref in enumerate((b0_ref, b1_ref, b2_ref, b3_ref)):
        gate = _sigmoid(jnp.dot(xn, wg_ref[m], preferred_element_type=F32) + bg_ref[m])
        term = gate * _bdot(b_ref[...], wb_ref[m])
        merged = term if merged is None else merged + term
    o_ref[...] = h + _bdot(merged, wo_ref[...])


def _merge_call(h, branches, g, w_gate, b_gate, w_branch, w_out):
    bsz, seq, _ = h.shape
    tm = min(DENSE_TM, seq)
    consts = (g.reshape(1, -1), w_gate, b_gate.reshape(4, 1, D_MODEL), w_branch, w_out)
    return pl.pallas_call(
        _merge_kernel,
        grid=(bsz, seq // tm),
        in_specs=([_row_spec(tm, D_MODEL)] + [_row_spec(tm, MIX)] * 4
                  + [_resident(t.shape) for t in consts]),
        out_specs=_row_spec(tm, D_MODEL),
        out_shape=jax.ShapeDtypeStruct(h.shape, F32),
        compiler_params=_params(("parallel", "parallel")),
    )(h, *branches, *consts)


def _ffn_kernel(h_ref, g_ref, wg_ref, wu_ref, wd_ref, o_ref):
    h = h_ref[...]
    hn = _rmsnorm(h, g_ref[...]).astype(BF16)
    acc = h
    for f in range(0, D_FF, FFN_TF):
        gate = jnp.dot(hn, wg_ref[:, f:f + FFN_TF], preferred_element_type=F32)
        up = jnp.dot(hn, wu_ref[:, f:f + FFN_TF], preferred_element_type=F32)
        acc = acc + _bdot(_silu(gate) * up, wd_ref[f:f + FFN_TF, :])
    o_ref[...] = acc


def _ffn_call(h, g, w_gate, w_up, w_down):
    bsz, seq, _ = h.shape
    tm = min(DENSE_TM, seq)
    consts = (g.reshape(1, -1), w_gate, w_up, w_down)
    return pl.pallas_call(
        _ffn_kernel,
        grid=(bsz, seq // tm),
        in_specs=[_row_spec(tm, D_MODEL)] + [_resident(t.shape) for t in consts],
        out_specs=_row_spec(tm, D_MODEL),
        out_shape=jax.ShapeDtypeStruct(h.shape, F32),
        compiler_params=_params(("parallel", "parallel")),
    )(h, *consts)


def _ple_kernel(h_ref, p_ref, g_ref, wg_ref, wp_ref, gf_ref, o_ref, *, final_norm):
    h = h_ref[...]
    hp = _rmsnorm(h, g_ref[...])
    out = h + _sigmoid(_bdot(hp, wg_ref[...])) * _bdot(p_ref[...], wp_ref[...])
    if final_norm:
        out = _rmsnorm(out, gf_ref[...])
    o_ref[...] = out


def _ple_call(h, p, layer, g, w_gate, w_proj, g_final, final_norm):
    bsz, seq, _ = h.shape
    tm = min(DENSE_TM, seq)
    consts = (g.reshape(1, -1), w_gate, w_proj, g_final.reshape(1, -1))
    return pl.pallas_call(
        functools.partial(_ple_kernel, final_norm=final_norm),
        grid=(bsz, seq // tm),
        in_specs=([_row_spec(tm, D_MODEL),
                   pl.BlockSpec((None, None, tm, PLE_DIM), lambda b, c: (layer, b, c, 0))]
                  + [_resident(t.shape) for t in consts]),
        out_specs=_row_spec(tm, D_MODEL),
        out_shape=jax.ShapeDtypeStruct(h.shape, F32),
        compiler_params=_params(("parallel", "parallel")),
    )(h, p, *consts)


def kernel(x, p, norm_mix, w_in, nsa_pos_k, nsa_pos_v, nsa_cmp_k1, nsa_cmp_k2, nsa_cmp_v1,
           nsa_cmp_v2, hgrn_lb_logits, hgrn_norm, ret_norm_g, ret_norm_b, rwkv_mu, rwkv_w0,
           rwkv_w_up, rwkv_a0, rwkv_a_up, rwkv_g_up, rwkv_k_k, rwkv_k_a, rwkv_r_k, rwkv_norm_g,
           rwkv_norm_b, w_branch, w_gate, b_gate, w_out, norm_ffn, w_ffn_gate, w_ffn_up,
           w_ffn_down, norm_ple, w_ple_gate, w_ple_proj, norm_final):
    depth = w_in.shape[0]
    w_in_pad = jnp.concatenate(
        [w_in[:, :, :NSA_WIDTH], jnp.zeros((depth, D_MODEL, NSA_PAD - NSA_WIDTH), w_in.dtype),
         w_in[:, :, NSA_WIDTH:]], axis=-1).astype(BF16)
    bf = lambda t: t.astype(BF16)
    h = x
    for i in range(depth):
        nsa_in, u_hgrn, u_ret, u_rwkv = _inproj_call(h, norm_mix[i], w_in_pad[i])
        o_hgrn, o_ret, o_rwkv = _recurrent_call(
            u_hgrn, hgrn_lb_logits, hgrn_norm[i], i, u_ret, ret_norm_g[i], ret_norm_b[i],
            u_rwkv, rwkv_mu[i], rwkv_w0[i], rwkv_w_up[i], rwkv_a0[i], rwkv_a_up[i], rwkv_g_up[i],
            rwkv_k_k[i], rwkv_k_a[i], rwkv_r_k[i], rwkv_norm_g[i], rwkv_norm_b[i])
        branches = (
            _nsa_call(*nsa_in, nsa_pos_k[i], nsa_pos_v[i], nsa_cmp_k1[i], nsa_cmp_k2[i],
                      nsa_cmp_v1[i], nsa_cmp_v2[i]),
            o_hgrn, o_ret, o_rwkv,
        )
        h = _merge_call(h, branches, norm_mix[i], bf(w_gate[i]), b_gate[i], bf(w_branch[i]),
                        bf(w_out[i]))
        h = _ffn_call(h, norm_ffn[i], bf(w_ffn_gate[i]), bf(w_ffn_up[i]), bf(w_ffn_down[i]))
        h = _ple_call(h, p, i, norm_ple[i], bf(w_ple_gate[i]), bf(w_ple_proj[i]), norm_final,
                      final_norm=(i == depth - 1))
    return h
```

```python
import functools
import math

import jax
import jax.numpy as jnp
from jax import lax
from jax.experimental import pallas as pl
from jax.experimental.pallas import tpu as pltpu

F32 = jnp.float32
BF16 = jnp.bfloat16

D_MODEL = 1024
N_HEADS = 4
HEAD_DIM = 64
MIX = N_HEADS * HEAD_DIM
D_FF = 2816
PLE_DIM = 256
NORM_EPS = 1e-6
NEG_BIG = -1e30
POS_BIG = 1e30
GATE_FLOOR = 1e-20

NSA_CMP_BLOCK = 32
NSA_CMP_STRIDE = 16
NSA_SEL_BLOCK = 64
NSA_TOP_N = 16
NSA_WINDOW = 512
NSA_CMP_HIDDEN = 128
NSA_WIDTH = 652
NSA_PAD = 768

RET_ROPE_BASE = 10000.0
RET_GN_EPS = 1e-5
RWKV_GN_EPS = 64e-5

VMEM_LIMIT = 56 * 1024 * 1024


def _bdot(a, b):
    return jnp.dot(a.astype(BF16), b.astype(BF16), preferred_element_type=F32)


def _bdot_nt(a, b):
    return lax.dot_general(a.astype(BF16), b.astype(BF16), (((1,), (1,)), ((), ())),
                           preferred_element_type=F32)


def _sigmoid(x):
    return 0.5 * jnp.tanh(0.5 * x) + 0.5


def _silu(x):
    return x * _sigmoid(x)


def _params(sem):
    return pltpu.CompilerParams(dimension_semantics=sem, vmem_limit_bytes=VMEM_LIMIT)


def _row_spec(tile, width):
    return pl.BlockSpec((None, tile, width), lambda b, c: (b, c, 0))


def _const_spec(shape):
    nd = len(shape)
    return pl.BlockSpec(shape, lambda b, c: (0,) * nd)


REC_TB = 256
RWKV_C = 32


def _split_dot(a_bf16, x, parts):
    total = None
    rest = x
    for _ in range(parts):
        piece = rest.astype(BF16)
        rest = rest - piece.astype(F32)
        term = jnp.dot(a_bf16, piece, preferred_element_type=F32)
        total = term if total is None else total + term
    return total


def _head_sum_mxu(x, ones_t):
    hi = x.astype(BF16)
    lo = (x - hi.astype(F32)).astype(BF16)
    return (jnp.dot(hi, ones_t, preferred_element_type=F32)
            + jnp.dot(lo, ones_t, preferred_element_type=F32))


def _rwkv_steps(u_ref, mu_ref, w0_ref, wup_ref, a0_ref, aup_ref, gup_ref, kk_ref, ka_ref,
                rk_ref, ng_ref, nb_ref, ones_ref, o_ref, state_ref, prev_ref):
    c = pl.program_id(1)
    ones = ones_ref[...]

    @pl.when(c == 0)
    def _():
        state_ref[...] = jnp.zeros_like(state_ref)
        prev_ref[...] = jnp.zeros_like(prev_ref)

    u = u_ref[...]
    tb = u.shape[0]
    row = lax.broadcasted_iota(jnp.int32, u.shape, 0)
    u_prev = jnp.where(row == 0, prev_ref[...], pltpu.roll(u, 1, axis=0))
    prev_ref[...] = u[tb - 1:tb, :]
    xs = u + mu_ref[...] * (u_prev - u)
    r = xs[:, 0:MIX]
    k = xs[:, MIX:2 * MIX]
    v = xs[:, 2 * MIX:3 * MIX]
    w_lo = xs[:, 3 * MIX:3 * MIX + 64]
    a_lo = xs[:, 3 * MIX + 64:3 * MIX + 128]
    g_lo = xs[:, 3 * MIX + 128:3 * MIX + 256]

    logw = -math.exp(-0.5) * _sigmoid(w0_ref[...] + _bdot(jnp.tanh(w_lo), wup_ref[...]))
    a = _sigmoid(a0_ref[...] + _bdot(a_lo, aup_ref[...]))
    g = _bdot(_sigmoid(g_lo), gup_ref[...])
    kk = k * kk_ref[...]
    kk = kk * lax.rsqrt(jnp.maximum(_head_sum_mxu(kk * kk, ones), 1e-24))
    k2 = k * (1.0 + (a - 1.0) * ka_ref[...])
    alpha = -kk
    beta = kk * a
    bonus = _head_sum_mxu(r * k2 * rk_ref[...], ones) * v

    C = RWKV_C
    ti = lax.broadcasted_iota(jnp.int32, (tb, tb), 0)
    si = lax.broadcasted_iota(jnp.int32, (tb, tb), 1)
    same_chunk = (ti // C) == (si // C)
    prefix = (same_chunk & (ti >= si)).astype(BF16)
    cum = _split_dot(prefix, logw, 3)
    cum_last = _split_dot(same_chunk.astype(BF16), logw, 3)
    gam_all = jnp.exp(cum_last)
    e_inv = jnp.exp(-cum)
    e_last = jnp.exp(cum_last - cum)
    ag_all = alpha * jnp.exp(cum - logw)
    rg_all = r * jnp.exp(cum)
    bi_all = beta * e_inv
    ki_all = k2 * e_inv
    bl_all = beta * e_last
    kl_all = k2 * e_last

    hc = N_HEADS * C
    row_head = lax.broadcasted_iota(jnp.int32, (hc, MIX), 0) // C
    lane_head = lax.broadcasted_iota(jnp.int32, (hc, MIX), 1) // HEAD_DIM
    own = row_head == lane_head

    def stack(x):
        return jnp.where(own, jnp.concatenate([x] * N_HEADS, axis=0), 0.0).astype(BF16)

    rr = lax.broadcasted_iota(jnp.int32, (hc, hc), 0)
    cc = lax.broadcasted_iota(jnp.int32, (hc, hc), 1)
    strict = rr > cc
    incl = rr >= cc
    eye_hc = (rr == cc).astype(F32)
    kr = lax.broadcasted_iota(jnp.int32, (MIX, MIX), 0)
    kc_ = lax.broadcasted_iota(jnp.int32, (MIX, MIX), 1)
    eye_k = kr == kc_

    def chunk_affine(ch, out):
        sl = slice(ch * C, (ch + 1) * C)
        ag, rg, bi, ki, bl, kl, vm = (stack(t[sl]) for t in (ag_all, rg_all, bi_all, ki_all,
                                                               bl_all, kl_all, v))
        aa = _bdot_nt(jnp.concatenate([ag, rg], axis=0), jnp.concatenate([bi, ki], axis=0))
        yield
        a_ab = jnp.where(strict, aa[:hc, :hc], 0.0)
        a_ak = jnp.where(strict, aa[:hc, hc:], 0.0)
        a_rb = jnp.where(incl, aa[hc:, :hc], 0.0)
        a_rk = jnp.where(incl, aa[hc:, hc:], 0.0)
        t_inv = eye_hc + a_ab
        pw = _bdot(a_ab, a_ab)
        av = _bdot(a_ak, vm)
        yield
        for _ in range(int(math.log2(C)) - 2):
            both = _bdot(pw, jnp.concatenate([t_inv, pw], axis=-1))
            t_inv = t_inv + both[:, :hc]
            pw = both[:, hc:]
            yield
        t_inv = t_inv + _bdot(pw, t_inv)
        yield
        w12 = _bdot(t_inv, jnp.concatenate([ag.astype(F32), av], axis=-1))
        yield
        ry = _bdot(a_rb, w12)
        rq = rg.astype(F32) + ry[:, :MIX]
        y0 = ry[:, MIX:] + _bdot(a_rk, vm)
        mn = _bdot(bl.T, w12)
        m_mat = jnp.where(eye_k, gam_all[ch * C:ch * C + 1, :], 0.0) + mn[:, :MIX]
        n_mat = mn[:, MIX:] + _bdot(kl.T, vm)
        out.append((jnp.concatenate([rq, m_mat], axis=0).astype(BF16), y0, n_mat))

    yield
    affine = [[] for _ in range(tb // C)]
    chains = [chunk_affine(ch, affine[ch]) for ch in range(tb // C)]
    live = True
    while live:
        live = False
        for chain in chains:
            live = (next(chain, "done") != "done") or live
        yield

    state = state_ref[...]
    y_chunks = []
    for ((lhs, y0, n_mat),) in affine:
        prod = jnp.dot(lhs, state.astype(BF16), preferred_element_type=F32)
        ym = prod[:hc] + y0
        y_chunks.append(ym[0:C] + ym[C:2 * C] + ym[2 * C:3 * C] + ym[3 * C:4 * C])
        state = prod[hc:] + n_mat
        yield
    state_ref[...] = state
    y = jnp.concatenate(y_chunks, axis=0)

    mean = _head_sum_mxu(y, ones) * (1.0 / HEAD_DIM)
    yc = y - mean
    var = _head_sum_mxu(yc * yc, ones) * (1.0 / HEAD_DIM)
    yn = yc * lax.rsqrt(var + RWKV_GN_EPS) * ng_ref[...] + nb_ref[...]
    o_ref[...] = (yn + bonus) * g


NSA_TQ = 256
NSA_TK = 512
NSA_VROWS = 80
NSA_UNROLL = 2
NSA_CMP_PARTS = 4
NSA_ONEHOT = 16
GROUP = NSA_CMP_STRIDE


def _ones_row_block(width):
    row = lax.broadcasted_iota(jnp.int32, (NSA_VROWS - HEAD_DIM, width), 0)
    return (row == 0).astype(F32)


def _nsa_compress_kernel(xk_ref, xv_ref, pk_ref, pv_ref, k1_ref, k2_ref, v1_ref, v2_ref, ov_ref,
                         kc_ref, vc_ref):
    ng = kc_ref.shape[0]

    def compress(x_ref, pos_ref, w1_ref, w2_ref):
        first = jnp.zeros((ng, k2_ref.shape[0]), F32)
        second = jnp.zeros((ng, k2_ref.shape[0]), F32)
        for j in range(GROUP):
            xj = x_ref[pl.ds(j, ng, stride=GROUP), :]
            lo, hi = j * HEAD_DIM, (GROUP + j) * HEAD_DIM
            first = first + _bdot(xj + pos_ref[j:j + 1, :], w1_ref[lo:lo + HEAD_DIM, :])
            second = second + _bdot(xj + pos_ref[GROUP + j:GROUP + j + 1, :],
                                    w1_ref[hi:hi + HEAD_DIM, :])
        hid = first + pltpu.roll(second, ng - 1, axis=0)
        return _bdot(_silu(hid), w2_ref[...])

    kc = compress(xk_ref, pk_ref, k1_ref, k2_ref)
    vc = compress(xv_ref, pv_ref, v1_ref, v2_ref)
    kc_ref[...] = kc.astype(BF16)
    ng = vc.shape[0]
    vc_ref[0:NSA_VROWS, :] = jnp.concatenate([vc.T, _ones_row_block(ng)], axis=0).astype(BF16)
    vc_ref[NSA_VROWS:, :] = ov_ref[...]


def _nsa_compress_call(xk, xv, pos_k, pos_v, k1, k2, v1, v2, overlap_t):
    bsz, seq, width = xk.shape
    ng = seq // GROUP
    n_sel = overlap_t.shape[0]
    args = (xk, xv, pos_k, pos_v, k1.astype(BF16), k2.astype(BF16),
            v1.astype(BF16), v2.astype(BF16), overlap_t)
    blk = pl.BlockSpec((None, seq, width), lambda b: (b, 0, 0))
    const = lambda t: pl.BlockSpec(t.shape, lambda b: (0,) * t.ndim)
    return pl.pallas_call(
        _nsa_compress_kernel,
        grid=(bsz,),
        in_specs=[blk, blk] + [const(t) for t in args[2:]],
        out_specs=[pl.BlockSpec((None, ng, HEAD_DIM), lambda b: (b, 0, 0)),
                   pl.BlockSpec((None, NSA_VROWS + n_sel, ng), lambda b: (b, 0, 0))],
        out_shape=[jax.ShapeDtypeStruct((bsz, ng, HEAD_DIM), BF16),
                   jax.ShapeDtypeStruct((bsz, NSA_VROWS + n_sel, ng), BF16)],
        compiler_params=_params(("parallel",)),
    )(*args)


REMOVED = -3e38


def _nsa_kernel(q_ref, g_ref, ks_ref, vs_ref, kw_ref, vw_ref, kc_ref, vc_ref, o_ref,
                selbias_ref, *, n_top):
    c = pl.program_id(1)
    tq = q_ref.shape[0]
    tk = ks_ref.shape[1]
    ct = kc_ref.shape[0]
    ns = selbias_ref.shape[0]
    t0 = c * tq
    cols = N_HEADS * tq

    q_t = (q_ref[...] * (HEAD_DIM ** -0.5)).T
    qs = jnp.concatenate([q_t[h * HEAD_DIM:(h + 1) * HEAD_DIM] for h in range(N_HEADS)],
                         axis=1).astype(BF16)
    t_q = t0 + lax.broadcasted_iota(jnp.int32, (1, tq), 1)
    t_col = jnp.concatenate([t_q] * N_HEADS, axis=1)

    def online(carry, s, v_aug):
        m, acc = carry
        m_new = jnp.maximum(m, jnp.max(s, axis=0, keepdims=True))
        p = jnp.exp(s - m_new)
        acc = jnp.exp(m - m_new) * acc + jnp.dot(v_aug, p.astype(BF16), preferred_element_type=F32)
        return m_new, acc

    def normalise(acc):
        return acc[0:HEAD_DIM] / acc[HEAD_DIM:HEAD_DIM + 1]

    init_aug = (jnp.full((1, cols), NEG_BIG, F32), jnp.zeros((vs_ref.shape[1], cols), F32))

    vrows = vs_ref.shape[1]

    def cmp_quarters(quarters):
        rows = quarters * (ct // NSA_CMP_PARTS)
        n_row = lax.broadcasted_iota(jnp.int32, (rows, 1), 0)
        valid = (n_row * NSA_CMP_STRIDE + (NSA_CMP_BLOCK - 1)) <= t_col
        s = jnp.where(valid, jnp.dot(kc_ref[0:rows, :], qs, preferred_element_type=F32), NEG_BIG)
        m = jnp.max(s, axis=0, keepdims=True)
        p = jnp.exp(s - m)
        p_hi = p.astype(BF16)
        p_lo = (p - p_hi.astype(F32)).astype(BF16)
        res = jnp.dot(vc_ref[:, 0:rows], p_hi, preferred_element_type=F32)
        imp_lo = jnp.dot(vc_ref[vrows:, 0:rows], p_lo, preferred_element_type=F32)
        return m, jnp.concatenate([res[:vrows], res[vrows:] + imp_lo], axis=0)

    last_valid = (t0 + tq - NSA_CMP_BLOCK) // NSA_CMP_STRIDE
    quarter = jnp.clip(last_valid // (ct // NSA_CMP_PARTS), 0, NSA_CMP_PARTS - 1)
    m_c, res_c = lax.switch(quarter, [functools.partial(cmp_quarters, i + 1)
                                      for i in range(NSA_CMP_PARTS)])
    inv_c = jnp.where(m_c > 0.5 * NEG_BIG, 1.0 / res_c[HEAD_DIM:HEAD_DIM + 1], 0.0)
    o_cmp = res_c[0:HEAD_DIM] * inv_c
    imp4 = res_c[vrows:] * inv_c
    imp = imp4[:, 0:tq]
    for h in range(1, N_HEADS):
        imp = imp + imp4[:, h * tq:(h + 1) * tq]

    def window_steps(out):
        tw = kw_ref.shape[1]
        n_wt = (NSA_WINDOW + tq) // tw
        jw = (t0 - NSA_WINDOW) // tw
        wkey_row = lax.broadcasted_iota(jnp.int32, (tw, 1), 0)
        s_parts = []
        w_tiles = []
        for i in range(n_wt):
            exists = (jw + i) >= 0
            w_tiles.append(jnp.maximum(jw + i, 0))
            dist = t_q - ((jw + i) * tw + wkey_row)
            s = jnp.dot(kw_ref[w_tiles[i]], qs, preferred_element_type=F32)
            if i == 0:
                bias = jnp.where((dist < NSA_WINDOW) & exists, 0.0, NEG_BIG)
                s = s + jnp.concatenate([bias] * N_HEADS, axis=1)
            elif i == n_wt - 1:
                bias = jnp.where(dist >= 0, 0.0, NEG_BIG)
                s = s + jnp.concatenate([bias] * N_HEADS, axis=1)
            else:
                s = s + jnp.where(exists, 0.0, NEG_BIG)
            s_parts.append(s)
            yield
        m_w = s_parts[0].max(axis=0, keepdims=True)
        for sp in s_parts[1:]:
            m_w = jnp.maximum(m_w, sp.max(axis=0, keepdims=True))
        yield
        acc_w = jnp.zeros((vw_ref.shape[1], cols), F32)
        for i, sp in enumerate(s_parts):
            p = jnp.exp(sp - m_w)
            acc_w = acc_w + jnp.dot(vw_ref[w_tiles[i]], p.astype(BF16),
                                    preferred_element_type=F32)
            yield
        out.append(normalise(acc_w))

    blk = lax.broadcasted_iota(jnp.int32, (ns, 1), 0)
    blk_f = blk.astype(F32)
    cur = t_q // NSA_SEL_BLOCK
    forced = (blk == 0) | (blk == cur) | (blk == cur - 1)
    score = jnp.where(forced, POS_BIG, jnp.where(blk <= cur, imp, NEG_BIG))
    chosen = jnp.zeros((ns, tq), jnp.bool_)
    window_out = []
    window = window_steps(window_out)
    for _ in range(n_top):
        best = jnp.max(score, axis=0, keepdims=True)
        first = jnp.min(jnp.where(score == best, blk_f, float(ns)), axis=0, keepdims=True)
        hit = blk_f == first
        chosen = chosen | hit
        score = jnp.where(hit, REMOVED, score)
        next(window, None)
    for _ in window:
        pass
    (o_win,) = window_out
    selbias_ref[...] = jnp.where(chosen, 0.0, NEG_BIG)

    per_tile = tk // NSA_SEL_BLOCK
    key_row = lax.broadcasted_iota(jnp.int32, (tk, 1), 0)

    pad_rows = jnp.zeros((ks_ref.shape[2] - HEAD_DIM - NSA_ONEHOT, cols), BF16)
    bias_pad = jnp.zeros((NSA_ONEHOT - per_tile, tq), F32)

    def sel_scores(j, live=None):
        start = pl.multiple_of(j * per_tile, per_tile)
        bias = selbias_ref[pl.ds(start, per_tile), :]
        if live is not None:
            bias = jnp.where(live, bias, NEG_BIG)
        bias = jnp.concatenate([bias, bias_pad], axis=0)
        bias = jnp.concatenate([bias.astype(BF16)] * N_HEADS, axis=1)
        rhs = jnp.concatenate([bias, qs, pad_rows], axis=0)
        return jnp.dot(ks_ref[j], rhs, preferred_element_type=F32)

    j_last = t0 // tk

    def sel_group(i, carry):
        tiles = []
        for u in range(NSA_UNROLL):
            j = i * NSA_UNROLL + u
            tiles.append((jnp.minimum(j, j_last - 1), None if u == 0 else j < j_last))
        scores = [sel_scores(j, live) for j, live in tiles]
        for (j, _), s in zip(tiles, scores):
            carry = online(carry, s, vs_ref[j])
        return carry

    carry = lax.fori_loop(0, (j_last + NSA_UNROLL - 1) // NSA_UNROLL, sel_group, init_aug)
    causal = jnp.where((j_last * tk + key_row) <= t_q, 0.0, NEG_BIG)
    _, acc_s = online(carry, sel_scores(j_last) + jnp.concatenate([causal] * N_HEADS, axis=1),
                      vs_ref[j_last])
    o_sel = normalise(acc_s)

    gates = _sigmoid(g_ref[...]).T

    def gate_row(branch):
        return jnp.concatenate([gates[branch * N_HEADS + h:branch * N_HEADS + h + 1, :]
                                for h in range(N_HEADS)], axis=1)

    out = gate_row(0) * o_cmp + gate_row(1) * o_sel + gate_row(2) * o_win
    o_ref[...] = jnp.concatenate([out[:, h * tq:(h + 1) * tq].T for h in range(N_HEADS)], axis=-1)


def _nsa_call(q, gates, k_cmp, v_cmp, k_sel, v_sel, k_win, v_win, pos_k, pos_v, k1, k2, v1, v2):
    bsz, seq, _ = q.shape
    ng = seq // GROUP
    n_sel = seq // NSA_SEL_BLOCK
    n_top = min(NSA_TOP_N, n_sel)
    cmp_start = jnp.arange(ng) * NSA_CMP_STRIDE
    sel_start = jnp.arange(n_sel) * NSA_SEL_BLOCK
    overlap = ((cmp_start[:, None] < sel_start[None, :] + NSA_SEL_BLOCK)
               & (cmp_start[:, None] + NSA_CMP_BLOCK > sel_start[None, :])).astype(BF16)
    kc, vc_stack = _nsa_compress_call(k_cmp, v_cmp, pos_k, pos_v, k1, k2, v1, v2, overlap.T)
    tq = min(NSA_TQ, seq)
    tk, tw = v_sel.shape[-1], v_win.shape[-1]
    operands = (q, gates,
                k_sel.reshape(bsz, seq // tk, tk, k_sel.shape[-1]), v_sel,
                k_win.reshape(bsz, seq // tw, tw, HEAD_DIM), v_win,
                kc, vc_stack)
    per_batch = lambda t: pl.BlockSpec((None,) + t.shape[1:],
                                       lambda b, c: (b,) + (0,) * (t.ndim - 1))
    return pl.pallas_call(
        functools.partial(_nsa_kernel, n_top=n_top),
        grid=(bsz, seq // tq),
        in_specs=[_row_spec(tq, MIX), _row_spec(tq, 128)] + [per_batch(t) for t in operands[2:]],
        out_specs=_row_spec(tq, MIX),
        out_shape=jax.ShapeDtypeStruct((bsz, seq, MIX), F32),
        scratch_shapes=[pltpu.VMEM((n_sel, tq), F32)],
        compiler_params=_params(("parallel", "arbitrary")),
    )(*operands)


HGRN_C = 32


def _hgrn_steps(u_ref, lbl_ref, ng_ref, ones_ref, o_ref, state_ref, *, layer):
    c = pl.program_id(1)

    @pl.when(c == 0)
    def _():
        state_ref[...] = jnp.zeros_like(state_ref)

    logits = lbl_ref[...]
    ex = jnp.exp(logits - jnp.max(logits, axis=0, keepdims=True))
    soft = ex / jnp.sum(ex, axis=0, keepdims=True)
    lb = jnp.sum(soft[0:layer + 1], axis=0, keepdims=True) - soft[0:1]

    u = u_ref[...]
    tb = u.shape[0]
    q = _silu(u[:, 0:MIX])
    f = lb + (1.0 - lb) / (1.0 + jnp.exp(-u[:, MIX:2 * MIX]))
    logf = jnp.log(jnp.maximum(f, GATE_FLOOR))
    k = 1.0 - f
    v = u[:, 2 * MIX:3 * MIX]
    og = u[:, 3 * MIX:4 * MIX]

    C = HGRN_C
    SUB = 8
    ones_bd = ones_ref[...]
    ti = lax.broadcasted_iota(jnp.int32, (tb, tb), 0)
    si = lax.broadcasted_iota(jnp.int32, (tb, tb), 1)
    same_chunk = (ti // C) == (si // C)
    b_all = _split_dot((same_chunk & (ti >= si)).astype(BF16), logf, 3)
    b_last_all = _split_dot(same_chunk.astype(BF16), logf, 3)
    b2_all = b_all * math.log2(math.e)
    qe_all = q * jnp.exp(b_all)
    kd_all = k * jnp.exp(b_last_all - b_all)
    g_last_all = jnp.exp(b_last_all)
    row8 = lax.broadcasted_iota(jnp.int32, (SUB, 1), 0)
    hr = lax.broadcasted_iota(jnp.int32, (MIX, MIX), 0) // HEAD_DIM
    hc_ = lax.broadcasted_iota(jnp.int32, (MIX, MIX), 1) // HEAD_DIM
    same_head = hr == hc_

    def chunk_intra(ch, out):
        sl = slice(ch * C, (ch + 1) * C)
        qc, kc, vc, b2 = q[sl], k[sl], v[sl], b2_all[sl]
        pieces = []
        for s in range(C):
            r0 = (s // SUB) * SUB
            pm = qc[r0:] * (kc[s:s + 1, :] * jnp.exp2(b2[r0:] - b2[s:s + 1, :]))
            top = jnp.where(row8 + r0 >= s, pm[0:SUB], 0.0)
            pieces.append(top if C - r0 == SUB else jnp.concatenate([top, pm[SUB:]], axis=0))
        attn = jnp.dot(jnp.concatenate(pieces, axis=0).astype(BF16), ones_bd,
                       preferred_element_type=F32)
        outer = jnp.dot(vc.T.astype(BF16), kd_all[sl].astype(BF16), preferred_element_type=F32)
        yield
        groups = [jnp.zeros((SUB, MIX), F32) for _ in range(C // SUB)]
        off = 0
        for s in range(C):
            g0 = s // SUB
            for g in range(g0, C // SUB):
                groups[g] = groups[g] + attn[off:off + SUB, :] * vc[s:s + 1, :]
                off += SUB
        out.append((jnp.concatenate(groups, axis=0), jnp.where(same_head, outer, 0.0)))

    yield
    n_chunks = tb // C
    intra = [[] for _ in range(n_chunks)]
    chains = [chunk_intra(ch, intra[ch]) for ch in range(n_chunks)]
    next(chains[0])
    for ch in range(n_chunks):
        if ch + 1 < n_chunks:
            next(chains[ch + 1])
        next(chains[ch], None)
        yield

    state = state_ref[...]
    o_chunks = []
    for ch in range(n_chunks):
        sl = slice(ch * C, (ch + 1) * C)
        ((o_intra, outer),) = intra[ch]
        o_chunks.append(o_intra + _bdot_nt(qe_all[sl], state))
        state = state * g_last_all[ch * C:ch * C + 1, :] + outer
        if ch % 2 == 1:
            yield
    state_ref[...] = state
    o = jnp.concatenate(o_chunks, axis=0)
    ms = _head_sum_mxu(o * o, ones_bd) * (1.0 / HEAD_DIM)
    o_ref[...] = o * lax.rsqrt(ms + NORM_EPS) * ng_ref[...] * _silu(og)


def _block_ones():
    hid = jnp.arange(MIX) // HEAD_DIM
    return (hid[:, None] == hid[None, :]).astype(BF16)


def _ret_steps(u_ref, cos_ref, sin_ref, dm_ref, qd_ref, kd_ref, cd_ref, ng_ref, nb_ref,
               ones_ref, o_ref, state_ref):
    c = pl.program_id(1)

    @pl.when(c == 0)
    def _():
        state_ref[...] = jnp.zeros_like(state_ref)

    u = u_ref[...]
    cosf = cos_ref[...]
    sins = sin_ref[...]
    half = HEAD_DIM // 2

    def rope(a):
        outs = []
        for j in range(MIX // 128):
            blk = a[:, j * 128:(j + 1) * 128]
            lane = lax.broadcasted_iota(jnp.int32, blk.shape, 1)
            swapped = jnp.where((lane % HEAD_DIM) < half, pltpu.roll(blk, 128 - half, axis=1),
                                pltpu.roll(blk, half, axis=1))
            outs.append(swapped)
        return a * cosf + jnp.concatenate(outs, axis=-1) * sins

    q = rope(u[:, 0:MIX])
    k = rope(u[:, MIX:2 * MIX]) * (HEAD_DIM ** -0.5)
    v = u[:, 2 * MIX:3 * MIX]
    g = u[:, 3 * MIX:4 * MIX]
    qd = q * qd_ref[...]
    kd = k * kd_ref[...]
    C = u.shape[0]
    yield
    row_head = lax.broadcasted_iota(jnp.int32, (N_HEADS * C, MIX), 0) // C
    lane_head = lax.broadcasted_iota(jnp.int32, (N_HEADS * C, MIX), 1) // HEAD_DIM
    own = row_head == lane_head
    q_stack = jnp.where(own, jnp.concatenate([q] * N_HEADS, axis=0), 0.0)
    s = _bdot_nt(q_stack, k) * dm_ref[...]
    sv = jnp.where(own, _bdot(s, v), 0.0)
    o = sv[0:C]
    for h in range(1, N_HEADS):
        o = o + sv[h * C:(h + 1) * C]
    yield
    state = state_ref[...]
    o = o + _bdot(qd, state)
    kr = lax.broadcasted_iota(jnp.int32, (MIX, MIX), 0) // HEAD_DIM
    kc = lax.broadcasted_iota(jnp.int32, (MIX, MIX), 1) // HEAD_DIM
    state_ref[...] = state * cd_ref[...] + jnp.where(kr == kc, _bdot(kd.T, v), 0.0)
    ones = ones_ref[...]
    mean = _head_sum_mxu(o, ones) * (1.0 / HEAD_DIM)
    oc = o - mean
    var = _head_sum_mxu(oc * oc, ones) * (1.0 / HEAD_DIM)
    y = oc * lax.rsqrt(var + RET_GN_EPS) * ng_ref[...] + nb_ref[...]
    o_ref[...] = y * _silu(g)


def _ret_operands(u, norm_g, norm_b):
    bsz, seq, _ = u.shape
    C = min(REC_TB, seq)
    pos = jnp.arange(seq, dtype=F32)
    inv_freq = RET_ROPE_BASE ** (-jnp.arange(0, HEAD_DIM, 2, dtype=F32) / HEAD_DIM)
    ang = pos[:, None] * inv_freq[None, :]
    cos, sin = jnp.cos(ang), jnp.sin(ang)
    cosf = jnp.tile(jnp.concatenate([cos, cos], axis=-1), (1, N_HEADS))
    sins = jnp.tile(jnp.concatenate([-sin, sin], axis=-1), (1, N_HEADS))
    log_gamma = jnp.log(1.0 - jnp.exp2(-5.0 - jnp.arange(N_HEADS, dtype=F32)))
    i = jnp.arange(C, dtype=F32)
    dpos = i[:, None] - i[None, :]
    dm = jnp.where(dpos >= 0, jnp.exp(jnp.maximum(dpos, 0.0)[None] * log_gamma[:, None, None]), 0.0)
    lanes = lambda t: jnp.repeat(t, HEAD_DIM, axis=-1)
    qd = lanes(jnp.exp((i + 1.0)[:, None] * log_gamma[None, :]))
    kd = lanes(jnp.exp((C - 1.0 - i)[:, None] * log_gamma[None, :]))
    cd = lanes(jnp.exp(C * log_gamma)[None, :])
    args = (u, cosf, sins, dm.reshape(N_HEADS * C, C), qd, kd, cd,
            norm_g.reshape(1, -1).astype(F32), norm_b.reshape(1, -1).astype(F32), _block_ones())
    in_specs = ([_row_spec(C, 4 * MIX),
                 pl.BlockSpec((C, MIX), lambda b, c: (c, 0)),
                 pl.BlockSpec((C, MIX), lambda b, c: (c, 0))]
                + [_const_spec(t.shape) for t in args[3:]])
    return args, in_specs


def _recurrent_kernel(*refs, layer, n_in):
    n_h, n_t, n_r = n_in
    h_in, t_in, r_in = refs[:n_h], refs[n_h:n_h + n_t], refs[n_h + n_t:n_h + n_t + n_r]
    o_h, o_t, o_r, state_h, state_t, state_r, prev_r = refs[n_h + n_t + n_r:]
    streams = [(_rwkv_steps(*r_in, o_r, state_r, prev_r), 1),
               (_hgrn_steps(*h_in, o_h, state_h, layer=layer), RWKV_C // HGRN_C),
               (_ret_steps(*t_in, o_t, state_t), 1)]
    live = True
    while live:
        live = False
        for steps, per_round in streams:
            for _ in range(per_round):
                live = (next(steps, "done") != "done") or live


def _recurrent_call(u_hgrn, lb_logits, hgrn_norm, layer, u_ret, ret_norm_g, ret_norm_b,
                    u_rwkv, *rwkv_params):
    bsz, seq, _ = u_hgrn.shape
    tb = min(REC_TB, seq)
    h_args = (u_hgrn, lb_logits.astype(F32), hgrn_norm.reshape(1, -1).astype(F32), _block_ones())
    t_args, t_specs = _ret_operands(u_ret, ret_norm_g, ret_norm_b)
    vec = lambda t: t.reshape(1, -1).astype(F32)
    mu, w0, w_up, a0, a_up, g_up, k_k, k_a, r_k, norm_g, norm_b = rwkv_params
    r_args = (u_rwkv, vec(mu), vec(w0), w_up, vec(a0), a_up, g_up, vec(k_k), vec(k_a), vec(r_k),
              vec(norm_g), vec(norm_b), _block_ones())
    specs = lambda args: ([_row_spec(tb, 4 * MIX)] + [_const_spec(t.shape) for t in args[1:]])
    out = jax.ShapeDtypeStruct((bsz, seq, MIX), F32)
    state = pltpu.VMEM((MIX, MIX), F32)
    return pl.pallas_call(
        functools.partial(_recurrent_kernel, layer=layer,
                          n_in=(len(h_args), len(t_args), len(r_args))),
        grid=(bsz, seq // tb),
        in_specs=specs(h_args) + t_specs + specs(r_args),
        out_specs=[_row_spec(tb, MIX)] * 3,
        out_shape=[out] * 3,
        scratch_shapes=[state, state, state, pltpu.VMEM((1, 4 * MIX), F32)],
        compiler_params=_params(("parallel", "arbitrary")),
    )(*h_args, *t_args, *r_args)


DENSE_TM = 512
FFN_TF = 256


def _rmsnorm(x, g):
    return x * lax.rsqrt(jnp.mean(x * x, axis=-1, keepdims=True) + NORM_EPS) * g


def _resident(shape):
    nd = len(shape)
    return pl.BlockSpec(shape, lambda b, c: (0,) * nd, pipeline_mode=pl.Buffered(1))


def _inproj_kernel(h_ref, g_ref, w_ref, onehot_ref, q_ref, gate_ref, kc_ref, vc_ref, ks_ref,
                   vs_ref, kw_ref, vw_ref, hg_ref, rt_ref, rw_ref):
    xn = _rmsnorm(h_ref[...], g_ref[...]).astype(BF16)
    nsa = jnp.dot(xn, w_ref[:, 0:NSA_PAD], preferred_element_type=F32)
    tm = nsa.shape[0]
    col = lambda j: nsa[:, MIX + j * HEAD_DIM:MIX + (j + 1) * HEAD_DIM]
    q_ref[...] = nsa[:, 0:MIX]
    kc_ref[...] = col(0)
    vc_ref[...] = col(1)
    gate_ref[...] = nsa[:, MIX + 6 * HEAD_DIM:NSA_PAD]
    ks_ref[...] = jnp.concatenate(
        [onehot_ref[...], col(2).astype(BF16),
         jnp.zeros((tm, ks_ref.shape[1] - NSA_ONEHOT - HEAD_DIM), BF16)], axis=-1)
    kw_ref[...] = col(4).astype(BF16)
    vs_ref[...] = jnp.concatenate([col(3).T, _ones_row_block(tm)], axis=0).astype(BF16)
    vw_t = jnp.concatenate([col(5).T, _ones_row_block(tm)], axis=0).astype(BF16)
    tw = vw_ref.shape[-1]
    for t in range(tm // tw):
        vw_ref[t] = vw_t[:, t * tw:(t + 1) * tw]
    off = NSA_PAD
    for ref in (hg_ref, rt_ref, rw_ref):
        ref[...] = jnp.dot(xn, w_ref[:, off:off + 4 * MIX], preferred_element_type=F32)
        off += 4 * MIX


def _inproj_call(h, g, w_pad):
    bsz, seq, _ = h.shape
    tm = min(NSA_TK, seq)
    tw = min(NSA_TQ, seq)
    local = jnp.arange(tm) // NSA_SEL_BLOCK
    onehot = (local[:, None] == jnp.arange(NSA_ONEHOT)[None, :]).astype(BF16)
    struct = jax.ShapeDtypeStruct
    outs = [
        (struct((bsz, seq, MIX), F32), _row_spec(tm, MIX)),
        (struct((bsz, seq, 128), F32), _row_spec(tm, 128)),
        (struct((bsz, seq, HEAD_DIM), F32), _row_spec(tm, HEAD_DIM)),
        (struct((bsz, seq, HEAD_DIM), F32), _row_spec(tm, HEAD_DIM)),
        (struct((bsz, seq, 128), BF16), _row_spec(tm, 128)),
        (struct((bsz, seq // tm, NSA_VROWS, tm), BF16),
         pl.BlockSpec((None, None, NSA_VROWS, tm), lambda b, c: (b, c, 0, 0))),
        (struct((bsz, seq, HEAD_DIM), BF16), _row_spec(tm, HEAD_DIM)),
        (struct((bsz, seq // tw, NSA_VROWS, tw), BF16),
         pl.BlockSpec((None, tm // tw, NSA_VROWS, tw), lambda b, c: (b, c, 0, 0))),
    ] + [(struct((bsz, seq, 4 * MIX), F32), _row_spec(tm, 4 * MIX))] * 3
    res = pl.pallas_call(
        _inproj_kernel,
        grid=(bsz, seq // tm),
        in_specs=[_row_spec(tm, D_MODEL), _resident((1, D_MODEL)), _resident(w_pad.shape),
                  _resident(onehot.shape)],
        out_specs=[spec for _, spec in outs],
        out_shape=[shape for shape, _ in outs],
        compiler_params=_params(("parallel", "parallel")),
    )(h, g.reshape(1, -1), w_pad, onehot)
    return tuple(res[:8]), res[8], res[9], res[10]


def _merge_kernel(h_ref, b0_ref, b1_ref, b2_ref, b3_ref, g_ref, wg_ref, bg_ref, wb_ref, wo_ref,
                  o_ref):
    h = h_ref[...]
    xn = _rmsnorm(h, g_ref[...]).astype(BF16)
    merged = None
    for m, b_ref in enumerate((b0_ref, b1_ref, b2_ref, b3_ref)):
        gate = _sigmoid(jnp.dot(xn, wg_ref[m], preferred_element_type=F32) + bg_ref[m])
        term = gate * _bdot(b_ref[...], wb_ref[m])
        merged = term if merged is None else merged + term
    o_ref[...] = h + _bdot(merged, wo_ref[...])


def _merge_call(h, branches, g, w_gate, b_gate, w_branch, w_out):
    bsz, seq, _ = h.shape
    tm = min(DENSE_TM, seq)
    consts = (g.reshape(1, -1), w_gate, b_gate.reshape(4, 1, D_MODEL), w_branch, w_out)
    return pl.pallas_call(
        _merge_kernel,
        grid=(bsz, seq // tm),
        in_specs=([_row_spec(tm, D_MODEL)] + [_row_spec(tm, MIX)] * 4
                  + [_resident(t.shape) for t in consts]),
        out_specs=_row_spec(tm, D_MODEL),
        out_shape=jax.ShapeDtypeStruct(h.shape, F32),
        compiler_params=_params(("parallel", "parallel")),
    )(h, *branches, *consts)


def _ffn_kernel(h_ref, g_ref, wg_ref, wu_ref, wd_ref, o_ref):
    h = h_ref[...]
    hn = _rmsnorm(h, g_ref[...]).astype(BF16)
    acc = h
    for f in range(0, D_FF, FFN_TF):
        gate = jnp.dot(hn, wg_ref[:, f:f + FFN_TF], preferred_element_type=F32)
        up = jnp.dot(hn, wu_ref[:, f:f + FFN_TF], preferred_element_type=F32)
        acc = acc + _bdot(_silu(gate) * up, wd_ref[f:f + FFN_TF, :])
    o_ref[...] = acc


def _ffn_call(h, g, w_gate, w_up, w_down):
    bsz, seq, _ = h.shape
    tm = min(DENSE_TM, seq)
    consts = (g.reshape(1, -1), w_gate, w_up, w_down)
    return pl.pallas_call(
        _ffn_kernel,
        grid=(bsz, seq // tm),
        in_specs=[_row_spec(tm, D_MODEL)] + [_resident(t.shape) for t in consts],
        out_specs=_row_spec(tm, D_MODEL),
        out_shape=jax.ShapeDtypeStruct(h.shape, F32),
        compiler_params=_params(("parallel", "parallel")),
    )(h, *consts)


def _ple_kernel(h_ref, p_ref, g_ref, wg_ref, wp_ref, gf_ref, o_ref, *, final_norm):
    h = h_ref[...]
    hp = _rmsnorm(h, g_ref[...])
    out = h + _sigmoid(_bdot(hp, wg_ref[...])) * _bdot(p_ref[...], wp_ref[...])
    if final_norm:
        out = _rmsnorm(out, gf_ref[...])
    o_ref[...] = out


def _ple_call(h, p, layer, g, w_gate, w_proj, g_final, final_norm):
    bsz, seq, _ = h.shape
    tm = min(DENSE_TM, seq)
    consts = (g.reshape(1, -1), w_gate, w_proj, g_final.reshape(1, -1))
    return pl.pallas_call(
        functools.partial(_ple_kernel, final_norm=final_norm),
        grid=(bsz, seq // tm),
        in_specs=([_row_spec(tm, D_MODEL),
                   pl.BlockSpec((None, None, tm, PLE_DIM), lambda b, c: (layer, b, c, 0))]
                  + [_resident(t.shape) for t in consts]),
        out_specs=_row_spec(tm, D_MODEL),
        out_shape=jax.ShapeDtypeStruct(h.shape, F32),
        compiler_params=_params(("parallel", "parallel")),
    )(h, p, *consts)


def kernel(x, p, norm_mix, w_in, nsa_pos_k, nsa_pos_v, nsa_cmp_k1, nsa_cmp_k2, nsa_cmp_v1,
           nsa_cmp_v2, hgrn_lb_logits, hgrn_norm, ret_norm_g, ret_norm_b, rwkv_mu, rwkv_w0,
           rwkv_w_up, rwkv_a0, rwkv_a_up, rwkv_g_up, rwkv_k_k, rwkv_k_a, rwkv_r_k, rwkv_norm_g,
           rwkv_norm_b, w_branch, w_gate, b_gate, w_out, norm_ffn, w_ffn_gate, w_ffn_up,
           w_ffn_down, norm_ple, w_ple_gate, w_ple_proj, norm_final):
    depth = w_in.shape[0]
    w_in_pad = jnp.concatenate(
        [w_in[:, :, :NSA_WIDTH], jnp.zeros((depth, D_MODEL, NSA_PAD - NSA_WIDTH), w_in.dtype),
         w_in[:, :, NSA_WIDTH:]], axis=-1).astype(BF16)
    bf = lambda t: t.astype(BF16)
    h = x
    for i in range(depth):
        nsa_in, u_hgrn, u_ret, u_rwkv = _inproj_call(h, norm_mix[i], w_in_pad[i])
        o_hgrn, o_ret, o_rwkv = _recurrent_call(
            u_hgrn, hgrn_lb_logits, hgrn_norm[i], i, u_ret, ret_norm_g[i], ret_norm_b[i],
            u_rwkv, rwkv_mu[i], rwkv_w0[i], rwkv_w_up[i], rwkv_a0[i], rwkv_a_up[i], rwkv_g_up[i],
            rwkv_k_k[i], rwkv_k_a[i], rwkv_r_k[i], rwkv_norm_g[i], rwkv_norm_b[i])
        branches = (
            _nsa_call(*nsa_in, nsa_pos_k[i], nsa_pos_v[i], nsa_cmp_k1[i], nsa_cmp_k2[i],
                      nsa_cmp_v1[i], nsa_cmp_v2[i]),
            o_hgrn, o_ret, o_rwkv,
        )
        h = _merge_call(h, branches, norm_mix[i], bf(w_gate[i]), b_gate[i], bf(w_branch[i]),
                        bf(w_out[i]))
        h = _ffn_call(h, norm_ffn[i], bf(w_ffn_gate[i]), bf(w_ffn_up[i]), bf(w_ffn_down[i]))
        h = _ple_call(h, p, i, norm_ple[i], bf(w_ple_gate[i]), bf(w_ple_proj[i]), norm_final,
                      final_norm=(i == depth - 1))
    return h
```

```python
import functools
import math

import jax
import jax.numpy as jnp
from jax import lax
from jax.experimental import pallas as pl
from jax.experimental.pallas import tpu as pltpu

F32 = jnp.float32
BF16 = jnp.bfloat16

D_MODEL = 1024
N_HEADS = 4
HEAD_DIM = 64
MIX = N_HEADS * HEAD_DIM
D_FF = 2816
PLE_DIM = 256
NORM_EPS = 1e-6
NEG_BIG = -1e30
POS_BIG = 1e30
GATE_FLOOR = 1e-20

NSA_CMP_BLOCK = 32
NSA_CMP_STRIDE = 16
NSA_SEL_BLOCK = 64
NSA_TOP_N = 16
NSA_WINDOW = 512
NSA_CMP_HIDDEN = 128
NSA_WIDTH = 652
NSA_PAD = 768

RET_ROPE_BASE = 10000.0
RET_GN_EPS = 1e-5
RWKV_GN_EPS = 64e-5

VMEM_LIMIT = 56 * 1024 * 1024


def _bdot(a, b):
    return jnp.dot(a.astype(BF16), b.astype(BF16), preferred_element_type=F32)


def _bdot_nt(a, b):
    return lax.dot_general(a.astype(BF16), b.astype(BF16), (((1,), (1,)), ((), ())),
                           preferred_element_type=F32)


def _sigmoid(x):
    return 0.5 * jnp.tanh(0.5 * x) + 0.5


def _silu(x):
    return x * _sigmoid(x)


def _params(sem):
    return pltpu.CompilerParams(dimension_semantics=sem, vmem_limit_bytes=VMEM_LIMIT)


def _row_spec(tile, width):
    return pl.BlockSpec((None, tile, width), lambda b, c: (b, c, 0))


def _const_spec(shape):
    nd = len(shape)
    return pl.BlockSpec(shape, lambda b, c: (0,) * nd)


REC_TB = 256
RWKV_C = 32
REC_HGRN_START = 0
REC_RET_START = 12


def _split_dot(a_bf16, x, parts):
    total = None
    rest = x
    for _ in range(parts):
        piece = rest.astype(BF16)
        rest = rest - piece.astype(F32)
        term = jnp.dot(a_bf16, piece, preferred_element_type=F32)
        total = term if total is None else total + term
    return total


def _head_sum_mxu(x, ones_t):
    hi = x.astype(BF16)
    lo = (x - hi.astype(F32)).astype(BF16)
    return (jnp.dot(hi, ones_t, preferred_element_type=F32)
            + jnp.dot(lo, ones_t, preferred_element_type=F32))


def _rwkv_steps(u_ref, mu_ref, w0_ref, wup_ref, a0_ref, aup_ref, gup_ref, kk_ref, ka_ref,
                rk_ref, ng_ref, nb_ref, ones_ref, o_ref, state_ref, prev_ref):
    c = pl.program_id(1)
    ones = ones_ref[...]

    @pl.when(c == 0)
    def _():
        state_ref[...] = jnp.zeros_like(state_ref)
        prev_ref[...] = jnp.zeros_like(prev_ref)

    u = u_ref[...]
    tb = u.shape[0]
    row = lax.broadcasted_iota(jnp.int32, u.shape, 0)
    u_prev = jnp.where(row == 0, prev_ref[...], pltpu.roll(u, 1, axis=0))
    prev_ref[...] = u[tb - 1:tb, :]
    xs = u + mu_ref[...] * (u_prev - u)
    r = xs[:, 0:MIX]
    k = xs[:, MIX:2 * MIX]
    v = xs[:, 2 * MIX:3 * MIX]
    w_lo = xs[:, 3 * MIX:3 * MIX + 64]
    a_lo = xs[:, 3 * MIX + 64:3 * MIX + 128]
    g_lo = xs[:, 3 * MIX + 128:3 * MIX + 256]

    logw = -math.exp(-0.5) * _sigmoid(w0_ref[...] + _bdot(jnp.tanh(w_lo), wup_ref[...]))
    a = _sigmoid(a0_ref[...] + _bdot(a_lo, aup_ref[...]))
    g = _bdot(_sigmoid(g_lo), gup_ref[...])
    kk = k * kk_ref[...]
    kk = kk * lax.rsqrt(jnp.maximum(_head_sum_mxu(kk * kk, ones), 1e-24))
    k2 = k * (1.0 + (a - 1.0) * ka_ref[...])
    alpha = -kk
    beta = kk * a
    bonus = _head_sum_mxu(r * k2 * rk_ref[...], ones) * v

    C = RWKV_C
    ti = lax.broadcasted_iota(jnp.int32, (tb, tb), 0)
    si = lax.broadcasted_iota(jnp.int32, (tb, tb), 1)
    same_chunk = (ti // C) == (si // C)
    prefix = (same_chunk & (ti >= si)).astype(BF16)
    cum = _split_dot(prefix, logw, 3)
    cum_last = _split_dot(same_chunk.astype(BF16), logw, 3)
    gam_all = jnp.exp(cum_last)
    e_inv = jnp.exp(-cum)
    e_last = jnp.exp(cum_last - cum)
    ag_all = alpha * jnp.exp(cum - logw)
    rg_all = r * jnp.exp(cum)
    bi_all = beta * e_inv
    ki_all = k2 * e_inv
    bl_all = beta * e_last
    kl_all = k2 * e_last

    hc = N_HEADS * C
    row_head = lax.broadcasted_iota(jnp.int32, (hc, MIX), 0) // C
    lane_head = lax.broadcasted_iota(jnp.int32, (hc, MIX), 1) // HEAD_DIM
    own = row_head == lane_head

    def stack(x):
        return jnp.where(own, jnp.concatenate([x] * N_HEADS, axis=0), 0.0).astype(BF16)

    rr = lax.broadcasted_iota(jnp.int32, (hc, hc), 0)
    cc = lax.broadcasted_iota(jnp.int32, (hc, hc), 1)
    strict = rr > cc
    incl = rr >= cc
    eye_hc = (rr == cc).astype(F32)
    kr = lax.broadcasted_iota(jnp.int32, (MIX, MIX), 0)
    kc_ = lax.broadcasted_iota(jnp.int32, (MIX, MIX), 1)
    eye_k = kr == kc_

    def chunk_affine(ch, out):
        sl = slice(ch * C, (ch + 1) * C)
        ag, rg, bi, ki, bl, kl, vm = (stack(t[sl]) for t in (ag_all, rg_all, bi_all, ki_all,
                                                               bl_all, kl_all, v))
        aa = _bdot_nt(jnp.concatenate([ag, rg], axis=0), jnp.concatenate([bi, ki], axis=0))
        yield
        a_ab = jnp.where(strict, aa[:hc, :hc], 0.0)
        a_ak = jnp.where(strict, aa[:hc, hc:], 0.0)
        a_rb = jnp.where(incl, aa[hc:, :hc], 0.0)
        a_rk = jnp.where(incl, aa[hc:, hc:], 0.0)
        t_inv = eye_hc + a_ab
        pw = _bdot(a_ab, a_ab)
        av = _bdot(a_ak, vm)
        yield
        for _ in range(int(math.log2(C)) - 2):
            both = _bdot(pw, jnp.concatenate([t_inv, pw], axis=-1))
            t_inv = t_inv + both[:, :hc]
            pw = both[:, hc:]
            yield
        t_inv = t_inv + _bdot(pw, t_inv)
        yield
        w12 = _bdot(t_inv, jnp.concatenate([ag.astype(F32), av], axis=-1))
        yield
        ry = _bdot(a_rb, w12)
        rq = rg.astype(F32) + ry[:, :MIX]
        y0 = ry[:, MIX:] + _bdot(a_rk, vm)
        mn = _bdot(bl.T, w12)
        m_mat = jnp.where(eye_k, gam_all[ch * C:ch * C + 1, :], 0.0) + mn[:, :MIX]
        n_mat = mn[:, MIX:] + _bdot(kl.T, vm)
        out.append((jnp.concatenate([rq, m_mat], axis=0).astype(BF16), y0, n_mat))

    yield
    affine = [[] for _ in range(tb // C)]
    chains = [chunk_affine(ch, affine[ch]) for ch in range(tb // C)]
    live = True
    while live:
        live = False
        for chain in chains:
            live = (next(chain, "done") != "done") or live
        yield

    state = state_ref[...]
    y_chunks = []
    for ((lhs, y0, n_mat),) in affine:
        prod = jnp.dot(lhs, state.astype(BF16), preferred_element_type=F32)
        ym = prod[:hc] + y0
        y_chunks.append(ym[0:C] + ym[C:2 * C] + ym[2 * C:3 * C] + ym[3 * C:4 * C])
        state = prod[hc:] + n_mat
        yield
    state_ref[...] = state
    y = jnp.concatenate(y_chunks, axis=0)

    mean = _head_sum_mxu(y, ones) * (1.0 / HEAD_DIM)
    yc = y - mean
    var = _head_sum_mxu(yc * yc, ones) * (1.0 / HEAD_DIM)
    yn = yc * lax.rsqrt(var + RWKV_GN_EPS) * ng_ref[...] + nb_ref[...]
    o_ref[...] = (yn + bonus) * g


NSA_TQ = 256
NSA_TK = 512
NSA_VROWS = 80
NSA_UNROLL = 2
NSA_CMP_PARTS = 4
NSA_ONEHOT = 16
GROUP = NSA_CMP_STRIDE


def _ones_row_block(width):
    row = lax.broadcasted_iota(jnp.int32, (NSA_VROWS - HEAD_DIM, width), 0)
    return (row == 0).astype(F32)


def _nsa_compress_kernel(xk_ref, xv_ref, pk_ref, pv_ref, k1_ref, k2_ref, v1_ref, v2_ref, ov_ref,
                         kc_ref, vc_ref):
    ng = kc_ref.shape[0]

    def compress(x_ref, pos_ref, w1_ref, w2_ref):
        first = jnp.zeros((ng, k2_ref.shape[0]), F32)
        second = jnp.zeros((ng, k2_ref.shape[0]), F32)
        for j in range(GROUP):
            xj = x_ref[pl.ds(j, ng, stride=GROUP), :]
            lo, hi = j * HEAD_DIM, (GROUP + j) * HEAD_DIM
            first = first + _bdot(xj + pos_ref[j:j + 1, :], w1_ref[lo:lo + HEAD_DIM, :])
            second = second + _bdot(xj + pos_ref[GROUP + j:GROUP + j + 1, :],
                                    w1_ref[hi:hi + HEAD_DIM, :])
        hid = first + pltpu.roll(second, ng - 1, axis=0)
        return _bdot(_silu(hid), w2_ref[...])

    kc = compress(xk_ref, pk_ref, k1_ref, k2_ref)
    vc = compress(xv_ref, pv_ref, v1_ref, v2_ref)
    kc_ref[...] = kc.astype(BF16)
    ng = vc.shape[0]
    vc_ref[0:NSA_VROWS, :] = jnp.concatenate([vc.T, _ones_row_block(ng)], axis=0).astype(BF16)
    vc_ref[NSA_VROWS:, :] = ov_ref[...]


def _nsa_compress_call(xk, xv, pos_k, pos_v, k1, k2, v1, v2, overlap_t):
    bsz, seq, width = xk.shape
    ng = seq // GROUP
    n_sel = overlap_t.shape[0]
    args = (xk, xv, pos_k, pos_v, k1.astype(BF16), k2.astype(BF16),
            v1.astype(BF16), v2.astype(BF16), overlap_t)
    blk = pl.BlockSpec((None, seq, width), lambda b: (b, 0, 0))
    const = lambda t: pl.BlockSpec(t.shape, lambda b: (0,) * t.ndim)
    return pl.pallas_call(
        _nsa_compress_kernel,
        grid=(bsz,),
        in_specs=[blk, blk] + [const(t) for t in args[2:]],
        out_specs=[pl.BlockSpec((None, ng, HEAD_DIM), lambda b: (b, 0, 0)),
                   pl.BlockSpec((None, NSA_VROWS + n_sel, ng), lambda b: (b, 0, 0))],
        out_shape=[jax.ShapeDtypeStruct((bsz, ng, HEAD_DIM), BF16),
                   jax.ShapeDtypeStruct((bsz, NSA_VROWS + n_sel, ng), BF16)],
        compiler_params=_params(("parallel",)),
    )(*args)


REMOVED = -3e38


def _nsa_kernel(q_ref, g_ref, ks_ref, vs_ref, kw_ref, vw_ref, kc_ref, vc_ref, o_ref,
                selbias_ref, *, n_top):
    c = pl.program_id(1)
    tq = q_ref.shape[0]
    tk = ks_ref.shape[1]
    ct = kc_ref.shape[0]
    ns = selbias_ref.shape[0]
    t0 = c * tq
    cols = N_HEADS * tq

    q_t = (q_ref[...] * (HEAD_DIM ** -0.5)).T
    qs = jnp.concatenate([q_t[h * HEAD_DIM:(h + 1) * HEAD_DIM] for h in range(N_HEADS)],
                         axis=1).astype(BF16)
    t_q = t0 + lax.broadcasted_iota(jnp.int32, (1, tq), 1)
    t_col = jnp.concatenate([t_q] * N_HEADS, axis=1)

    def online(carry, s, v_aug):
        m, acc = carry
        m_new = jnp.maximum(m, jnp.max(s, axis=0, keepdims=True))
        p = jnp.exp(s - m_new)
        acc = jnp.exp(m - m_new) * acc + jnp.dot(v_aug, p.astype(BF16), preferred_element_type=F32)
        return m_new, acc

    def normalise(acc):
        return acc[0:HEAD_DIM] / acc[HEAD_DIM:HEAD_DIM + 1]

    init_aug = (jnp.full((1, cols), NEG_BIG, F32), jnp.zeros((vs_ref.shape[1], cols), F32))

    vrows = vs_ref.shape[1]

    def cmp_quarters(quarters):
        rows = quarters * (ct // NSA_CMP_PARTS)
        n_row = lax.broadcasted_iota(jnp.int32, (rows, 1), 0)
        valid = (n_row * NSA_CMP_STRIDE + (NSA_CMP_BLOCK - 1)) <= t_col
        s = jnp.where(valid, jnp.dot(kc_ref[0:rows, :], qs, preferred_element_type=F32), NEG_BIG)
        m = jnp.max(s, axis=0, keepdims=True)
        p = jnp.exp(s - m)
        p_hi = p.astype(BF16)
        p_lo = (p - p_hi.astype(F32)).astype(BF16)
        res = jnp.dot(vc_ref[:, 0:rows], p_hi, preferred_element_type=F32)
        imp_lo = jnp.dot(vc_ref[vrows:, 0:rows], p_lo, preferred_element_type=F32)
        return m, jnp.concatenate([res[:vrows], res[vrows:] + imp_lo], axis=0)

    last_valid = (t0 + tq - NSA_CMP_BLOCK) // NSA_CMP_STRIDE
    quarter = jnp.clip(last_valid // (ct // NSA_CMP_PARTS), 0, NSA_CMP_PARTS - 1)
    m_c, res_c = lax.switch(quarter, [functools.partial(cmp_quarters, i + 1)
                                      for i in range(NSA_CMP_PARTS)])
    inv_c = jnp.where(m_c > 0.5 * NEG_BIG, 1.0 / res_c[HEAD_DIM:HEAD_DIM + 1], 0.0)
    o_cmp = res_c[0:HEAD_DIM] * inv_c
    imp4 = res_c[vrows:] * inv_c
    imp = imp4[:, 0:tq]
    for h in range(1, N_HEADS):
        imp = imp + imp4[:, h * tq:(h + 1) * tq]

    def window_steps(out):
        tw = kw_ref.shape[1]
        n_wt = (NSA_WINDOW + tq) // tw
        jw = (t0 - NSA_WINDOW) // tw
        wkey_row = lax.broadcasted_iota(jnp.int32, (tw, 1), 0)
        s_parts = []
        w_tiles = []
        for i in range(n_wt):
            exists = (jw + i) >= 0
            w_tiles.append(jnp.maximum(jw + i, 0))
            dist = t_q - ((jw + i) * tw + wkey_row)
            s = jnp.dot(kw_ref[w_tiles[i]], qs, preferred_element_type=F32)
            if i == 0:
                bias = jnp.where((dist < NSA_WINDOW) & exists, 0.0, NEG_BIG)
                s = s + jnp.concatenate([bias] * N_HEADS, axis=1)
            elif i == n_wt - 1:
                bias = jnp.where(dist >= 0, 0.0, NEG_BIG)
                s = s + jnp.concatenate([bias] * N_HEADS, axis=1)
            else:
                s = s + jnp.where(exists, 0.0, NEG_BIG)
            s_parts.append(s)
            yield
        m_w = s_parts[0].max(axis=0, keepdims=True)
        for sp in s_parts[1:]:
            m_w = jnp.maximum(m_w, sp.max(axis=0, keepdims=True))
        yield
        acc_w = jnp.zeros((vw_ref.shape[1], cols), F32)
        for i, sp in enumerate(s_parts):
            p = jnp.exp(sp - m_w)
            acc_w = acc_w + jnp.dot(vw_ref[w_tiles[i]], p.astype(BF16),
                                    preferred_element_type=F32)
            yield
        out.append(normalise(acc_w))

    blk = lax.broadcasted_iota(jnp.int32, (ns, 1), 0)
    blk_f = blk.astype(F32)
    cur = t_q // NSA_SEL_BLOCK
    forced = (blk == 0) | (blk == cur) | (blk == cur - 1)
    score = jnp.where(forced, POS_BIG, jnp.where(blk <= cur, imp, NEG_BIG))
    chosen = jnp.zeros((ns, tq), jnp.bool_)
    window_out = []
    window = window_steps(window_out)
    for _ in range(n_top):
        best = jnp.max(score, axis=0, keepdims=True)
        first = jnp.min(jnp.where(score == best, blk_f, float(ns)), axis=0, keepdims=True)
        hit = blk_f == first
        chosen = chosen | hit
        score = jnp.where(hit, REMOVED, score)
        next(window, None)
    for _ in window:
        pass
    (o_win,) = window_out
    selbias_ref[...] = jnp.where(chosen, 0.0, NEG_BIG)

    per_tile = tk // NSA_SEL_BLOCK
    key_row = lax.broadcasted_iota(jnp.int32, (tk, 1), 0)

    pad_rows = jnp.zeros((ks_ref.shape[2] - HEAD_DIM - NSA_ONEHOT, cols), BF16)
    bias_pad = jnp.zeros((NSA_ONEHOT - per_tile, tq), F32)

    def sel_scores(j, live=None):
        start = pl.multiple_of(j * per_tile, per_tile)
        bias = selbias_ref[pl.ds(start, per_tile), :]
        if live is not None:
            bias = jnp.where(live, bias, NEG_BIG)
        bias = jnp.concatenate([bias, bias_pad], axis=0)
        bias = jnp.concatenate([bias.astype(BF16)] * N_HEADS, axis=1)
        rhs = jnp.concatenate([bias, qs, pad_rows], axis=0)
        return jnp.dot(ks_ref[j], rhs, preferred_element_type=F32)

    j_last = t0 // tk

    def sel_group(i, carry):
        tiles = []
        for u in range(NSA_UNROLL):
            j = i * NSA_UNROLL + u
            tiles.append((jnp.minimum(j, j_last - 1), None if u == 0 else j < j_last))
        scores = [sel_scores(j, live) for j, live in tiles]
        for (j, _), s in zip(tiles, scores):
            carry = online(carry, s, vs_ref[j])
        return carry

    carry = lax.fori_loop(0, (j_last + NSA_UNROLL - 1) // NSA_UNROLL, sel_group, init_aug)
    causal = jnp.where((j_last * tk + key_row) <= t_q, 0.0, NEG_BIG)
    _, acc_s = online(carry, sel_scores(j_last) + jnp.concatenate([causal] * N_HEADS, axis=1),
                      vs_ref[j_last])
    o_sel = normalise(acc_s)

    gates = _sigmoid(g_ref[...]).T

    def gate_row(branch):
        return jnp.concatenate([gates[branch * N_HEADS + h:branch * N_HEADS + h + 1, :]
                                for h in range(N_HEADS)], axis=1)

    out = gate_row(0) * o_cmp + gate_row(1) * o_sel + gate_row(2) * o_win
    o_ref[...] = jnp.concatenate([out[:, h * tq:(h + 1) * tq].T for h in range(N_HEADS)], axis=-1)


def _nsa_call(q, gates, k_cmp, v_cmp, k_sel, v_sel, k_win, v_win, pos_k, pos_v, k1, k2, v1, v2):
    bsz, seq, _ = q.shape
    ng = seq // GROUP
    n_sel = seq // NSA_SEL_BLOCK
    n_top = min(NSA_TOP_N, n_sel)
    cmp_start = jnp.arange(ng) * NSA_CMP_STRIDE
    sel_start = jnp.arange(n_sel) * NSA_SEL_BLOCK
    overlap = ((cmp_start[:, None] < sel_start[None, :] + NSA_SEL_BLOCK)
               & (cmp_start[:, None] + NSA_CMP_BLOCK > sel_start[None, :])).astype(BF16)
    kc, vc_stack = _nsa_compress_call(k_cmp, v_cmp, pos_k, pos_v, k1, k2, v1, v2, overlap.T)
    tq = min(NSA_TQ, seq)
    tk, tw = v_sel.shape[-1], v_win.shape[-1]
    operands = (q, gates,
                k_sel.reshape(bsz, seq // tk, tk, k_sel.shape[-1]), v_sel,
                k_win.reshape(bsz, seq // tw, tw, HEAD_DIM), v_win,
                kc, vc_stack)
    per_batch = lambda t: pl.BlockSpec((None,) + t.shape[1:],
                                       lambda b, c: (b,) + (0,) * (t.ndim - 1))
    return pl.pallas_call(
        functools.partial(_nsa_kernel, n_top=n_top),
        grid=(bsz, seq // tq),
        in_specs=[_row_spec(tq, MIX), _row_spec(tq, 128)] + [per_batch(t) for t in operands[2:]],
        out_specs=_row_spec(tq, MIX),
        out_shape=jax.ShapeDtypeStruct((bsz, seq, MIX), F32),
        scratch_shapes=[pltpu.VMEM((n_sel, tq), F32)],
        compiler_params=_params(("parallel", "arbitrary")),
    )(*operands)


HGRN_C = 32


def _hgrn_steps(u_ref, lbl_ref, ng_ref, ones_ref, o_ref, state_ref, *, layer):
    c = pl.program_id(1)

    @pl.when(c == 0)
    def _():
        state_ref[...] = jnp.zeros_like(state_ref)

    logits = lbl_ref[...]
    ex = jnp.exp(logits - jnp.max(logits, axis=0, keepdims=True))
    soft = ex / jnp.sum(ex, axis=0, keepdims=True)
    lb = jnp.sum(soft[0:layer + 1], axis=0, keepdims=True) - soft[0:1]

    u = u_ref[...]
    tb = u.shape[0]
    q = _silu(u[:, 0:MIX])
    f = lb + (1.0 - lb) / (1.0 + jnp.exp(-u[:, MIX:2 * MIX]))
    logf = jnp.log(jnp.maximum(f, GATE_FLOOR))
    k = 1.0 - f
    v = u[:, 2 * MIX:3 * MIX]
    og = u[:, 3 * MIX:4 * MIX]

    C = HGRN_C
    SUB = 8
    ones_bd = ones_ref[...]
    ti = lax.broadcasted_iota(jnp.int32, (tb, tb), 0)
    si = lax.broadcasted_iota(jnp.int32, (tb, tb), 1)
    same_chunk = (ti // C) == (si // C)
    b_all = _split_dot((same_chunk & (ti >= si)).astype(BF16), logf, 3)
    b_last_all = _split_dot(same_chunk.astype(BF16), logf, 3)
    b2_all = b_all * math.log2(math.e)
    qe_all = q * jnp.exp(b_all)
    kd_all = k * jnp.exp(b_last_all - b_all)
    g_last_all = jnp.exp(b_last_all)
    row8 = lax.broadcasted_iota(jnp.int32, (SUB, 1), 0)
    hr = lax.broadcasted_iota(jnp.int32, (MIX, MIX), 0) // HEAD_DIM
    hc_ = lax.broadcasted_iota(jnp.int32, (MIX, MIX), 1) // HEAD_DIM
    same_head = hr == hc_

    def chunk_intra(ch, out):
        sl = slice(ch * C, (ch + 1) * C)
        qc, kc, vc, b2 = q[sl], k[sl], v[sl], b2_all[sl]
        pieces = []
        for s in range(C):
            r0 = (s // SUB) * SUB
            pm = qc[r0:] * (kc[s:s + 1, :] * jnp.exp2(b2[r0:] - b2[s:s + 1, :]))
            top = jnp.where(row8 + r0 >= s, pm[0:SUB], 0.0)
            pieces.append(top if C - r0 == SUB else jnp.concatenate([top, pm[SUB:]], axis=0))
        attn = jnp.dot(jnp.concatenate(pieces, axis=0).astype(BF16), ones_bd,
                       preferred_element_type=F32)
        outer = jnp.dot(vc.T.astype(BF16), kd_all[sl].astype(BF16), preferred_element_type=F32)
        yield
        groups = [jnp.zeros((SUB, MIX), F32) for _ in range(C // SUB)]
        off = 0
        for s in range(C):
            g0 = s // SUB
            for g in range(g0, C // SUB):
                groups[g] = groups[g] + attn[off:off + SUB, :] * vc[s:s + 1, :]
                off += SUB
        out.append((jnp.concatenate(groups, axis=0), jnp.where(same_head, outer, 0.0)))

    yield
    n_chunks = tb // C
    intra = [[] for _ in range(n_chunks)]
    chains = [chunk_intra(ch, intra[ch]) for ch in range(n_chunks)]
    next(chains[0])
    for ch in range(n_chunks):
        if ch + 1 < n_chunks:
            next(chains[ch + 1])
        next(chains[ch], None)
        yield

    state = state_ref[...]
    o_chunks = []
    for ch in range(n_chunks):
        sl = slice(ch * C, (ch + 1) * C)
        ((o_intra, outer),) = intra[ch]
        o_chunks.append(o_intra + _bdot_nt(qe_all[sl], state))
        state = state * g_last_all[ch * C:ch * C + 1, :] + outer
        if ch % 2 == 1:
            yield
    state_ref[...] = state
    o = jnp.concatenate(o_chunks, axis=0)
    ms = _head_sum_mxu(o * o, ones_bd) * (1.0 / HEAD_DIM)
    o_ref[...] = o * lax.rsqrt(ms + NORM_EPS) * ng_ref[...] * _silu(og)


def _block_ones():
    hid = jnp.arange(MIX) // HEAD_DIM
    return (hid[:, None] == hid[None, :]).astype(BF16)


def _ret_steps(u_ref, cos_ref, sin_ref, dm_ref, qd_ref, kd_ref, cd_ref, ng_ref, nb_ref,
               ones_ref, o_ref, state_ref):
    c = pl.program_id(1)

    @pl.when(c == 0)
    def _():
        state_ref[...] = jnp.zeros_like(state_ref)

    u = u_ref[...]
    cosf = cos_ref[...]
    sins = sin_ref[...]
    half = HEAD_DIM // 2

    def rope(a):
        outs = []
        for j in range(MIX // 128):
            blk = a[:, j * 128:(j + 1) * 128]
            lane = lax.broadcasted_iota(jnp.int32, blk.shape, 1)
            swapped = jnp.where((lane % HEAD_DIM) < half, pltpu.roll(blk, 128 - half, axis=1),
                                pltpu.roll(blk, half, axis=1))
            outs.append(swapped)
        return a * cosf + jnp.concatenate(outs, axis=-1) * sins

    q = rope(u[:, 0:MIX])
    k = rope(u[:, MIX:2 * MIX]) * (HEAD_DIM ** -0.5)
    v = u[:, 2 * MIX:3 * MIX]
    g = u[:, 3 * MIX:4 * MIX]
    qd = q * qd_ref[...]
    kd = k * kd_ref[...]
    C = u.shape[0]
    yield
    row_head = lax.broadcasted_iota(jnp.int32, (N_HEADS * C, MIX), 0) // C
    lane_head = lax.broadcasted_iota(jnp.int32, (N_HEADS * C, MIX), 1) // HEAD_DIM
    own = row_head == lane_head
    q_stack = jnp.where(own, jnp.concatenate([q] * N_HEADS, axis=0), 0.0)
    s = _bdot_nt(q_stack, k) * dm_ref[...]
    sv = jnp.where(own, _bdot(s, v), 0.0)
    o = sv[0:C]
    for h in range(1, N_HEADS):
        o = o + sv[h * C:(h + 1) * C]
    yield
    state = state_ref[...]
    o = o + _bdot(qd, state)
    kr = lax.broadcasted_iota(jnp.int32, (MIX, MIX), 0) // HEAD_DIM
    kc = lax.broadcasted_iota(jnp.int32, (MIX, MIX), 1) // HEAD_DIM
    state_ref[...] = state * cd_ref[...] + jnp.where(kr == kc, _bdot(kd.T, v), 0.0)
    ones = ones_ref[...]
    mean = _head_sum_mxu(o, ones) * (1.0 / HEAD_DIM)
    oc = o - mean
    var = _head_sum_mxu(oc * oc, ones) * (1.0 / HEAD_DIM)
    y = oc * lax.rsqrt(var + RET_GN_EPS) * ng_ref[...] + nb_ref[...]
    o_ref[...] = y * _silu(g)


def _ret_operands(u, norm_g, norm_b):
    bsz, seq, _ = u.shape
    C = min(REC_TB, seq)
    pos = jnp.arange(seq, dtype=F32)
    inv_freq = RET_ROPE_BASE ** (-jnp.arange(0, HEAD_DIM, 2, dtype=F32) / HEAD_DIM)
    ang = pos[:, None] * inv_freq[None, :]
    cos, sin = jnp.cos(ang), jnp.sin(ang)
    cosf = jnp.tile(jnp.concatenate([cos, cos], axis=-1), (1, N_HEADS))
    sins = jnp.tile(jnp.concatenate([-sin, sin], axis=-1), (1, N_HEADS))
    log_gamma = jnp.log(1.0 - jnp.exp2(-5.0 - jnp.arange(N_HEADS, dtype=F32)))
    i = jnp.arange(C, dtype=F32)
    dpos = i[:, None] - i[None, :]
    dm = jnp.where(dpos >= 0, jnp.exp(jnp.maximum(dpos, 0.0)[None] * log_gamma[:, None, None]), 0.0)
    lanes = lambda t: jnp.repeat(t, HEAD_DIM, axis=-1)
    qd = lanes(jnp.exp((i + 1.0)[:, None] * log_gamma[None, :]))
    kd = lanes(jnp.exp((C - 1.0 - i)[:, None] * log_gamma[None, :]))
    cd = lanes(jnp.exp(C * log_gamma)[None, :])
    args = (u, cosf, sins, dm.reshape(N_HEADS * C, C), qd, kd, cd,
            norm_g.reshape(1, -1).astype(F32), norm_b.reshape(1, -1).astype(F32), _block_ones())
    in_specs = ([_row_spec(C, 4 * MIX),
                 pl.BlockSpec((C, MIX), lambda b, c: (c, 0)),
                 pl.BlockSpec((C, MIX), lambda b, c: (c, 0))]
                + [_const_spec(t.shape) for t in args[3:]])
    return args, in_specs


def _recurrent_kernel(*refs, layer, n_in):
    n_h, n_t, n_r = n_in
    h_in, t_in, r_in = refs[:n_h], refs[n_h:n_h + n_t], refs[n_h + n_t:n_h + n_t + n_r]
    o_h, o_t, o_r, state_h, state_t, state_r, prev_r = refs[n_h + n_t + n_r:]
    streams = [(_rwkv_steps(*r_in, o_r, state_r, prev_r), 0),
               (_hgrn_steps(*h_in, o_h, state_h, layer=layer), REC_HGRN_START),
               (_ret_steps(*t_in, o_t, state_t), REC_RET_START)]
    live, rnd = True, 0
    while live:
        live = False
        for steps, start in streams:
            if rnd >= start:
                live = (next(steps, "done") != "done") or live
            else:
                live = True
        rnd += 1


def _recurrent_call(u_hgrn, lb_logits, hgrn_norm, layer, u_ret, ret_norm_g, ret_norm_b,
                    u_rwkv, *rwkv_params):
    bsz, seq, _ = u_hgrn.shape
    tb = min(REC_TB, seq)
    h_args = (u_hgrn, lb_logits.astype(F32), hgrn_norm.reshape(1, -1).astype(F32), _block_ones())
    t_args, t_specs = _ret_operands(u_ret, ret_norm_g, ret_norm_b)
    vec = lambda t: t.reshape(1, -1).astype(F32)
    mu, w0, w_up, a0, a_up, g_up, k_k, k_a, r_k, norm_g, norm_b = rwkv_params
    r_args = (u_rwkv, vec(mu), vec(w0), w_up, vec(a0), a_up, g_up, vec(k_k), vec(k_a), vec(r_k),
              vec(norm_g), vec(norm_b), _block_ones())
    specs = lambda args: ([_row_spec(tb, 4 * MIX)] + [_const_spec(t.shape) for t in args[1:]])
    out = jax.ShapeDtypeStruct((bsz, seq, MIX), F32)
    state = pltpu.VMEM((MIX, MIX), F32)
    return pl.pallas_call(
        functools.partial(_recurrent_kernel, layer=layer,
                          n_in=(len(h_args), len(t_args), len(r_args))),
        grid=(bsz, seq // tb),
        in_specs=specs(h_args) + t_specs + specs(r_args),
        out_specs=[_row_spec(tb, MIX)] * 3,
        out_shape=[out] * 3,
        scratch_shapes=[state, state, state, pltpu.VMEM((1, 4 * MIX), F32)],
        compiler_params=_params(("parallel", "arbitrary")),
    )(*h_args, *t_args, *r_args)


DENSE_TM = 512
FFN_TF = 256


def _rmsnorm(x, g):
    return x * lax.rsqrt(jnp.mean(x * x, axis=-1, keepdims=True) + NORM_EPS) * g


def _resident(shape):
    nd = len(shape)
    return pl.BlockSpec(shape, lambda b, c: (0,) * nd, pipeline_mode=pl.Buffered(1))


def _inproj_kernel(h_ref, g_ref, w_ref, onehot_ref, q_ref, gate_ref, kc_ref, vc_ref, ks_ref,
                   vs_ref, kw_ref, vw_ref, hg_ref, rt_ref, rw_ref):
    xn = _rmsnorm(h_ref[...], g_ref[...]).astype(BF16)
    nsa = jnp.dot(xn, w_ref[:, 0:NSA_PAD], preferred_element_type=F32)
    tm = nsa.shape[0]
    col = lambda j: nsa[:, MIX + j * HEAD_DIM:MIX + (j + 1) * HEAD_DIM]
    q_ref[...] = nsa[:, 0:MIX]
    kc_ref[...] = col(0)
    vc_ref[...] = col(1)
    gate_ref[...] = nsa[:, MIX + 6 * HEAD_DIM:NSA_PAD]
    ks_ref[...] = jnp.concatenate(
        [onehot_ref[...], col(2).astype(BF16),
         jnp.zeros((tm, ks_ref.shape[1] - NSA_ONEHOT - HEAD_DIM), BF16)], axis=-1)
    kw_ref[...] = col(4).astype(BF16)
    vs_ref[...] = jnp.concatenate([col(3).T, _ones_row_block(tm)], axis=0).astype(BF16)
    vw_t = jnp.concatenate([col(5).T, _ones_row_block(tm)], axis=0).astype(BF16)
    tw = vw_ref.shape[-1]
    for t in range(tm // tw):
        vw_ref[t] = vw_t[:, t * tw:(t + 1) * tw]
    off = NSA_PAD
    for ref in (hg_ref, rt_ref, rw_ref):
        ref[...] = jnp.dot(xn, w_ref[:, off:off + 4 * MIX], preferred_element_type=F32)
        off += 4 * MIX


def _inproj_call(h, g, w_pad):
    bsz, seq, _ = h.shape
    tm = min(NSA_TK, seq)
    tw = min(NSA_TQ, seq)
    local = jnp.arange(tm) // NSA_SEL_BLOCK
    onehot = (local[:, None] == jnp.arange(NSA_ONEHOT)[None, :]).astype(BF16)
    struct = jax.ShapeDtypeStruct
    outs = [
        (struct((bsz, seq, MIX), F32), _row_spec(tm, MIX)),
        (struct((bsz, seq, 128), F32), _row_spec(tm, 128)),
        (struct((bsz, seq, HEAD_DIM), F32), _row_spec(tm, HEAD_DIM)),
        (struct((bsz, seq, HEAD_DIM), F32), _row_spec(tm, HEAD_DIM)),
        (struct((bsz, seq, 128), BF16), _row_spec(tm, 128)),
        (struct((bsz, seq // tm, NSA_VROWS, tm), BF16),
         pl.BlockSpec((None, None, NSA_VROWS, tm), lambda b, c: (b, c, 0, 0))),
        (struct((bsz, seq, HEAD_DIM), BF16), _row_spec(tm, HEAD_DIM)),
        (struct((bsz, seq // tw, NSA_VROWS, tw), BF16),
         pl.BlockSpec((None, tm // tw, NSA_VROWS, tw), lambda b, c: (b, c, 0, 0))),
    ] + [(struct((bsz, seq, 4 * MIX), F32), _row_spec(tm, 4 * MIX))] * 3
    res = pl.pallas_call(
        _inproj_kernel,
        grid=(bsz, seq // tm),
        in_specs=[_row_spec(tm, D_MODEL), _resident((1, D_MODEL)), _resident(w_pad.shape),
                  _resident(onehot.shape)],
        out_specs=[spec for _, spec in outs],
        out_shape=[shape for shape, _ in outs],
        compiler_params=_params(("parallel", "parallel")),
    )(h, g.reshape(1, -1), w_pad, onehot)
    return tuple(res[:8]), res[8], res[9], res[10]


def _merge_kernel(h_ref, b0_ref, b1_ref, b2_ref, b3_ref, g_ref, wg_ref, bg_ref, wb_ref, wo_ref,
                  o_ref):
    h = h_ref[...]
    xn = _rmsnorm(h, g_ref[...]).astype(BF16)
    merged = None
    for m, b_ref in enumerate((b0_ref, b1_ref, b2_ref, b3_ref)):
        gate = _sigmoid(jnp.dot(xn, wg_ref[m], preferred_element_type=F32) + bg_ref[m])
        term = gate * _bdot(b_ref[...], wb_ref[m])
        merged = term if merged is None else merged + term
    o_ref[...] = h + _bdot(merged, wo_ref[...])


def _merge_call(h, branches, g, w_gate, b_gate, w_branch, w_out):
    bsz, seq, _ = h.shape
    tm = min(DENSE_TM, seq)
    consts = (g.reshape(1, -1), w_gate, b_gate.reshape(4, 1, D_MODEL), w_branch, w_out)
    return pl.pallas_call(
        _merge_kernel,
        grid=(bsz, seq // tm),
        in_specs=([_row_spec(tm, D_MODEL)] + [_row_spec(tm, MIX)] * 4
                  + [_resident(t.shape) for t in consts]),
        out_specs=_row_spec(tm, D_MODEL),
        out_shape=jax.ShapeDtypeStruct(h.shape, F32),
        compiler_params=_params(("parallel", "parallel")),
    )(h, *branches, *consts)


def _ffn_kernel(h_ref, g_ref, wg_ref, wu_ref, wd_ref, o_ref):
    h = h_ref[...]
    hn = _rmsnorm(h, g_ref[...]).astype(BF16)
    acc = h
    for f in range(0, D_FF, FFN_TF):
        gate = jnp.dot(hn, wg_ref[:, f:f + FFN_TF], preferred_element_type=F32)
        up = jnp.dot(hn, wu_ref[:, f:f + FFN_TF], preferred_element_type=F32)
        acc = acc + _bdot(_silu(gate) * up, wd_ref[f:f + FFN_TF, :])
    o_ref[...] = acc


def _ffn_call(h, g, w_gate, w_up, w_down):
    bsz, seq, _ = h.shape
    tm = min(DENSE_TM, seq)
    consts = (g.reshape(1, -1), w_gate, w_up, w_down)
    return pl.pallas_call(
        _ffn_kernel,
        grid=(bsz, seq // tm),
        in_specs=[_row_spec(tm, D_MODEL)] + [_resident(t.shape) for t in consts],
        out_specs=_row_spec(tm, D_MODEL),
        out_shape=jax.ShapeDtypeStruct(h.shape, F32),
        compiler_params=_params(("parallel", "parallel")),
    )(h, *consts)


def _ple_kernel(h_ref, p_ref, g_ref, wg_ref, wp_ref, gf_ref, o_ref, *, final_norm):
    h = h_ref[...]
    hp = _rmsnorm(h, g_ref[...])
    out = h + _sigmoid(_bdot(hp, wg_ref[...])) * _bdot(p_ref[...], wp_ref[...])
    if final_norm:
        out = _rmsnorm(out, gf_ref[...])
    o_ref[...] = out


def _ple_call(h, p, layer, g, w_gate, w_proj, g_final, final_norm):
    bsz, seq, _ = h.shape
    tm = min(DENSE_TM, seq)
    consts = (g.reshape(1, -1), w_gate, w_proj, g_final.reshape(1, -1))
    return pl.pallas_call(
        functools.partial(_ple_kernel, final_norm=final_norm),
        grid=(bsz, seq // tm),
        in_specs=([_row_spec(tm, D_MODEL),
                   pl.BlockSpec((None, None, tm, PLE_DIM), lambda b, c: (layer, b, c, 0))]
                  + [_resident(t.shape) for t in consts]),
        out_specs=_row_spec(tm, D_MODEL),
        out_shape=jax.ShapeDtypeStruct(h.shape, F32),
        compiler_params=_params(("parallel", "parallel")),
    )(h, p, *consts)


def kernel(x, p, norm_mix, w_in, nsa_pos_k, nsa_pos_v, nsa_cmp_k1, nsa_cmp_k2, nsa_cmp_v1,
           nsa_cmp_v2, hgrn_lb_logits, hgrn_norm, ret_norm_g, ret_norm_b, rwkv_mu, rwkv_w0,
           rwkv_w_up, rwkv_a0, rwkv_a_up, rwkv_g_up, rwkv_k_k, rwkv_k_a, rwkv_r_k, rwkv_norm_g,
           rwkv_norm_b, w_branch, w_gate, b_gate, w_out, norm_ffn, w_ffn_gate, w_ffn_up,
           w_ffn_down, norm_ple, w_ple_gate, w_ple_proj, norm_final):
    depth = w_in.shape[0]
    w_in_pad = jnp.concatenate(
        [w_in[:, :, :NSA_WIDTH], jnp.zeros((depth, D_MODEL, NSA_PAD - NSA_WIDTH), w_in.dtype),
         w_in[:, :, NSA_WIDTH:]], axis=-1).astype(BF16)
    bf = lambda t: t.astype(BF16)
    h = x
    for i in range(depth):
        nsa_in, u_hgrn, u_ret, u_rwkv = _inproj_call(h, norm_mix[i], w_in_pad[i])
        o_hgrn, o_ret, o_rwkv = _recurrent_call(
            u_hgrn, hgrn_lb_logits, hgrn_norm[i], i, u_ret, ret_norm_g[i], ret_norm_b[i],
            u_rwkv, rwkv_mu[i], rwkv_w0[i], rwkv_w_up[i], rwkv_a0[i], rwkv_a_up[i], rwkv_g_up[i],
            rwkv_k_k[i], rwkv_k_a[i], rwkv_r_k[i], rwkv_norm_g[i], rwkv_norm_b[i])
        branches = (
            _nsa_call(*nsa_in, nsa_pos_k[i], nsa_pos_v[i], nsa_cmp_k1[i], nsa_cmp_k2[i],
                      nsa_cmp_v1[i], nsa_cmp_v2[i]),
            o_hgrn, o_ret, o_rwkv,
        )
        h = _merge_call(h, branches, norm_mix[i], bf(w_gate[i]), b_gate[i], bf(w_branch[i]),
                        bf(w_out[i]))
        h = _ffn_call(h, norm_ffn[i], bf(w_ffn_gate[i]), bf(w_ffn_up[i]), bf(w_ffn_down[i]))
        h = _ple_call(h, p, i, norm_ple[i], bf(w_ple_gate[i]), bf(w_ple_proj[i]), norm_final,
                      final_norm=(i == depth - 1))
    return h
```
